```python
import math
import jax, jax.numpy as jnp
from jax import lax
import numpy as np

D_MODEL = 1024
BATCH = 32
SEQ = 256
DEPTH = 4
DEC_BATCH = 2
DEC_SEQ = 2048
PAST_LEN = 256

GRID_W = 64
N_MIXERS = 2
N_A_LAYERS = (DEPTH + 1) // 2
N_B_LAYERS = DEPTH // 2
A_HEAD_DIM = 128
A_HEADS = D_MODEL // A_HEAD_DIM
A_CHUNK = 16
B_HEAD_DIM = 64
B_HEADS = D_MODEL // B_HEAD_DIM
B_LORA_W = max(32, int(round(1.8 * D_MODEL ** 0.5 / 32)) * 32)
B_LORA_A = max(32, int(round(1.8 * D_MODEL ** 0.5 / 32)) * 32)
B_LORA_G = max(32, int(round(0.6 * D_MODEL ** 0.8 / 32)) * 32)
D_FF = 4 * D_MODEL
DN_ALPHA = (2 * DEPTH) ** 0.25
DN_BETA = (8 * DEPTH) ** -0.25
LN_EPS = 1e-5
RMS_EPS = 1e-6
GN_EPS = 64e-5
DECAY_SCALE = 0.606531
POS_BASE = 10000.0
EXP_CLIP = 80.0

kernel_name = 'hgrn2_rwkv7_bidir_diffusion_step'


def layer_norm(x, g, b):
    xf = x.astype(jnp.float32)
    mu = jnp.mean(xf, -1, keepdims=True)
    var = jnp.mean(jnp.square(xf - mu), -1, keepdims=True)
    y = (xf - mu) * lax.rsqrt(var + LN_EPS) * g.astype(jnp.float32) + b.astype(jnp.float32)
    return y.astype(x.dtype)


def grid_pos_embed(n_tokens, dtype):
    rows = n_tokens // GRID_W
    quarter = D_MODEL // 4
    half = D_MODEL // 2
    omega = 1.0 / (POS_BASE ** (jnp.arange(quarter, dtype=jnp.float32) / quarter))
    r = jnp.arange(rows, dtype=jnp.float32)[:, None] * omega
    cc = jnp.arange(GRID_W, dtype=jnp.float32)[:, None] * omega
    row_emb = jnp.concatenate([jnp.sin(r), jnp.cos(r)], -1)
    col_emb = jnp.concatenate([jnp.sin(cc), jnp.cos(cc)], -1)
    emb = jnp.concatenate([
        jnp.broadcast_to(row_emb[:, None, :], (rows, GRID_W, half)),
        jnp.broadcast_to(col_emb[None, :, :], (rows, GRID_W, half))], -1)
    return emb.reshape(rows * GRID_W, D_MODEL).astype(dtype)


def adaln(cond, w, b):
    m = jax.nn.silu(cond) @ w + b
    return jnp.split(m.reshape(-1, 1, 6 * D_MODEL), 6, axis=-1)


def squared_relu_mlp(h, w_up, w_down):
    return jnp.square(jax.nn.relu(h @ w_up)) @ w_down


def hgrn2_mixer(h, w_in, lb, norm_g, w_o, s0):
    f32 = jnp.float32
    bsz, seq_len, _ = h.shape
    H, K, C = A_HEADS, A_HEAD_DIM, A_CHUNK
    n_chunks = seq_len // C
    proj = (h @ w_in).astype(f32)
    qfi = proj[..., :6 * D_MODEL].reshape(bsz, seq_len, 2, 3, H, K)
    gate = proj[..., 6 * D_MODEL:]
    q = jax.nn.silu(qfi[:, :, :, 0])
    z = qfi[:, :, :, 1]
    v = qfi[:, :, :, 2]
    lbh = lb.astype(f32).reshape(2, H, K)
    log_f = jax.nn.log_sigmoid(z) + jnp.log1p(lbh * jnp.exp(jnp.minimum(-z, EXP_CLIP)))
    k = (1.0 - lbh) * jax.nn.sigmoid(-z)

    def to_chunks(t):
        t = jnp.stack([t[:, :, 0], jnp.flip(t[:, :, 1], axis=1)], axis=1)
        return t.reshape(bsz, 2, n_chunks, C, H, K).transpose(0, 1, 4, 2, 3, 5)

    q, k, v, log_f = to_chunks(q), to_chunks(k), to_chunks(v), to_chunks(log_f)
    cum = jnp.cumsum(log_f, axis=-2)
    causal = jnp.tril(jnp.ones((C, C), dtype=bool))[:, :, None]
    rel = cum[..., :, None, :] - cum[..., None, :, :]
    pair_decay = jnp.where(causal, jnp.exp(jnp.where(causal, rel, 0.0)), 0.0)
    scores = jnp.einsum('bzhntk,bzhnsk,bzhntsk->bzhnts', q, k, pair_decay)
    o_intra = jnp.einsum('bzhnts,bzhnsv->bzhntv', scores, v)
    q_in = q * jnp.exp(cum)
    k_out = k * jnp.exp(cum[..., -1:, :] - cum)
    u = jnp.einsum('bzhnsk,bzhnsv->bzhnkv', k_out, v)
    chunk_decay = jnp.exp(cum[..., -1, :])

    def chunk_step(s, inp):
        q_n, d_n, u_n = inp
        o_n = jnp.einsum('bzhtk,bzhkv->bzhtv', q_n, s)
        return d_n[..., None] * s + u_n, o_n

    xs = (jnp.moveaxis(q_in, 3, 0), jnp.moveaxis(chunk_decay, 3, 0), jnp.moveaxis(u, 3, 0))
    s_fin, o_inter = lax.scan(chunk_step, s0.astype(f32), xs)
    o = o_intra + jnp.moveaxis(o_inter, 0, 3)
    o = o.transpose(0, 1, 3, 4, 2, 5).reshape(bsz, 2, seq_len, H, K)
    o = o[:, 0] + jnp.flip(o[:, 1], axis=1)
    o = o * lax.rsqrt(jnp.mean(o * o, -1, keepdims=True) + RMS_EPS)
    o = o.reshape(bsz, seq_len, D_MODEL) * norm_g.astype(f32) * jax.nn.silu(gate)
    return o.astype(h.dtype) @ w_o, s_fin.astype(h.dtype)


def rwkv7_mixer(h, mu, w_rkv, w0, w_la, w_lb, a0, a_la, a_lb, g_la, g_lb,
                k_k, k_a, r_k, gn_g, gn_b, w_o, s0):
    f32 = jnp.float32
    bsz, seq_len, _ = h.shape
    H, N = B_HEADS, B_HEAD_DIM
    zeros = jnp.zeros_like(h[:, :1])
    nbr = 0.5 * (jnp.concatenate([zeros, h[:, :-1]], 1) + jnp.concatenate([h[:, 1:], zeros], 1))
    xx = nbr - h
    xs = h[None] + xx[None] * mu[:, None, None, :]
    rkv = jnp.einsum('nbld,nde->nble', jnp.stack([xs[0], xs[2], xs[3]]), w_rkv)
    rkv = rkv.astype(f32).reshape(3, bsz, seq_len, 2, H, N)
    r, k, v = rkv[0], rkv[1], rkv[2]
    zw = w0 + jnp.einsum('blzr,zre->blze', jnp.tanh(jnp.einsum('bld,dzr->blzr', xs[1], w_la)), w_lb)
    decay = jnp.exp(-DECAY_SCALE * jax.nn.sigmoid(zw.astype(f32))).reshape(bsz, seq_len, 2, H, N)
    za = a0 + jnp.einsum('blzr,zre->blze', jnp.einsum('bld,dzr->blzr', xs[4], a_la), a_lb)
    a = jax.nn.sigmoid(za.astype(f32)).reshape(bsz, seq_len, 2, H, N)
    g = (jax.nn.sigmoid(xs[5] @ g_la) @ g_lb).astype(f32)
    kk = k * k_k.astype(f32).reshape(2, H, N)
    kk = kk / jnp.maximum(jnp.sqrt(jnp.sum(kk * kk, -1, keepdims=True)), 1e-12)
    k = k * (1.0 + (a - 1.0) * k_a.astype(f32).reshape(2, H, N))
    bonus = jnp.sum(r * k * r_k.astype(f32).reshape(2, H, N), -1, keepdims=True) * v

    def to_scan(t):
        return jnp.moveaxis(jnp.stack([t[:, :, 0], jnp.flip(t[:, :, 1], axis=1)], axis=2), 1, 0)

    def step(s, inp):
        r_t, w_t, k_t, v_t, kk_t, a_t = inp
        sa = jnp.einsum('bzhvk,bzhk->bzhv', s, kk_t)
        s = s * w_t[..., None, :] - sa[..., None] * (kk_t * a_t)[..., None, :] + v_t[..., None] * k_t[..., None, :]
        return s, jnp.einsum('bzhvk,bzhk->bzhv', s, r_t)

    s_fin, y = lax.scan(step, s0.astype(f32),
                        (to_scan(r), to_scan(decay), to_scan(k), to_scan(v), to_scan(kk), to_scan(a)))
    y = jnp.moveaxis(y, 0, 1)
    y = jnp.stack([y[:, :, 0], jnp.flip(y[:, :, 1], axis=1)], axis=2)
    m = jnp.mean(y, -1, keepdims=True)
    var = jnp.mean(jnp.square(y - m), -1, keepdims=True)
    y = (y - m) * lax.rsqrt(var + GN_EPS) * gn_g.astype(f32).reshape(2, H, N) + gn_b.astype(f32).reshape(2, H, N)
    y = y + bonus
    y = (y[:, :, 0] + y[:, :, 1]).reshape(bsz, seq_len, D_MODEL) * g
    return y.astype(h.dtype) @ w_o, s_fin.astype(h.dtype)


def trunk_layer(x, mod, mixer, s0, ln_g, ln_b, w_up, w_down):
    sh1, sc1, g1, sh2, sc2, g2 = mod
    y, s_fin = mixer(x * (1.0 + sc1) + sh1, s0)
    x = layer_norm(DN_ALPHA * x + g1 * y, ln_g[0], ln_b[0])
    y = squared_relu_mlp(x * (1.0 + sc2) + sh2, w_up, w_down)
    x = layer_norm(DN_ALPHA * x + g2 * y, ln_g[1], ln_b[1])
    return x, s_fin


def setup_inputs(seed: int = 0) -> dict:
    key = jax.random.key(seed)
    ks = iter(jax.random.split(key, 40))
    D = D_MODEL

    def nrm(shape, scale):
        return jax.random.normal(next(ks), shape, jnp.float32) * scale

    return {
        'x_prompt': nrm((BATCH, SEQ, D), 1.0),
        'x_sample': nrm((DEC_BATCH, DEC_SEQ, D), 1.0),
        'state_hgrn': nrm((DEC_BATCH, N_A_LAYERS, 2, A_HEADS, A_HEAD_DIM, A_HEAD_DIM), 0.5),
        'state_rwkv': nrm((DEC_BATCH, N_B_LAYERS, 2, B_HEADS, B_HEAD_DIM, B_HEAD_DIM), 0.3),
        'c': nrm((DEC_BATCH, D), 1.0),
        'c_ctx': nrm((D,), 1.0),
        'ada_w': nrm((DEPTH, D, 6 * D), D ** -0.5),
        'ada_b': nrm((DEPTH, 6 * D), 0.02),
        'ln_g': 1.0 + nrm((DEPTH, 2, D), 0.02),
        'ln_b': nrm((DEPTH, 2, D), 0.02),
        'ffn_w_up': nrm((DEPTH, D, D_FF), D ** -0.5),
        'ffn_w_down': nrm((DEPTH, D_FF, D), DN_BETA * D_FF ** -0.5),
        'hgrn_w_in': nrm((N_A_LAYERS, D, 7 * D), D ** -0.5),
        'hgrn_lb': nrm((N_A_LAYERS, 2, D), 0.5),
        'hgrn_norm_g': 1.0 + nrm((N_A_LAYERS, D), 0.02),
        'hgrn_w_o': nrm((N_A_LAYERS, D, D), DN_BETA * D ** -0.5),
        'rwkv_mu': jax.random.uniform(next(ks), (N_B_LAYERS, 6, D), jnp.float32),
        'rwkv_w_rkv': nrm((N_B_LAYERS, 3, D, 2 * D), D ** -0.5),
        'rwkv_w0': nrm((N_B_LAYERS, 2, D), 0.5),
        'rwkv_w_la': nrm((N_B_LAYERS, D, 2, B_LORA_W), D ** -0.5),
        'rwkv_w_lb': nrm((N_B_LAYERS, 2, B_LORA_W, D), 0.5 * B_LORA_W ** -0.5),
        'rwkv_a0': nrm((N_B_LAYERS, 2, D), 0.2),
        'rwkv_a_la': nrm((N_B_LAYERS, D, 2, B_LORA_A), D ** -0.5),
        'rwkv_a_lb': nrm((N_B_LAYERS, 2, B_LORA_A, D), 0.5 * B_LORA_A ** -0.5),
        'rwkv_g_la': nrm((N_B_LAYERS, D, B_LORA_G), D ** -0.5),
        'rwkv_g_lb': nrm((N_B_LAYERS, B_LORA_G, D), B_LORA_G ** -0.5),
        'rwkv_k_k': 0.85 + nrm((N_B_LAYERS, 2, D), 0.05),
        'rwkv_k_a': 1.0 + nrm((N_B_LAYERS, 2, D), 0.05),
        'rwkv_r_k': nrm((N_B_LAYERS, 2, D), 0.1),
        'rwkv_gn_g': 1.0 + nrm((N_B_LAYERS, 2, D), 0.02),
        'rwkv_gn_b': nrm((N_B_LAYERS, 2, D), 0.02),
        'rwkv_w_o': nrm((N_B_LAYERS, D, D), DN_BETA * D ** -0.5),
    }


def reference(x_prompt, x_sample, state_hgrn, state_rwkv, c, c_ctx, ada_w, ada_b, ln_g, ln_b,
              ffn_w_up, ffn_w_down, hgrn_w_in, hgrn_lb, hgrn_norm_g, hgrn_w_o,
              rwkv_mu, rwkv_w_rkv, rwkv_w0, rwkv_w_la, rwkv_w_lb, rwkv_a0, rwkv_a_la, rwkv_a_lb,
              rwkv_g_la, rwkv_g_lb, rwkv_k_k, rwkv_k_a, rwkv_r_k, rwkv_gn_g, rwkv_gn_b, rwkv_w_o):
    lb_soft = jax.nn.softmax(hgrn_lb.astype(jnp.float32), axis=0)
    lower_bounds = jnp.cumsum(lb_soft, axis=0) - lb_soft[0]
    x_p = x_prompt
    x_s = x_sample + grid_pos_embed(x_sample.shape[1], x_sample.dtype)
    n_ctx = x_prompt.shape[0]
    new_hgrn = []
    new_rwkv = []
    for l in range(DEPTH):
        j = l // N_MIXERS
        if l % N_MIXERS == 0:
            def mixer(h, s0, j=j):
                return hgrn2_mixer(h, hgrn_w_in[j], lower_bounds[j], hgrn_norm_g[j], hgrn_w_o[j], s0)
            s0_ctx = jnp.zeros((n_ctx, 2, A_HEADS, A_HEAD_DIM, A_HEAD_DIM), x_prompt.dtype)
            s0_lat = state_hgrn[:, j]
        else:
            def mixer(h, s0, j=j):
                return rwkv7_mixer(h, rwkv_mu[j], rwkv_w_rkv[j], rwkv_w0[j], rwkv_w_la[j], rwkv_w_lb[j],
                                   rwkv_a0[j], rwkv_a_la[j], rwkv_a_lb[j], rwkv_g_la[j], rwkv_g_lb[j],
                                   rwkv_k_k[j], rwkv_k_a[j], rwkv_r_k[j], rwkv_gn_g[j], rwkv_gn_b[j],
                                   rwkv_w_o[j], s0)
            s0_ctx = jnp.zeros((n_ctx, 2, B_HEADS, B_HEAD_DIM, B_HEAD_DIM), x_prompt.dtype)
            s0_lat = state_rwkv[:, j]
        mod_p = adaln(c_ctx, ada_w[l], ada_b[l])
        mod_s = adaln(c, ada_w[l], ada_b[l])
        x_p, s_ctx = trunk_layer(x_p, mod_p, mixer, s0_ctx, ln_g[l], ln_b[l], ffn_w_up[l], ffn_w_down[l])
        x_s, _ = trunk_layer(x_s, mod_s, mixer, s0_lat, ln_g[l], ln_b[l], ffn_w_up[l], ffn_w_down[l])
        if l % N_MIXERS == 0:
            new_hgrn.append(s_ctx)
        else:
            new_rwkv.append(s_ctx)
    new_state_hgrn = jnp.stack(new_hgrn, axis=1)
    new_state_rwkv = jnp.stack(new_rwkv, axis=1)
    return (x_p, x_s, new_state_hgrn, new_state_rwkv)
```

```python
import functools

import jax
import jax.numpy as jnp
from jax import lax
from jax.experimental import pallas as pl
from jax.experimental.pallas import tpu as pltpu

F32 = jnp.float32
BF16 = jnp.bfloat16
HIGHEST = lax.Precision.HIGHEST

LN_EPS = 1e-5
RMS_EPS = 1e-6
GN_EPS = 64e-5
DECAY_SCALE = 0.606531
EXP_CLIP = 80.0
POS_BASE = 10000.0
GRID_W = 64

LANES = 128
HGRN_CHUNK = 16
RWKV_HEAD = 64
VMEM_LIMIT = 56 * 1024 * 1024

NT_DIMS = (((1,), (1,)), ((), ()))
TN_DIMS = (((0,), (0,)), ((), ()))


def _dot(a, b, dims=None, precision=None):
    if dims is None:
        return jnp.dot(a, b, preferred_element_type=F32, precision=precision)
    return lax.dot_general(a, b, dims, preferred_element_type=F32, precision=precision)


def _bdot(a, b, dims=None):
    return _dot(a.astype(BF16), b.astype(BF16), dims)


def _hdot(a, b, dims=None):
    return _dot(a, b, dims, precision=HIGHEST)


def _sigmoid(x):
    return jax.nn.sigmoid(x)


def _layer_norm(x, g, b):
    mu = jnp.mean(x, -1, keepdims=True)
    xc = x - mu
    var = jnp.mean(xc * xc, -1, keepdims=True)
    return xc * lax.rsqrt(var + LN_EPS) * g + b


def _seg_index(i, tm, n_prompt_rows, sample_len):
    start = i * tm
    return jnp.where(start < n_prompt_rows, 0, 1 + (start - n_prompt_rows) // sample_len)


def _adaln_kernel(c_ref, w_ref, b_ref, o_ref):
    c = c_ref[...]
    s = c * _sigmoid(c)
    o_ref[...] = _bdot(s, w_ref[...]) + b_ref[...]


def _adaln(cond8, ada_w, ada_b, tn=1536):
    depth, d, n = ada_w.shape
    return pl.pallas_call(
        _adaln_kernel,
        grid=(depth, n // tn),
        in_specs=[
            pl.BlockSpec((8, d), lambda l, j: (0, 0)),
            pl.BlockSpec((None, d, tn), lambda l, j: (l, 0, j)),
            pl.BlockSpec((None, 1, tn), lambda l, j: (l, 0, j)),
        ],
        out_specs=pl.BlockSpec((None, 8, tn), lambda l, j: (l, 0, j)),
        out_shape=jax.ShapeDtypeStruct((depth, 8, n), F32),
        compiler_params=pltpu.CompilerParams(
            dimension_semantics=("parallel", "parallel"), vmem_limit_bytes=VMEM_LIMIT),
        name="adaln",
    )(cond8, ada_w, ada_b.reshape(depth, 1, n))


def _modmm_kernel(x_ref, mod_ref, w_ref, o_ref, xb_ref):
    @pl.when(pl.program_id(1) == 0)
    def _():
        m = mod_ref[...]
        xb_ref[...] = (x_ref[...] * (1.0 + m[1:2]) + m[0:1]).astype(BF16)

    o_ref[...] = _dot(xb_ref[...], w_ref[...])


def _modmm(x, mods, w, seg, tm=512, tn=1024):
    t, d = x.shape
    n = w.shape[1]
    return pl.pallas_call(
        _modmm_kernel,
        grid=(t // tm, n // tn),
        in_specs=[
            pl.BlockSpec((tm, d), lambda i, j: (i, 0)),
            pl.BlockSpec((None, 6, d), lambda i, j: (seg(i, tm), 0, 0)),
            pl.BlockSpec((d, tn), lambda i, j: (0, j)),
        ],
        out_specs=pl.BlockSpec((tm, tn), lambda i, j: (i, j)),
        out_shape=jax.ShapeDtypeStruct((t, n), F32),
        scratch_shapes=[pltpu.VMEM((tm, d), BF16)],
        compiler_params=pltpu.CompilerParams(
            dimension_semantics=("parallel", "arbitrary"), vmem_limit_bytes=VMEM_LIMIT),
        name="modmm",
    )(x, mods, w)


def _out_ln_kernel(o_ref, x_ref, mod_ref, w_ref, g_ref, b_ref, y_ref, *, alpha):
    y = _dot(o_ref[...], w_ref[...])
    gate = mod_ref[...][2:3]
    y_ref[...] = _layer_norm(alpha * x_ref[...] + gate * y, g_ref[...], b_ref[...])


def _out_ln(o, x, mods, w, g, b, seg, alpha, tm=512):
    t, d = x.shape
    return pl.pallas_call(
        functools.partial(_out_ln_kernel, alpha=alpha),
        grid=(t // tm,),
        in_specs=[
            pl.BlockSpec((tm, d), lambda i: (i, 0)),
            pl.BlockSpec((tm, d), lambda i: (i, 0)),
            pl.BlockSpec((None, 6, d), lambda i: (seg(i, tm), 0, 0)),
            pl.BlockSpec((d, d), lambda i: (0, 0)),
            pl.BlockSpec((1, d), lambda i: (0, 0)),
            pl.BlockSpec((1, d), lambda i: (0, 0)),
        ],
        out_specs=pl.BlockSpec((tm, d), lambda i: (i, 0)),
        out_shape=jax.ShapeDtypeStruct((t, d), F32),
        compiler_params=pltpu.CompilerParams(
            dimension_semantics=("parallel",), vmem_limit_bytes=VMEM_LIMIT),
        name="out_ln",
    )(o, x, mods, w, g, b)


def _mlp_ln_kernel(x_ref, mod_ref, wu_ref, wd_ref, g_ref, b_ref, y_ref, xb_ref, acc_ref, *, alpha):
    f = pl.program_id(1)

    @pl.when(f == 0)
    def _():
        m = mod_ref[...]
        xb_ref[...] = (x_ref[...] * (1.0 + m[4:5]) + m[3:4]).astype(BF16)
        acc_ref[...] = jnp.zeros_like(acc_ref)

    h = jnp.maximum(_dot(xb_ref[...], wu_ref[...]), 0.0)
    acc_ref[...] += _dot((h * h).astype(BF16), wd_ref[...])

    @pl.when(f == pl.num_programs(1) - 1)
    def _():
        gate = mod_ref[...][5:6]
        y_ref[...] = _layer_norm(alpha * x_ref[...] + gate * acc_ref[...], g_ref[...], b_ref[...])


def _mlp_ln(x, mods, w_up, w_down, g, b, seg, alpha, tm=512, tf=1024):
    t, d = x.shape
    dff = w_up.shape[1]
    return pl.pallas_call(
        functools.partial(_mlp_ln_kernel, alpha=alpha),
        grid=(t // tm, dff // tf),
        in_specs=[
            pl.BlockSpec((tm, d), lambda i, f: (i, 0)),
            pl.BlockSpec((None, 6, d), lambda i, f: (seg(i, tm), 0, 0)),
            pl.BlockSpec((d, tf), lambda i, f: (0, f)),
            pl.BlockSpec((tf, d), lambda i, f: (f, 0)),
            pl.BlockSpec((1, d), lambda i, f: (0, 0)),
            pl.BlockSpec((1, d), lambda i, f: (0, 0)),
        ],
        out_specs=pl.BlockSpec((tm, d), lambda i, f: (i, 0)),
        out_shape=jax.ShapeDtypeStruct((t, d), F32),
        scratch_shapes=[pltpu.VMEM((tm, d), BF16), pltpu.VMEM((tm, d), F32)],
        compiler_params=pltpu.CompilerParams(
            dimension_semantics=("parallel", "arbitrary"), vmem_limit_bytes=VMEM_LIMIT),
        name="mlp_ln",
    )(x, mods, w_up, w_down, g, b)


def _hgrn_tile(q_ref, z_ref, v_ref, lbz, rows, consts, fwd):
    r_i, c_i, tri, blk = consts
    pos = r_i & (HGRN_CHUNK - 1)
    qr = q_ref[rows, :]
    zz = z_ref[rows, :]
    v = v_ref[rows, :]
    q = qr * _sigmoid(qr)
    log_f = (jnp.minimum(zz, 0.0) - jnp.log1p(jnp.exp(-jnp.abs(zz)))
             + jnp.log1p(lbz * jnp.exp(jnp.minimum(-zz, EXP_CLIP))))
    k = (1.0 - lbz) * _sigmoid(-zz)
    cum = _hdot(tri, log_f)
    ctot = _hdot(blk, log_f)
    q_in = q * jnp.exp(cum)
    k_out = k * jnp.exp(ctot - cum)
    dec = jnp.exp(ctot)

    scores = jnp.zeros((LANES, LANES), F32)
    for d in range(HGRN_CHUNK):
        if d == 0:
            kr, cr = k, cum
        else:
            sh = d if fwd else LANES - d
            kr = pltpu.roll(k, sh, 0)
            cr = pltpu.roll(cum, sh, 0)
        valid = (pos >= d) if fwd else (pos <= HGRN_CHUNK - 1 - d)
        e = jnp.exp(jnp.where(valid, cum - cr, 0.0))
        s = jnp.sum(q * kr * e, axis=-1, keepdims=True)
        tgt = (c_i == r_i - d) if fwd else (c_i == r_i + d)
        scores = scores + jnp.where(tgt & valid, s, 0.0)
    o_intra = _bdot(scores, v)

    v_t = v.T
    chunk_of_lane = c_i >> 4
    lhs = jnp.concatenate(
        [jnp.where(chunk_of_lane == c, v_t, 0.0) for c in range(LANES // HGRN_CHUNK)], axis=0)
    upd = _bdot(lhs, k_out)
    return q_in, o_intra, upd, dec


def _hgrn_rec_kernel(*refs, seq_len, zero_init, want_state, aliased):
    qf_ref, zf_ref, vf_ref, qb_ref, zb_ref, vb_ref, gate_ref, lb_ref, ng_ref = refs[:9]
    pos = 9
    s0_ref = None
    if not zero_init:
        s0_ref = refs[pos]
        pos += 1
    if aliased:
        pos += 1
    o_ref = refs[pos]
    pos += 1
    sfin_ref = None
    if want_state:
        sfin_ref = refs[pos]
        pos += 1
    osum_ref, st_ref = refs[pos:pos + 2]

    n_tiles = seq_len // LANES
    n_chunks = LANES // HGRN_CHUNK
    r_i = lax.broadcasted_iota(jnp.int32, (LANES, LANES), 0)
    c_i = lax.broadcasted_iota(jnp.int32, (LANES, LANES), 1)
    same = (r_i >> 4) == (c_i >> 4)
    blk = same.astype(F32)
    tri = ((same & (c_i <= r_i)).astype(F32), (same & (c_i >= r_i)).astype(F32))
    lb = lb_ref[...]

    for z in range(2):
        if zero_init:
            st_ref[z] = jnp.zeros((LANES, LANES), F32)
        else:
            st_ref[z] = s0_ref[z].T

    def fwd_body(i, carry):
        rows = pl.ds(pl.multiple_of(i * LANES, LANES), LANES)
        q_in, o_intra, upd, dec = _hgrn_tile(
            qf_ref, zf_ref, vf_ref, lb[0:1], rows, (r_i, c_i, tri[0], blk), True)
        st = st_ref[0]
        outs = [None] * n_chunks
        for c in range(n_chunks):
            lo = c * HGRN_CHUNK
            outs[c] = _bdot(q_in[lo:lo + HGRN_CHUNK], st, NT_DIMS)
            st = st * dec[lo:lo + 1] + upd[c * LANES:(c + 1) * LANES]
        st_ref[0] = st
        osum_ref[rows, :] = o_intra + jnp.concatenate(outs, axis=0)
        return carry

    def bwd_body(i, carry):
        ti = n_tiles - 1 - i
        rows = pl.ds(pl.multiple_of(ti * LANES, LANES), LANES)
        q_in, o_intra, upd, dec = _hgrn_tile(
            qb_ref, zb_ref, vb_ref, lb[1:2], rows, (r_i, c_i, tri[1], blk), False)
        st = st_ref[1]
        outs = [None] * n_chunks
        for c in range(n_chunks - 1, -1, -1):
            lo = c * HGRN_CHUNK
            outs[c] = _bdot(q_in[lo:lo + HGRN_CHUNK], st, NT_DIMS)
            st = st * dec[lo:lo + 1] + upd[c * LANES:(c + 1) * LANES]
        st_ref[1] = st
        o = osum_ref[rows, :] + o_intra + jnp.concatenate(outs, axis=0)
        o = o * lax.rsqrt(jnp.mean(o * o, -1, keepdims=True) + RMS_EPS)
        g = gate_ref[rows, :]
        o_ref[rows, :] = (o * ng_ref[...] * (g * _sigmoid(g))).astype(o_ref.dtype)
        return carry

    lax.fori_loop(0, n_tiles, fwd_body, 0)
    lax.fori_loop(0, n_tiles, bwd_body, 0)
    if want_state:
        for z in range(2):
            sfin_ref[z] = st_ref[z].T


def _hgrn_rec(proj, lb, norm_g, s0, prev_out, *, n_seq, seq_len, row_block0, n_heads, total_rows):
    d = n_heads * LANES
    zero_init = s0 is None
    want_state = s0 is None
    aliased = prev_out is not None

    def col(block):
        return lambda b, h: (row_block0 + b, block * n_heads + h)

    blk = (seq_len, LANES)
    in_specs = [pl.BlockSpec(blk, col(0)), pl.BlockSpec(blk, col(1)), pl.BlockSpec(blk, col(2)),
                pl.BlockSpec(blk, col(3)), pl.BlockSpec(blk, col(4)), pl.BlockSpec(blk, col(5)),
                pl.BlockSpec(blk, col(6)),
                pl.BlockSpec((2, LANES), lambda b, h: (0, h)),
                pl.BlockSpec((1, LANES), lambda b, h: (0, h))]
    args = [proj] * 7 + [lb, norm_g.reshape(1, d)]
    if not zero_init:
        in_specs.append(pl.BlockSpec((None, 2, None, LANES, LANES), lambda b, h: (b, 0, h, 0, 0)))
        args.append(s0)
    io_alias = {}
    if aliased:
        in_specs.append(pl.BlockSpec(memory_space=pl.ANY))
        io_alias = {len(args): 0}
        args.append(prev_out)
    out_specs = [pl.BlockSpec(blk, lambda b, h: (row_block0 + b, h))]
    out_shape = [jax.ShapeDtypeStruct((total_rows, d), BF16)]
    if want_state:
        out_specs.append(pl.BlockSpec((None, 2, None, LANES, LANES), lambda b, h: (b, 0, h, 0, 0)))
        out_shape.append(jax.ShapeDtypeStruct((n_seq, 2, n_heads, LANES, LANES), F32))
    res = pl.pallas_call(
        functools.partial(_hgrn_rec_kernel, seq_len=seq_len, zero_init=zero_init,
                          want_state=want_state, aliased=aliased),
        grid=(n_seq, n_heads),
        in_specs=in_specs,
        out_specs=out_specs,
        out_shape=out_shape,
        scratch_shapes=[pltpu.VMEM((seq_len, LANES), F32), pltpu.VMEM((2, LANES, LANES), F32)],
        input_output_aliases=io_alias,
        compiler_params=pltpu.CompilerParams(
            dimension_semantics=("parallel", "parallel"), vmem_limit_bytes=VMEM_LIMIT),
        name="hgrn_rec",
    )(*args)
    return res


def _int_mod(x, n):
    return x & (n - 1) if n & (n - 1) == 0 else lax.rem(x, n)


def _token_shift(x_ref, xp_ref, xn_ref, mod_ref, tile_start, n_prompt_rows, prompt_len, sample_len):
    m = mod_ref[...]
    sh, sc = m[0:1], 1.0 + m[1:2]
    h = x_ref[...] * sc + sh
    h_before = xp_ref[7:8, :] * sc + sh
    h_after = xn_ref[0:1, :] * sc + sh
    tm = h.shape[0]
    rr = lax.broadcasted_iota(jnp.int32, (tm, 1), 0)
    grow = tile_start + rr
    in_prompt = grow < n_prompt_rows
    pos = jnp.where(in_prompt, _int_mod(grow, prompt_len), _int_mod(grow - n_prompt_rows, sample_len))
    last = jnp.where(in_prompt, prompt_len - 1, sample_len - 1)
    prev = jnp.where(rr == 0, h_before, pltpu.roll(h, 1, 0))
    prev = jnp.where(pos == 0, 0.0, prev)
    nxt = jnp.where(rr == tm - 1, h_after, pltpu.roll(h, tm - 1, 0))
    nxt = jnp.where(pos == last, 0.0, nxt)
    return h, 0.5 * (prev + nxt) - h


def _rwkv_rkv_kernel(x_ref, xp_ref, xn_ref, mod_ref, mu_ref, w_ref, o_ref, h_ref, xx_ref, *, tm, seq_info):
    @pl.when(pl.program_id(1) == 0)
    def _():
        h, xx = _token_shift(x_ref, xp_ref, xn_ref, mod_ref, pl.program_id(0) * tm, *seq_info)
        h_ref[...] = h
        xx_ref[...] = xx

    xs = h_ref[...] + xx_ref[...] * mu_ref[...]
    o_ref[...] = _dot(xs.astype(BF16), w_ref[...])


def _halo_specs(tm, d, t, n_grid_axes):
    nb = t // 8

    def before(i, *_):
        return (jnp.maximum(i * (tm // 8) - 1, 0), 0)

    def after(i, *_):
        return (jnp.minimum((i + 1) * (tm // 8), nb - 1), 0)

    del n_grid_axes
    return pl.BlockSpec((8, d), before), pl.BlockSpec((8, d), after)


def _rwkv_rkv(x, mods, mu3, w_rkv, seg, *, n_prompt_rows, prompt_len, sample_len, tm=256):
    t, d = x.shape
    n = w_rkv.shape[2]
    before, after = _halo_specs(tm, d, t, 2)
    return pl.pallas_call(
        functools.partial(_rwkv_rkv_kernel, tm=tm, seq_info=(n_prompt_rows, prompt_len, sample_len)),
        grid=(t // tm, 3),
        in_specs=[
            pl.BlockSpec((tm, d), lambda i, j: (i, 0)),
            before, after,
            pl.BlockSpec((None, 6, d), lambda i, j: (seg(i, tm), 0, 0)),
            pl.BlockSpec((None, 1, d), lambda i, j: (j, 0, 0)),
            pl.BlockSpec((None, d, n), lambda i, j: (j, 0, 0)),
        ],
        out_specs=pl.BlockSpec((None, tm, n), lambda i, j: (j, i, 0)),
        out_shape=jax.ShapeDtypeStruct((3, t, n), F32),
        scratch_shapes=[pltpu.VMEM((tm, d), F32), pltpu.VMEM((tm, d), F32)],
        compiler_params=pltpu.CompilerParams(
            dimension_semantics=("parallel", "arbitrary"), vmem_limit_bytes=VMEM_LIMIT),
        name="rwkv_rkv",
    )(x, x, x, mods, mu3, w_rkv)


def _rwkv_lora_kernel(x_ref, xp_ref, xn_ref, mod_ref, mu_ref, wla_ref, wlb_ref, w0_ref, ala_ref, alb_ref,
                      a0_ref, gla_ref, glb_ref, lw_ref, a_ref, g_ref, *, tm, seq_info):
    h, xx = _token_shift(x_ref, xp_ref, xn_ref, mod_ref, pl.program_id(0) * tm, *seq_info)
    mu = mu_ref[...]
    xs_w = (h + xx * mu[1:2]).astype(BF16)
    xs_a = (h + xx * mu[4:5]).astype(BF16)
    xs_g = (h + xx * mu[5:6]).astype(BF16)
    zw = w0_ref[...] + _bdot(jnp.tanh(_dot(xs_w, wla_ref[...])), wlb_ref[...])
    lw_ref[...] = -DECAY_SCALE * _sigmoid(zw)
    za = a0_ref[...] + _bdot(_dot(xs_a, ala_ref[...]), alb_ref[...])
    a_ref[...] = _sigmoid(za)
    g_ref[...] = _bdot(_sigmoid(_dot(xs_g, gla_ref[...])), glb_ref[...])


def _rwkv_lora(x, mods, mu, weights, seg, *, n_prompt_rows, prompt_len, sample_len, tm=256):
    t, d = x.shape
    before, after = _halo_specs(tm, d, t, 1)

    def whole(arr):
        return pl.BlockSpec(arr.shape, lambda i: (0,) * arr.ndim)

    return pl.pallas_call(
        functools.partial(_rwkv_lora_kernel, tm=tm, seq_info=(n_prompt_rows, prompt_len, sample_len)),
        grid=(t // tm,),
        in_specs=[pl.BlockSpec((tm, d), lambda i: (i, 0)), before, after,
                  pl.BlockSpec((None, 6, d), lambda i: (seg(i, tm), 0, 0)), whole(mu)]
                 + [whole(w) for w in weights],
        out_specs=[pl.BlockSpec((tm, 2 * d), lambda i: (i, 0)),
                   pl.BlockSpec((tm, 2 * d), lambda i: (i, 0)),
                   pl.BlockSpec((tm, d), lambda i: (i, 0))],
        out_shape=[jax.ShapeDtypeStruct((t, 2 * d), F32), jax.ShapeDtypeStruct((t, 2 * d), F32),
                   jax.ShapeDtypeStruct((t, d), F32)],
        compiler_params=pltpu.CompilerParams(
            dimension_semantics=("parallel",), vmem_limit_bytes=VMEM_LIMIT),
        name="rwkv_lora",
    )(x, x, x, mods, mu, *weights)


def _unit_triangular_inverse(a, eye):
    m = eye - a
    p = a
    for _ in range(6):
        p = _hdot(p, p)
        m = m + _hdot(m, p)
    return m


def _rwkv_tile(data_refs, params, rows, consts, fwd):
    r_ref, k_ref, v_ref, lw_ref, a_ref = data_refs
    kkp, kap, rkp = params
    bd, tri, strict, incl, eye, head_masks = consts
    r = r_ref[rows, :]
    k = k_ref[rows, :]
    v = v_ref[rows, :]
    lw = lw_ref[rows, :]
    a = a_ref[rows, :]

    kk = k * kkp
    kk = kk / jnp.maximum(jnp.sqrt(_hdot(kk * kk, bd)), 1e-12)
    k2 = k * (1.0 + (a - 1.0) * kap)
    bonus = _hdot(r * k2 * rkp, bd) * v
    b = kk * a

    cl = _hdot(tri, lw)
    cle = cl - lw
    cm = cl[LANES // 2:LANES // 2 + 1]
    ct = cl[LANES - 1:LANES] if fwd else cl[0:1]
    kkt = kk * jnp.exp(cle - cm)
    rt = r * jnp.exp(cl - cm)
    e_inv = jnp.exp(cm - cl)
    kh = k2 * e_inv
    bh = b * e_inv
    kkd = kk * jnp.exp(cle)
    rd = r * jnp.exp(cl)
    e_out = jnp.exp(ct - cl)
    kg = k2 * e_out
    bg = b * e_out
    gc = jnp.exp(ct)

    lhs = jnp.concatenate([kkt * head_masks[0], kkt * head_masks[1],
                           rt * head_masks[0], rt * head_masks[1]], axis=0)
    gk = _hdot(lhs, kh, NT_DIMS)
    gb = _hdot(lhs, bh, NT_DIMS)

    wt = u0 = y0 = None
    rp = rd
    for hd in range(2):
        lo = hd * LANES
        a_k = jnp.where(strict, gk[lo:lo + LANES], 0.0)
        a_b = jnp.where(strict, gb[lo:lo + LANES], 0.0)
        b_k = jnp.where(incl, gk[2 * LANES + lo:3 * LANES + lo], 0.0)
        b_b = jnp.where(incl, gb[2 * LANES + lo:3 * LANES + lo], 0.0)
        minv = _unit_triangular_inverse(a_b, eye)
        v_h = v * head_masks[hd]
        wt_h = _hdot(minv, kkd * head_masks[hd])
        u0_h = _hdot(minv, _hdot(a_k, v_h))
        y0_h = _hdot(b_k, v_h) - _hdot(b_b, u0_h)
        rp = rp - _hdot(b_b, wt_h)
        wt = wt_h if wt is None else wt + wt_h
        u0 = u0_h if u0 is None else u0 + u0_h
        y0 = y0_h if y0 is None else y0 + y0_h

    q0 = bd * (_hdot(v, kg, TN_DIMS) - _hdot(u0.T, bg))
    p_mat = _hdot(wt, bg, TN_DIMS)
    return rp, y0, gc, q0, p_mat, bonus


def _rwkv_rec_kernel(*refs, seq_len, zero_init, want_state, aliased):
    data = (refs[0:5], refs[5:10])
    g_ref, kk_ref, ka_ref, rk_ref, gg_ref, gb_ref = refs[10:16]
    pos = 16
    s0_ref = None
    if not zero_init:
        s0_ref = refs[pos]
        pos += 1
    if aliased:
        pos += 1
    o_ref = refs[pos]
    pos += 1
    sfin_ref = None
    if want_state:
        sfin_ref = refs[pos]
        pos += 1
    osum_ref, st_ref = refs[pos:pos + 2]

    n_tiles = seq_len // LANES
    r_i = lax.broadcasted_iota(jnp.int32, (LANES, LANES), 0)
    c_i = lax.broadcasted_iota(jnp.int32, (LANES, LANES), 1)
    bd_bool = (r_i >> 6) == (c_i >> 6)
    bd = bd_bool.astype(F32)
    eye = (r_i == c_i).astype(F32)
    lane = lax.broadcasted_iota(jnp.int32, (1, LANES), 1)
    head_masks = ((lane < RWKV_HEAD).astype(F32), (lane >= RWKV_HEAD).astype(F32))
    inv_n = 1.0 / RWKV_HEAD

    for z in range(2):
        st_ref[z] = jnp.zeros((LANES, LANES), F32) if zero_init else s0_ref[z]

    def direction(z):
        fwd = z == 0
        tri = ((c_i <= r_i) if fwd else (c_i >= r_i)).astype(F32)
        strict = (c_i < r_i) if fwd else (c_i > r_i)
        incl = (c_i <= r_i) if fwd else (c_i >= r_i)
        consts = (bd, tri, strict, incl, eye, head_masks)
        params = (kk_ref[z:z + 1, :], ka_ref[z:z + 1, :], rk_ref[z:z + 1, :])
        gn_g = gg_ref[z:z + 1, :]
        gn_b = gb_ref[z:z + 1, :]

        def body(i, carry):
            ti = i if fwd else n_tiles - 1 - i
            rows = pl.ds(pl.multiple_of(ti * LANES, LANES), LANES)
            rp, y0, gc, q0, p_mat, bonus = _rwkv_tile(data[z], params, rows, consts, fwd)
            st = st_ref[z]
            y = _hdot(rp, st, NT_DIMS) + y0
            st_ref[z] = st * gc + q0 - bd * _hdot(st, p_mat)
            mean = _hdot(y, bd) * inv_n
            yc = y - mean
            var = _hdot(yc * yc, bd) * inv_n
            out = yc * lax.rsqrt(var + GN_EPS) * gn_g + gn_b + bonus
            if fwd:
                osum_ref[rows, :] = out
            else:
                o_ref[rows, :] = ((osum_ref[rows, :] + out) * g_ref[rows, :]).astype(o_ref.dtype)
            return carry

        lax.fori_loop(0, n_tiles, body, 0)

    direction(0)
    direction(1)
    if want_state:
        for z in range(2):
            sfin_ref[z] = st_ref[z]


def _rwkv_rec(rkv, lw, a, g, params, s0, prev_out, *, n_seq, seq_len, row_block0, total_rows):
    d = g.shape[1]
    n_pairs = d // LANES
    zero_init = s0 is None
    want_state = s0 is None
    aliased = prev_out is not None
    blk = (seq_len, LANES)

    def dir_specs(z):
        col = lambda b, p: (row_block0 + b, z * n_pairs + p)
        return ([pl.BlockSpec((None,) + blk, lambda b, p, j=j: (j, row_block0 + b, z * n_pairs + p))
                 for j in range(3)] + [pl.BlockSpec(blk, col), pl.BlockSpec(blk, col)])

    in_specs = dir_specs(0) + dir_specs(1)
    in_specs.append(pl.BlockSpec(blk, lambda b, p: (row_block0 + b, p)))
    in_specs += [pl.BlockSpec((2, LANES), lambda b, p: (0, p)) for _ in range(5)]
    args = [rkv, rkv, rkv, lw, a] * 2 + [g] + list(params)
    state_spec = pl.BlockSpec((None, 2, None, LANES, LANES), lambda b, p: (b, 0, p, 0, 0))
    if not zero_init:
        in_specs.append(state_spec)
        args.append(s0)
    io_alias = {}
    if aliased:
        in_specs.append(pl.BlockSpec(memory_space=pl.ANY))
        io_alias = {len(args): 0}
        args.append(prev_out)
    out_specs = [pl.BlockSpec(blk, lambda b, p: (row_block0 + b, p))]
    out_shape = [jax.ShapeDtypeStruct((total_rows, d), BF16)]
    if want_state:
        out_specs.append(state_spec)
        out_shape.append(jax.ShapeDtypeStruct((n_seq, 2, n_pairs, LANES, LANES), F32))
    return pl.pallas_call(
        functools.partial(_rwkv_rec_kernel, seq_len=seq_len, zero_init=zero_init,
                          want_state=want_state, aliased=aliased),
        grid=(n_seq, n_pairs),
        in_specs=in_specs,
        out_specs=out_specs,
        out_shape=out_shape,
        scratch_shapes=[pltpu.VMEM((seq_len, LANES), F32), pltpu.VMEM((2, LANES, LANES), F32)],
        input_output_aliases=io_alias,
        compiler_params=pltpu.CompilerParams(
            dimension_semantics=("parallel", "parallel"), vmem_limit_bytes=VMEM_LIMIT),
        name="rwkv_rec",
    )(*args)


def _rwkv_prepare_weights(mu, w_rkv, w0, w_la, w_lb, a0, a_la, a_lb, g_la, g_lb):
    d = mu.shape[1]
    rank_w = w_la.shape[2]
    rank_a = a_la.shape[2]
    rank_g = g_la.shape[1]
    rank_g_pad = -(-rank_g // LANES) * LANES

    def block_diag(w):
        rank = w.shape[1]
        out = jnp.zeros((2, rank, 2, d), w.dtype)
        out = out.at[0, :, 0, :].set(w[0]).at[1, :, 1, :].set(w[1])
        return out.reshape(2 * rank, 2 * d)

    lora = (
        w_la.reshape(d, 2 * rank_w).astype(BF16), block_diag(w_lb).astype(BF16), w0.reshape(1, 2 * d),
        a_la.reshape(d, 2 * rank_a).astype(BF16), block_diag(a_lb).astype(BF16), a0.reshape(1, 2 * d),
        jnp.pad(g_la, ((0, 0), (0, rank_g_pad - rank_g))).astype(BF16),
        jnp.pad(g_lb, ((0, rank_g_pad - rank_g), (0, 0))).astype(BF16),
    )
    mu3 = jnp.stack([mu[0], mu[2], mu[3]]).reshape(3, 1, d)
    return {"mu3": mu3, "w_rkv": w_rkv.astype(BF16), "lora": lora}


def _pair_states(s):
    n, _, h, hn, _ = s.shape
    s = s.reshape(n, 2, h // 2, 2, hn, hn)
    out = jnp.zeros((n, 2, h // 2, 2, hn, 2, hn), s.dtype)
    out = out.at[:, :, :, 0, :, 0, :].set(s[:, :, :, 0]).at[:, :, :, 1, :, 1, :].set(s[:, :, :, 1])
    return out.reshape(n, 2, h // 2, 2 * hn, 2 * hn)


def _unpair_states(sp):
    n, _, hp, hn2, _ = sp.shape
    hn = hn2 // 2
    sp = sp.reshape(n, 2, hp, 2, hn, 2, hn)
    return jnp.stack([sp[:, :, :, 0, :, 0, :], sp[:, :, :, 1, :, 1, :]], axis=3).reshape(n, 2, 2 * hp, hn, hn)


def _embed_kernel(xp_ref, xs_ref, pos_ref, o_ref, *, n_prompt_tiles):
    i = pl.program_id(0)

    @pl.when(i < n_prompt_tiles)
    def _():
        o_ref[...] = xp_ref[...]

    @pl.when(i >= n_prompt_tiles)
    def _():
        o_ref[...] = xs_ref[...] + pos_ref[...]


def _grid_pos_embed(n_tokens, d):
    rows = n_tokens // GRID_W
    quarter = d // 4
    omega = 1.0 / (POS_BASE ** (jnp.arange(quarter, dtype=F32) / quarter))
    r = jnp.arange(rows, dtype=F32)[:, None] * omega
    cc = jnp.arange(GRID_W, dtype=F32)[:, None] * omega
    row_emb = jnp.concatenate([jnp.sin(r), jnp.cos(r)], -1)
    col_emb = jnp.concatenate([jnp.sin(cc), jnp.cos(cc)], -1)
    emb = jnp.concatenate([
        jnp.broadcast_to(row_emb[:, None, :], (rows, GRID_W, d // 2)),
        jnp.broadcast_to(col_emb[None, :, :], (rows, GRID_W, d // 2))], -1)
    return emb.reshape(rows * GRID_W, d)


def _embed(xp, xs, pos, sample_len, tm=512):
    n_p, d = xp.shape
    n_s = xs.shape[0]
    npt = n_p // tm
    pos_tiles = sample_len // tm
    return pl.pallas_call(
        functools.partial(_embed_kernel, n_prompt_tiles=npt),
        grid=((n_p + n_s) // tm,),
        in_specs=[
            pl.BlockSpec((tm, d), lambda i: (jnp.minimum(i, npt - 1), 0)),
            pl.BlockSpec((tm, d), lambda i: (jnp.maximum(i - npt, 0), 0)),
            pl.BlockSpec((tm, d), lambda i: (lax.rem(jnp.maximum(i - npt, 0), pos_tiles), 0)),
        ],
        out_specs=pl.BlockSpec((tm, d), lambda i: (i, 0)),
        out_shape=jax.ShapeDtypeStruct((n_p + n_s, d), F32),
        compiler_params=pltpu.CompilerParams(
            dimension_semantics=("parallel",), vmem_limit_bytes=VMEM_LIMIT),
        name="embed",
    )(xp, xs, pos)


def kernel(x_prompt, x_sample, state_hgrn, state_rwkv, c, c_ctx, ada_w, ada_b, ln_g, ln_b, ffn_w_up, ffn_w_down, hgrn_w_in, hgrn_lb, hgrn_norm_g, hgrn_w_o, rwkv_mu, rwkv_w_rkv, rwkv_w0, rwkv_w_la, rwkv_w_lb, rwkv_a0, rwkv_a_la, rwkv_a_lb, rwkv_g_la, rwkv_g_lb, rwkv_k_k, rwkv_k_a, rwkv_r_k, rwkv_gn_g, rwkv_gn_b, rwkv_w_o):
    n_b, l_p, d = x_prompt.shape
    n_s, l_s, _ = x_sample.shape
    depth = ada_w.shape[0]
    n_p_rows = n_b * l_p
    total = n_p_rows + n_s * l_s
    assert n_p_rows % l_s == 0 and l_p % LANES == 0 and l_s % LANES == 0
    alpha = (2 * depth) ** 0.25
    a_heads = d // LANES

    def seg(i, tm):
        return _seg_index(i, tm, n_p_rows, l_s)

    x = _embed(x_prompt.reshape(n_p_rows, d), x_sample.reshape(n_s * l_s, d), _grid_pos_embed(l_s, d), l_s)

    cond8 = jnp.zeros((8, d), F32).at[0].set(c_ctx).at[1:1 + n_s].set(c)
    mods = _adaln(cond8, ada_w, ada_b).reshape(depth, 8, 6, d)

    lb_soft = jax.nn.softmax(hgrn_lb.astype(F32), axis=0)
    lower_bounds = jnp.cumsum(lb_soft, axis=0) - lb_soft[0]

    new_hgrn = []
    new_rwkv = []
    for l in range(depth):
        j = l // 2
        if l % 2 == 0:
            proj = _modmm(x, mods[l], hgrn_w_in[j].astype(BF16), seg)
            o, s_ctx = _hgrn_rec(proj, lower_bounds[j], hgrn_norm_g[j], None, None, n_seq=n_b, seq_len=l_p,
                                 row_block0=0, n_heads=a_heads, total_rows=total)
            (o,) = _hgrn_rec(proj, lower_bounds[j], hgrn_norm_g[j], state_hgrn[:, j], o, n_seq=n_s, seq_len=l_s,
                             row_block0=n_p_rows // l_s, n_heads=a_heads, total_rows=total)
            new_hgrn.append(s_ctx)
            w_o = hgrn_w_o[j]
        else:
            prep = _rwkv_prepare_weights(rwkv_mu[j], rwkv_w_rkv[j], rwkv_w0[j], rwkv_w_la[j], rwkv_w_lb[j],
                                         rwkv_a0[j], rwkv_a_la[j], rwkv_a_lb[j], rwkv_g_la[j], rwkv_g_lb[j])
            seq_kw = dict(n_prompt_rows=n_p_rows, prompt_len=l_p, sample_len=l_s)
            rkv = _rwkv_rkv(x, mods[l], prep["mu3"], prep["w_rkv"], seg, **seq_kw)
            lw, a, g = _rwkv_lora(x, mods[l], rwkv_mu[j], prep["lora"], seg, **seq_kw)
            params = (rwkv_k_k[j], rwkv_k_a[j], rwkv_r_k[j], rwkv_gn_g[j], rwkv_gn_b[j])
            o, s_ctx = _rwkv_rec(rkv, lw, a, g, params, None, None, n_seq=n_b, seq_len=l_p,
                                 row_block0=0, total_rows=total)
            (o,) = _rwkv_rec(rkv, lw, a, g, params, _pair_states(state_rwkv[:, j]), o, n_seq=n_s, seq_len=l_s,
                             row_block0=n_p_rows // l_s, total_rows=total)
            new_rwkv.append(_unpair_states(s_ctx))
            w_o = rwkv_w_o[j]
        x = _out_ln(o, x, mods[l], w_o.astype(BF16), ln_g[l, 0:1], ln_b[l, 0:1], seg, alpha)
        x = _mlp_ln(x, mods[l], ffn_w_up[l].astype(BF16), ffn_w_down[l].astype(BF16),
                    ln_g[l, 1:2], ln_b[l, 1:2], seg, alpha)

    y_prompt = x[:n_p_rows].reshape(n_b, l_p, d)
    y_sample = x[n_p_rows:].reshape(n_s, l_s, d)
    return (y_prompt, y_sample, jnp.stack(new_hgrn, axis=1), jnp.stack(new_rwkv, axis=1))
```

```python
import functools

import jax
import jax.numpy as jnp
from jax import lax
from jax.experimental import pallas as pl
from jax.experimental.pallas import tpu as pltpu

F32 = jnp.float32
BF16 = jnp.bfloat16
HIGHEST = lax.Precision.HIGHEST

LN_EPS = 1e-5
RMS_EPS = 1e-6
GN_EPS = 64e-5
DECAY_SCALE = 0.606531
EXP_CLIP = 80.0
POS_BASE = 10000.0
GRID_W = 64

LANES = 128
HGRN_CHUNK = 16
RWKV_HEAD = 64
VMEM_LIMIT = 56 * 1024 * 1024

NT_DIMS = (((1,), (1,)), ((), ()))
TN_DIMS = (((0,), (0,)), ((), ()))


def _dot(a, b, dims=None, precision=None):
    if dims is None:
        return jnp.dot(a, b, preferred_element_type=F32, precision=precision)
    return lax.dot_general(a, b, dims, preferred_element_type=F32, precision=precision)


def _bdot(a, b, dims=None):
    return _dot(a.astype(BF16), b.astype(BF16), dims)


def _hdot(a, b, dims=None):
    return _dot(a, b, dims, precision=HIGHEST)


def _split_bf16(x):
    hi = x.astype(BF16)
    return hi, (x - hi.astype(F32)).astype(BF16)


def _dot_x3(a, b, dims=None):
    ah, al = _split_bf16(a)
    bh, bl = _split_bf16(b)
    return _dot(ah, bh, dims) + (_dot(ah, bl, dims) + _dot(al, bh, dims))


def _dot_x2l(a, b, dims=None):
    ah, al = _split_bf16(a)
    b = b.astype(BF16)
    return _dot(ah, b, dims) + _dot(al, b, dims)


def _dot_x2r(a, b, dims=None):
    bh, bl = _split_bf16(b)
    a = a.astype(BF16)
    return _dot(a, bh, dims) + _dot(a, bl, dims)


def _sigmoid(x):
    return jax.nn.sigmoid(x)


def _layer_norm(x, g, b):
    mu = jnp.mean(x, -1, keepdims=True)
    xc = x - mu
    var = jnp.mean(xc * xc, -1, keepdims=True)
    return xc * lax.rsqrt(var + LN_EPS) * g + b


def _seg_index(i, tm, n_prompt_rows, sample_len):
    start = i * tm
    return jnp.where(start < n_prompt_rows, 0, 1 + (start - n_prompt_rows) // sample_len)


def _adaln_kernel(c_ref, w_ref, b_ref, o_ref):
    c = c_ref[...]
    s = c * _sigmoid(c)
    o_ref[...] = _bdot(s, w_ref[...]) + b_ref[...]


def _adaln(cond8, ada_w, ada_b, tn=1536):
    depth, d, n = ada_w.shape
    return pl.pallas_call(
        _adaln_kernel,
        grid=(depth, n // tn),
        in_specs=[
            pl.BlockSpec((8, d), lambda l, j: (0, 0)),
            pl.BlockSpec((None, d, tn), lambda l, j: (l, 0, j)),
            pl.BlockSpec((None, 1, tn), lambda l, j: (l, 0, j)),
        ],
        out_specs=pl.BlockSpec((None, 8, tn), lambda l, j: (l, 0, j)),
        out_shape=jax.ShapeDtypeStruct((depth, 8, n), F32),
        compiler_params=pltpu.CompilerParams(
            dimension_semantics=("parallel", "parallel"), vmem_limit_bytes=VMEM_LIMIT),
        name="adaln",
    )(cond8, ada_w, ada_b.reshape(depth, 1, n))


def _modmm_kernel(x_ref, mod_ref, w_ref, o_ref, xb_ref):
    @pl.when(pl.program_id(1) == 0)
    def _():
        m = mod_ref[...]
        xb_ref[...] = (x_ref[...] * (1.0 + m[1:2]) + m[0:1]).astype(BF16)

    o_ref[...] = _dot(xb_ref[...], w_ref[...])


def _modmm(x, mods, w, seg, tm=512, tn=1024):
    t, d = x.shape
    n = w.shape[1]
    return pl.pallas_call(
        _modmm_kernel,
        grid=(t // tm, n // tn),
        in_specs=[
            pl.BlockSpec((tm, d), lambda i, j: (i, 0)),
            pl.BlockSpec((None, 6, d), lambda i, j: (seg(i, tm), 0, 0)),
            pl.BlockSpec((d, tn), lambda i, j: (0, j)),
        ],
        out_specs=pl.BlockSpec((tm, tn), lambda i, j: (i, j)),
        out_shape=jax.ShapeDtypeStruct((t, n), F32),
        scratch_shapes=[pltpu.VMEM((tm, d), BF16)],
        compiler_params=pltpu.CompilerParams(
            dimension_semantics=("parallel", "arbitrary"), vmem_limit_bytes=VMEM_LIMIT),
        name="modmm",
    )(x, mods, w)


def _out_ln_kernel(o_ref, x_ref, mod_ref, w_ref, g_ref, b_ref, y_ref, *, alpha):
    y = _dot(o_ref[...], w_ref[...])
    gate = mod_ref[...][2:3]
    y_ref[...] = _layer_norm(alpha * x_ref[...] + gate * y, g_ref[...], b_ref[...])


def _out_ln(o, x, mods, w, g, b, seg, alpha, tm=512):
    t, d = x.shape
    return pl.pallas_call(
        functools.partial(_out_ln_kernel, alpha=alpha),
        grid=(t // tm,),
        in_specs=[
            pl.BlockSpec((tm, d), lambda i: (i, 0)),
            pl.BlockSpec((tm, d), lambda i: (i, 0)),
            pl.BlockSpec((None, 6, d), lambda i: (seg(i, tm), 0, 0)),
            pl.BlockSpec((d, d), lambda i: (0, 0)),
            pl.BlockSpec((1, d), lambda i: (0, 0)),
            pl.BlockSpec((1, d), lambda i: (0, 0)),
        ],
        out_specs=pl.BlockSpec((tm, d), lambda i: (i, 0)),
        out_shape=jax.ShapeDtypeStruct((t, d), F32),
        compiler_params=pltpu.CompilerParams(
            dimension_semantics=("parallel",), vmem_limit_bytes=VMEM_LIMIT),
        name="out_ln",
    )(o, x, mods, w, g, b)


def _mlp_ln_kernel(x_ref, mod_ref, wu_ref, wd_ref, g_ref, b_ref, y_ref, xb_ref, acc_ref, *, alpha):
    f = pl.program_id(1)

    @pl.when(f == 0)
    def _():
        m = mod_ref[...]
        xb_ref[...] = (x_ref[...] * (1.0 + m[4:5]) + m[3:4]).astype(BF16)
        acc_ref[...] = jnp.zeros_like(acc_ref)

    h = jnp.maximum(_dot(xb_ref[...], wu_ref[...]), 0.0)
    acc_ref[...] += _dot((h * h).astype(BF16), wd_ref[...])

    @pl.when(f == pl.num_programs(1) - 1)
    def _():
        gate = mod_ref[...][5:6]
        y_ref[...] = _layer_norm(alpha * x_ref[...] + gate * acc_ref[...], g_ref[...], b_ref[...])


def _mlp_ln(x, mods, w_up, w_down, g, b, seg, alpha, tm=512, tf=1024):
    t, d = x.shape
    dff = w_up.shape[1]
    return pl.pallas_call(
        functools.partial(_mlp_ln_kernel, alpha=alpha),
        grid=(t // tm, dff // tf),
        in_specs=[
            pl.BlockSpec((tm, d), lambda i, f: (i, 0)),
            pl.BlockSpec((None, 6, d), lambda i, f: (seg(i, tm), 0, 0)),
            pl.BlockSpec((d, tf), lambda i, f: (0, f)),
            pl.BlockSpec((tf, d), lambda i, f: (f, 0)),
            pl.BlockSpec((1, d), lambda i, f: (0, 0)),
            pl.BlockSpec((1, d), lambda i, f: (0, 0)),
        ],
        out_specs=pl.BlockSpec((tm, d), lambda i, f: (i, 0)),
        out_shape=jax.ShapeDtypeStruct((t, d), F32),
        scratch_shapes=[pltpu.VMEM((tm, d), BF16), pltpu.VMEM((tm, d), F32)],
        compiler_params=pltpu.CompilerParams(
            dimension_semantics=("parallel", "arbitrary"), vmem_limit_bytes=VMEM_LIMIT),
        name="mlp_ln",
    )(x, mods, w_up, w_down, g, b)


def _hgrn_tile(q_ref, z_ref, v_ref, lbz, rows, consts, fwd):
    r_i, c_i, tri, blk = consts
    pos = r_i & (HGRN_CHUNK - 1)
    qr = q_ref[rows, :]
    zz = z_ref[rows, :]
    v = v_ref[rows, :]
    q = qr * _sigmoid(qr)
    log_f = (jnp.minimum(zz, 0.0) - jnp.log1p(jnp.exp(-jnp.abs(zz)))
             + jnp.log1p(lbz * jnp.exp(jnp.minimum(-zz, EXP_CLIP))))
    k = (1.0 - lbz) * _sigmoid(-zz)
    cum = _hdot(tri, log_f)
    ctot = _hdot(blk, log_f)
    q_in = q * jnp.exp(cum)
    k_out = k * jnp.exp(ctot - cum)
    dec = jnp.exp(ctot)

    scores = jnp.zeros((LANES, LANES), F32)
    for d in range(HGRN_CHUNK):
        if d == 0:
            kr, cr = k, cum
        else:
            sh = d if fwd else LANES - d
            kr = pltpu.roll(k, sh, 0)
            cr = pltpu.roll(cum, sh, 0)
        valid = (pos >= d) if fwd else (pos <= HGRN_CHUNK - 1 - d)
        e = jnp.exp(jnp.where(valid, cum - cr, 0.0))
        s = jnp.sum(q * kr * e, axis=-1, keepdims=True)
        tgt = (c_i == r_i - d) if fwd else (c_i == r_i + d)
        scores = scores + jnp.where(tgt & valid, s, 0.0)
    o_intra = _bdot(scores, v)

    v_t = v.T
    chunk_of_lane = c_i >> 4
    lhs = jnp.concatenate(
        [jnp.where(chunk_of_lane == c, v_t, 0.0) for c in range(LANES // HGRN_CHUNK)], axis=0)
    upd = _bdot(lhs, k_out)
    return q_in, o_intra, upd, dec


def _hgrn_rec_kernel(*refs, seq_len, zero_init, want_state, aliased):
    qf_ref, zf_ref, vf_ref, qb_ref, zb_ref, vb_ref, gate_ref, lb_ref, ng_ref = refs[:9]
    pos = 9
    s0_ref = None
    if not zero_init:
        s0_ref = refs[pos]
        pos += 1
    if aliased:
        pos += 1
    o_ref = refs[pos]
    pos += 1
    sfin_ref = None
    if want_state:
        sfin_ref = refs[pos]
        pos += 1
    osum_ref, st_ref = refs[pos:pos + 2]

    n_tiles = seq_len // LANES
    n_chunks = LANES // HGRN_CHUNK
    r_i = lax.broadcasted_iota(jnp.int32, (LANES, LANES), 0)
    c_i = lax.broadcasted_iota(jnp.int32, (LANES, LANES), 1)
    same = (r_i >> 4) == (c_i >> 4)
    blk = same.astype(F32)
    tri = ((same & (c_i <= r_i)).astype(F32), (same & (c_i >= r_i)).astype(F32))
    lb = lb_ref[...]

    for z in range(2):
        if zero_init:
            st_ref[z] = jnp.zeros((LANES, LANES), F32)
        else:
            st_ref[z] = s0_ref[z].T

    def fwd_body(i, carry):
        rows = pl.ds(pl.multiple_of(i * LANES, LANES), LANES)
        q_in, o_intra, upd, dec = _hgrn_tile(
            qf_ref, zf_ref, vf_ref, lb[0:1], rows, (r_i, c_i, tri[0], blk), True)
        st = st_ref[0]
        outs = [None] * n_chunks
        for c in range(n_chunks):
            lo = c * HGRN_CHUNK
            outs[c] = _bdot(q_in[lo:lo + HGRN_CHUNK], st, NT_DIMS)
            st = st * dec[lo:lo + 1] + upd[c * LANES:(c + 1) * LANES]
        st_ref[0] = st
        osum_ref[rows, :] = o_intra + jnp.concatenate(outs, axis=0)
        return carry

    def bwd_body(i, carry):
        ti = n_tiles - 1 - i
        rows = pl.ds(pl.multiple_of(ti * LANES, LANES), LANES)
        q_in, o_intra, upd, dec = _hgrn_tile(
            qb_ref, zb_ref, vb_ref, lb[1:2], rows, (r_i, c_i, tri[1], blk), False)
        st = st_ref[1]
        outs = [None] * n_chunks
        for c in range(n_chunks - 1, -1, -1):
            lo = c * HGRN_CHUNK
            outs[c] = _bdot(q_in[lo:lo + HGRN_CHUNK], st, NT_DIMS)
            st = st * dec[lo:lo + 1] + upd[c * LANES:(c + 1) * LANES]
        st_ref[1] = st
        o = osum_ref[rows, :] + o_intra + jnp.concatenate(outs, axis=0)
        o = o * lax.rsqrt(jnp.mean(o * o, -1, keepdims=True) + RMS_EPS)
        g = gate_ref[rows, :]
        o_ref[rows, :] = (o * ng_ref[...] * (g * _sigmoid(g))).astype(o_ref.dtype)
        return carry

    lax.fori_loop(0, n_tiles, fwd_body, 0)
    lax.fori_loop(0, n_tiles, bwd_body, 0)
    if want_state:
        for z in range(2):
            sfin_ref[z] = st_ref[z].T


def _hgrn_rec(proj, lb, norm_g, s0, prev_out, *, n_seq, seq_len, row_block0, n_heads, total_rows):
    d = n_heads * LANES
    zero_init = s0 is None
    want_state = s0 is None
    aliased = prev_out is not None

    def col(block):
        return lambda b, h: (row_block0 + b, block * n_heads + h)

    blk = (seq_len, LANES)
    in_specs = [pl.BlockSpec(blk, col(0)), pl.BlockSpec(blk, col(1)), pl.BlockSpec(blk, col(2)),
                pl.BlockSpec(blk, col(3)), pl.BlockSpec(blk, col(4)), pl.BlockSpec(blk, col(5)),
                pl.BlockSpec(blk, col(6)),
                pl.BlockSpec((2, LANES), lambda b, h: (0, h)),
                pl.BlockSpec((1, LANES), lambda b, h: (0, h))]
    args = [proj] * 7 + [lb, norm_g.reshape(1, d)]
    if not zero_init:
        in_specs.append(pl.BlockSpec((None, 2, None, LANES, LANES), lambda b, h: (b, 0, h, 0, 0)))
        args.append(s0)
    io_alias = {}
    if aliased:
        in_specs.append(pl.BlockSpec(memory_space=pl.ANY))
        io_alias = {len(args): 0}
        args.append(prev_out)
    out_specs = [pl.BlockSpec(blk, lambda b, h: (row_block0 + b, h))]
    out_shape = [jax.ShapeDtypeStruct((total_rows, d), BF16)]
    if want_state:
        out_specs.append(pl.BlockSpec((None, 2, None, LANES, LANES), lambda b, h: (b, 0, h, 0, 0)))
        out_shape.append(jax.ShapeDtypeStruct((n_seq, 2, n_heads, LANES, LANES), F32))
    res = pl.pallas_call(
        functools.partial(_hgrn_rec_kernel, seq_len=seq_len, zero_init=zero_init,
                          want_state=want_state, aliased=aliased),
        grid=(n_seq, n_heads),
        in_specs=in_specs,
        out_specs=out_specs,
        out_shape=out_shape,
        scratch_shapes=[pltpu.VMEM((seq_len, LANES), F32), pltpu.VMEM((2, LANES, LANES), F32)],
        input_output_aliases=io_alias,
        compiler_params=pltpu.CompilerParams(
            dimension_semantics=("parallel", "parallel"), vmem_limit_bytes=VMEM_LIMIT),
        name="hgrn_rec",
    )(*args)
    return res


def _int_mod(x, n):
    return x & (n - 1) if n & (n - 1) == 0 else lax.rem(x, n)


def _token_shift(x_ref, xp_ref, xn_ref, mod_ref, tile_start, n_prompt_rows, prompt_len, sample_len):
    m = mod_ref[...]
    sh, sc = m[0:1], 1.0 + m[1:2]
    h = x_ref[...] * sc + sh
    h_before = xp_ref[7:8, :] * sc + sh
    h_after = xn_ref[0:1, :] * sc + sh
    tm = h.shape[0]
    rr = lax.broadcasted_iota(jnp.int32, (tm, 1), 0)
    grow = tile_start + rr
    in_prompt = grow < n_prompt_rows
    pos = jnp.where(in_prompt, _int_mod(grow, prompt_len), _int_mod(grow - n_prompt_rows, sample_len))
    last = jnp.where(in_prompt, prompt_len - 1, sample_len - 1)
    prev = jnp.where(rr == 0, h_before, pltpu.roll(h, 1, 0))
    prev = jnp.where(pos == 0, 0.0, prev)
    nxt = jnp.where(rr == tm - 1, h_after, pltpu.roll(h, tm - 1, 0))
    nxt = jnp.where(pos == last, 0.0, nxt)
    return h, 0.5 * (prev + nxt) - h


def _rwkv_rkv_kernel(x_ref, xp_ref, xn_ref, mod_ref, mu_ref, w_ref, o_ref, h_ref, xx_ref, *, tm, seq_info):
    @pl.when(pl.program_id(1) == 0)
    def _():
        h, xx = _token_shift(x_ref, xp_ref, xn_ref, mod_ref, pl.program_id(0) * tm, *seq_info)
        h_ref[...] = h
        xx_ref[...] = xx

    xs = h_ref[...] + xx_ref[...] * mu_ref[...]
    o_ref[...] = _dot(xs.astype(BF16), w_ref[...])


def _halo_specs(tm, d, t, n_grid_axes):
    nb = t // 8

    def before(i, *_):
        return (jnp.maximum(i * (tm // 8) - 1, 0), 0)

    def after(i, *_):
        return (jnp.minimum((i + 1) * (tm // 8), nb - 1), 0)

    del n_grid_axes
    return pl.BlockSpec((8, d), before), pl.BlockSpec((8, d), after)


def _rwkv_rkv(x, mods, mu3, w_rkv, seg, *, n_prompt_rows, prompt_len, sample_len, tm=256):
    t, d = x.shape
    n = w_rkv.shape[2]
    before, after = _halo_specs(tm, d, t, 2)
    return pl.pallas_call(
        functools.partial(_rwkv_rkv_kernel, tm=tm, seq_info=(n_prompt_rows, prompt_len, sample_len)),
        grid=(t // tm, 3),
        in_specs=[
            pl.BlockSpec((tm, d), lambda i, j: (i, 0)),
            before, after,
            pl.BlockSpec((None, 6, d), lambda i, j: (seg(i, tm), 0, 0)),
            pl.BlockSpec((None, 1, d), lambda i, j: (j, 0, 0)),
            pl.BlockSpec((None, d, n), lambda i, j: (j, 0, 0)),
        ],
        out_specs=pl.BlockSpec((None, tm, n), lambda i, j: (j, i, 0)),
        out_shape=jax.ShapeDtypeStruct((3, t, n), F32),
        scratch_shapes=[pltpu.VMEM((tm, d), F32), pltpu.VMEM((tm, d), F32)],
        compiler_params=pltpu.CompilerParams(
            dimension_semantics=("parallel", "arbitrary"), vmem_limit_bytes=VMEM_LIMIT),
        name="rwkv_rkv",
    )(x, x, x, mods, mu3, w_rkv)


def _rwkv_lora_kernel(x_ref, xp_ref, xn_ref, mod_ref, mu_ref, wla_ref, wlb_ref, w0_ref, ala_ref, alb_ref,
                      a0_ref, gla_ref, glb_ref, lw_ref, a_ref, g_ref, *, tm, seq_info):
    h, xx = _token_shift(x_ref, xp_ref, xn_ref, mod_ref, pl.program_id(0) * tm, *seq_info)
    mu = mu_ref[...]
    xs_w = (h + xx * mu[1:2]).astype(BF16)
    xs_a = (h + xx * mu[4:5]).astype(BF16)
    xs_g = (h + xx * mu[5:6]).astype(BF16)
    zw = w0_ref[...] + _bdot(jnp.tanh(_dot(xs_w, wla_ref[...])), wlb_ref[...])
    lw_ref[...] = -DECAY_SCALE * _sigmoid(zw)
    za = a0_ref[...] + _bdot(_dot(xs_a, ala_ref[...]), alb_ref[...])
    a_ref[...] = _sigmoid(za)
    g_ref[...] = _bdot(_sigmoid(_dot(xs_g, gla_ref[...])), glb_ref[...])


def _rwkv_lora(x, mods, mu, weights, seg, *, n_prompt_rows, prompt_len, sample_len, tm=256):
    t, d = x.shape
    before, after = _halo_specs(tm, d, t, 1)

    def whole(arr):
        return pl.BlockSpec(arr.shape, lambda i: (0,) * arr.ndim)

    return pl.pallas_call(
        functools.partial(_rwkv_lora_kernel, tm=tm, seq_info=(n_prompt_rows, prompt_len, sample_len)),
        grid=(t // tm,),
        in_specs=[pl.BlockSpec((tm, d), lambda i: (i, 0)), before, after,
                  pl.BlockSpec((None, 6, d), lambda i: (seg(i, tm), 0, 0)), whole(mu)]
                 + [whole(w) for w in weights],
        out_specs=[pl.BlockSpec((tm, 2 * d), lambda i: (i, 0)),
                   pl.BlockSpec((tm, 2 * d), lambda i: (i, 0)),
                   pl.BlockSpec((tm, d), lambda i: (i, 0))],
        out_shape=[jax.ShapeDtypeStruct((t, 2 * d), F32), jax.ShapeDtypeStruct((t, 2 * d), F32),
                   jax.ShapeDtypeStruct((t, d), F32)],
        compiler_params=pltpu.CompilerParams(
            dimension_semantics=("parallel",), vmem_limit_bytes=VMEM_LIMIT),
        name="rwkv_lora",
    )(x, x, x, mods, mu, *weights)


def _rwkv_precompute(entries, consts):
    bd, eye, head_masks, r_i, c_i = consts
    n = len(entries)
    pre = []
    for r, k, v, lw, a, (kkp, kap, rkp), fwd in entries:
        tri = ((c_i <= r_i) if fwd else (c_i >= r_i)).astype(F32)
        kk = k * kkp
        kk = kk / jnp.maximum(jnp.sqrt(_dot_x2l(kk * kk, bd)), 1e-12)
        k2 = k * (1.0 + (a - 1.0) * kap)
        bonus = _dot_x2l(r * k2 * rkp, bd) * v
        b = kk * a
        cl = _dot_x2r(tri, lw)
        cle = cl - lw
        cm = cl[LANES // 2:LANES // 2 + 1]
        ct = cl[LANES - 1:LANES] if fwd else cl[0:1]
        e_inv = jnp.exp(cm - cl)
        e_out = jnp.exp(ct - cl)
        pre.append(dict(
            v=v, bonus=bonus, fwd=fwd,
            kkt=kk * jnp.exp(cle - cm), rt=r * jnp.exp(cl - cm), kh=k2 * e_inv, bh=b * e_inv,
            kkd=kk * jnp.exp(cle), rd=r * jnp.exp(cl), kg=k2 * e_out, bg=b * e_out, gc=jnp.exp(ct)))

    for e in pre:
        lhs = jnp.concatenate([e["kkt"] * head_masks[0], e["kkt"] * head_masks[1],
                               e["rt"] * head_masks[0], e["rt"] * head_masks[1]], axis=0)
        e["gk"] = _dot_x3(lhs, e["kh"], NT_DIMS)
        e["gb"] = _dot_x3(lhs, e["bh"], NT_DIMS)

    chains = []
    for e in pre:
        strict = (c_i < r_i) if e["fwd"] else (c_i > r_i)
        incl = (c_i <= r_i) if e["fwd"] else (c_i >= r_i)
        for hd in range(2):
            lo = hd * LANES
            chains.append(dict(
                e=e, hd=hd,
                a_k=jnp.where(strict, e["gk"][lo:lo + LANES], 0.0),
                a_b=jnp.where(strict, e["gb"][lo:lo + LANES], 0.0),
                b_k=jnp.where(incl, e["gk"][2 * LANES + lo:3 * LANES + lo], 0.0),
                b_b=jnp.where(incl, e["gb"][2 * LANES + lo:3 * LANES + lo], 0.0)))

    for c in chains:
        c["m"] = eye - jnp.where((r_i >> 1) == (c_i >> 1), c["a_b"], 0.0)
    for log_k in range(1, 7):
        off = ((r_i >> (log_k + 1)) == (c_i >> (log_k + 1))) & ((r_i >> log_k) != (c_i >> log_k))
        for c in chains:
            c["t"] = _dot_x3(jnp.where(off, c["a_b"], 0.0), c["m"])
        for c in chains:
            c["m"] = c["m"] - _dot_x3(c["m"], c["t"])

    for c in chains:
        e = c["e"]
        c["v_h"] = e["v"] * head_masks[c["hd"]]
        c["kkd_h"] = e["kkd"] * head_masks[c["hd"]]
        c["akv"] = _dot_x3(c["a_k"], c["v_h"])
    for c in chains:
        c["wt"] = _dot_x3(c["m"], c["kkd_h"])
        c["u0"] = _dot_x3(c["m"], c["akv"])
    for c in chains:
        c["y0"] = _dot_x3(c["b_k"], c["v_h"]) - _dot_x3(c["b_b"], c["u0"])
        c["rp"] = _dot_x3(c["b_b"], c["wt"])

    out = []
    for i, e in enumerate(pre):
        c0, c1 = chains[2 * i], chains[2 * i + 1]
        wt = c0["wt"] + c1["wt"]
        u0 = c0["u0"] + c1["u0"]
        q0 = bd * (_dot_x3(e["v"], e["kg"], TN_DIMS) - _dot_x3(u0.T, e["bg"]))
        p_mat = _dot_x3(wt, e["bg"], TN_DIMS)
        out.append((e["rd"] - c0["rp"] - c1["rp"], c0["y0"] + c1["y0"], e["gc"], q0, p_mat, e["bonus"]))
    assert len(out) == n
    return out


def _rwkv_rec_kernel(*refs, seq_len, zero_init, want_state, aliased):
    data = (refs[0:5], refs[5:10])
    g_ref, kk_ref, ka_ref, rk_ref, gg_ref, gb_ref = refs[10:16]
    pos = 16
    s0_ref = None
    if not zero_init:
        s0_ref = refs[pos]
        pos += 1
    if aliased:
        pos += 1
    o_ref = refs[pos]
    pos += 1
    sfin_ref = None
    if want_state:
        sfin_ref = refs[pos]
        pos += 1
    osum_ref, st_ref = refs[pos:pos + 2]

    n_tiles = seq_len // LANES
    unroll = 2
    assert n_tiles % unroll == 0 and (n_tiles == unroll or (n_tiles // 2) % unroll == 0)
    r_i = lax.broadcasted_iota(jnp.int32, (LANES, LANES), 0)
    c_i = lax.broadcasted_iota(jnp.int32, (LANES, LANES), 1)
    bd = ((r_i >> 6) == (c_i >> 6)).astype(F32)
    eye = (r_i == c_i).astype(F32)
    lane = lax.broadcasted_iota(jnp.int32, (1, LANES), 1)
    head_masks = ((lane < RWKV_HEAD).astype(F32), (lane >= RWKV_HEAD).astype(F32))
    consts = (bd, eye, head_masks, r_i, c_i)
    inv_n = 1.0 / RWKV_HEAD

    for z in range(2):
        st_ref[z] = jnp.zeros((LANES, LANES), F32) if zero_init else s0_ref[z]

    def block(i, visit):
        slots = []
        for z in range(2):
            for u in range(unroll):
                t = i * unroll + u
                slots.append((z, t if z == 0 else n_tiles - 1 - t))
        entries = []
        for z, t in slots:
            rows = pl.ds(pl.multiple_of(t * LANES, LANES), LANES)
            r_ref, k_ref, v_ref, lw_ref, a_ref = data[z]
            params = (kk_ref[z:z + 1, :], ka_ref[z:z + 1, :], rk_ref[z:z + 1, :])
            entries.append((r_ref[rows, :], k_ref[rows, :], v_ref[rows, :], lw_ref[rows, :], a_ref[rows, :],
                            params, z == 0))
        pre = _rwkv_precompute(entries, consts)

        ys = []
        st = [st_ref[0], st_ref[1]]
        for (z, _), (rp, y0, gc, q0, p_mat, _) in zip(slots, pre):
            ys.append(_dot_x3(rp, st[z], NT_DIMS) + y0)
            st[z] = st[z] * gc + q0 - bd * _dot_x3(st[z], p_mat)
        st_ref[0] = st[0]
        st_ref[1] = st[1]

        outs = []
        for (z, _), y, (_, _, _, _, _, bonus) in zip(slots, ys, pre):
            mean = _dot_x2l(y, bd) * inv_n
            yc = y - mean
            var = _dot_x2l(yc * yc, bd) * inv_n
            outs.append(yc * lax.rsqrt(var + GN_EPS) * gg_ref[z:z + 1, :] + gb_ref[z:z + 1, :] + bonus)

        if visit == "both":
            for u in range(unroll):
                rows = pl.ds(u * LANES, LANES)
                total = outs[u] + outs[unroll + (n_tiles - 1 - u)]
                o_ref[rows, :] = (total * g_ref[rows, :]).astype(o_ref.dtype)
            return
        for (z, t), out in zip(slots, outs):
            rows = pl.ds(pl.multiple_of(t * LANES, LANES), LANES)
            if visit == "first":
                osum_ref[rows, :] = out
            else:
                o_ref[rows, :] = ((osum_ref[rows, :] + out) * g_ref[rows, :]).astype(o_ref.dtype)

    if n_tiles == unroll:
        block(0, "both")
    else:
        def body(i, carry, visit):
            block(i, visit)
            return carry

        half = n_tiles // 2 // unroll
        lax.fori_loop(0, half, functools.partial(body, visit="first"), 0)
        lax.fori_loop(half, 2 * half, functools.partial(body, visit="second"), 0)
    if want_state:
        for z in range(2):
            sfin_ref[z] = st_ref[z]


def _rwkv_rec(rkv, lw, a, g, params, s0, prev_out, *, n_seq, seq_len, row_block0, total_rows):
    d = g.shape[1]
    n_pairs = d // LANES
    zero_init = s0 is None
    want_state = s0 is None
    aliased = prev_out is not None
    blk = (seq_len, LANES)

    def dir_specs(z):
        col = lambda b, p: (row_block0 + b, z * n_pairs + p)
        return ([pl.BlockSpec((None,) + blk, lambda b, p, j=j: (j, row_block0 + b, z * n_pairs + p))
                 for j in range(3)] + [pl.BlockSpec(blk, col), pl.BlockSpec(blk, col)])

    in_specs = dir_specs(0) + dir_specs(1)
    in_specs.append(pl.BlockSpec(blk, lambda b, p: (row_block0 + b, p)))
    in_specs += [pl.BlockSpec((2, LANES), lambda b, p: (0, p)) for _ in range(5)]
    args = [rkv, rkv, rkv, lw, a] * 2 + [g] + list(params)
    state_spec = pl.BlockSpec((None, 2, None, LANES, LANES), lambda b, p: (b, 0, p, 0, 0))
    if not zero_init:
        in_specs.append(state_spec)
        args.append(s0)
    io_alias = {}
    if aliased:
        in_specs.append(pl.BlockSpec(memory_space=pl.ANY))
        io_alias = {len(args): 0}
        args.append(prev_out)
    out_specs = [pl.BlockSpec(blk, lambda b, p: (row_block0 + b, p))]
    out_shape = [jax.ShapeDtypeStruct((total_rows, d), BF16)]
    if want_state:
        out_specs.append(state_spec)
        out_shape.append(jax.ShapeDtypeStruct((n_seq, 2, n_pairs, LANES, LANES), F32))
    return pl.pallas_call(
        functools.partial(_rwkv_rec_kernel, seq_len=seq_len, zero_init=zero_init,
                          want_state=want_state, aliased=aliased),
        grid=(n_seq, n_pairs),
        in_specs=in_specs,
        out_specs=out_specs,
        out_shape=out_shape,
        scratch_shapes=[pltpu.VMEM((seq_len, LANES), F32), pltpu.VMEM((2, LANES, LANES), F32)],
        input_output_aliases=io_alias,
        compiler_params=pltpu.CompilerParams(
            dimension_semantics=("parallel", "parallel"), vmem_limit_bytes=VMEM_LIMIT),
        name="rwkv_rec",
    )(*args)


def _rwkv_prepare_weights(mu, w_rkv, w0, w_la, w_lb, a0, a_la, a_lb, g_la, g_lb):
    d = mu.shape[1]
    rank_w = w_la.shape[2]
    rank_a = a_la.shape[2]
    rank_g = g_la.shape[1]
    rank_g_pad = -(-rank_g // LANES) * LANES

    def block_diag(w):
        rank = w.shape[1]
        out = jnp.zeros((2, rank, 2, d), w.dtype)
        out = out.at[0, :, 0, :].set(w[0]).at[1, :, 1, :].set(w[1])
        return out.reshape(2 * rank, 2 * d)

    lora = (
        w_la.reshape(d, 2 * rank_w).astype(BF16), block_diag(w_lb).astype(BF16), w0.reshape(1, 2 * d),
        a_la.reshape(d, 2 * rank_a).astype(BF16), block_diag(a_lb).astype(BF16), a0.reshape(1, 2 * d),
        jnp.pad(g_la, ((0, 0), (0, rank_g_pad - rank_g))).astype(BF16),
        jnp.pad(g_lb, ((0, rank_g_pad - rank_g), (0, 0))).astype(BF16),
    )
    mu3 = jnp.stack([mu[0], mu[2], mu[3]]).reshape(3, 1, d)
    return {"mu3": mu3, "w_rkv": w_rkv.astype(BF16), "lora": lora}


def _pair_states(s):
    n, _, h, hn, _ = s.shape
    s = s.reshape(n, 2, h // 2, 2, hn, hn)
    out = jnp.zeros((n, 2, h // 2, 2, hn, 2, hn), s.dtype)
    out = out.at[:, :, :, 0, :, 0, :].set(s[:, :, :, 0]).at[:, :, :, 1, :, 1, :].set(s[:, :, :, 1])
    return out.reshape(n, 2, h // 2, 2 * hn, 2 * hn)


def _unpair_states(sp):
    n, _, hp, hn2, _ = sp.shape
    hn = hn2 // 2
    sp = sp.reshape(n, 2, hp, 2, hn, 2, hn)
    return jnp.stack([sp[:, :, :, 0, :, 0, :], sp[:, :, :, 1, :, 1, :]], axis=3).reshape(n, 2, 2 * hp, hn, hn)


def _embed_kernel(xp_ref, xs_ref, pos_ref, o_ref, *, n_prompt_tiles):
    i = pl.program_id(0)

    @pl.when(i < n_prompt_tiles)
    def _():
        o_ref[...] = xp_ref[...]

    @pl.when(i >= n_prompt_tiles)
    def _():
        o_ref[...] = xs_ref[...] + pos_ref[...]


def _grid_pos_embed(n_tokens, d):
    rows = n_tokens // GRID_W
    quarter = d // 4
    omega = 1.0 / (POS_BASE ** (jnp.arange(quarter, dtype=F32) / quarter))
    r = jnp.arange(rows, dtype=F32)[:, None] * omega
    cc = jnp.arange(GRID_W, dtype=F32)[:, None] * omega
    row_emb = jnp.concatenate([jnp.sin(r), jnp.cos(r)], -1)
    col_emb = jnp.concatenate([jnp.sin(cc), jnp.cos(cc)], -1)
    emb = jnp.concatenate([
        jnp.broadcast_to(row_emb[:, None, :], (rows, GRID_W, d // 2)),
        jnp.broadcast_to(col_emb[None, :, :], (rows, GRID_W, d // 2))], -1)
    return emb.reshape(rows * GRID_W, d)


def _embed(xp, xs, pos, sample_len, tm=512):
    n_p, d = xp.shape
    n_s = xs.shape[0]
    npt = n_p // tm
    pos_tiles = sample_len // tm
    return pl.pallas_call(
        functools.partial(_embed_kernel, n_prompt_tiles=npt),
        grid=((n_p + n_s) // tm,),
        in_specs=[
            pl.BlockSpec((tm, d), lambda i: (jnp.minimum(i, npt - 1), 0)),
            pl.BlockSpec((tm, d), lambda i: (jnp.maximum(i - npt, 0), 0)),
            pl.BlockSpec((tm, d), lambda i: (lax.rem(jnp.maximum(i - npt, 0), pos_tiles), 0)),
        ],
        out_specs=pl.BlockSpec((tm, d), lambda i: (i, 0)),
        out_shape=jax.ShapeDtypeStruct((n_p + n_s, d), F32),
        compiler_params=pltpu.CompilerParams(
            dimension_semantics=("parallel",), vmem_limit_bytes=VMEM_LIMIT),
        name="embed",
    )(xp, xs, pos)


def kernel(x_prompt, x_sample, state_hgrn, state_rwkv, c, c_ctx, ada_w, ada_b, ln_g, ln_b, ffn_w_up, ffn_w_down, hgrn_w_in, hgrn_lb, hgrn_norm_g, hgrn_w_o, rwkv_mu, rwkv_w_rkv, rwkv_w0, rwkv_w_la, rwkv_w_lb, rwkv_a0, rwkv_a_la, rwkv_a_lb, rwkv_g_la, rwkv_g_lb, rwkv_k_k, rwkv_k_a, rwkv_r_k, rwkv_gn_g, rwkv_gn_b, rwkv_w_o):
    n_b, l_p, d = x_prompt.shape
    n_s, l_s, _ = x_sample.shape
    depth = ada_w.shape[0]
    n_p_rows = n_b * l_p
    total = n_p_rows + n_s * l_s
    assert n_p_rows % l_s == 0 and l_p % LANES == 0 and l_s % LANES == 0
    alpha = (2 * depth) ** 0.25
    a_heads = d // LANES

    def seg(i, tm):
        return _seg_index(i, tm, n_p_rows, l_s)

    x = _embed(x_prompt.reshape(n_p_rows, d), x_sample.reshape(n_s * l_s, d), _grid_pos_embed(l_s, d), l_s)

    cond8 = jnp.zeros((8, d), F32).at[0].set(c_ctx).at[1:1 + n_s].set(c)
    mods = _adaln(cond8, ada_w, ada_b).reshape(depth, 8, 6, d)

    lb_soft = jax.nn.softmax(hgrn_lb.astype(F32), axis=0)
    lower_bounds = jnp.cumsum(lb_soft, axis=0) - lb_soft[0]

    new_hgrn = []
    new_rwkv = []
    for l in range(depth):
        j = l // 2
        if l % 2 == 0:
            proj = _modmm(x, mods[l], hgrn_w_in[j].astype(BF16), seg)
            o, s_ctx = _hgrn_rec(proj, lower_bounds[j], hgrn_norm_g[j], None, None, n_seq=n_b, seq_len=l_p,
                                 row_block0=0, n_heads=a_heads, total_rows=total)
            (o,) = _hgrn_rec(proj, lower_bounds[j], hgrn_norm_g[j], state_hgrn[:, j], o, n_seq=n_s, seq_len=l_s,
                             row_block0=n_p_rows // l_s, n_heads=a_heads, total_rows=total)
            new_hgrn.append(s_ctx)
            w_o = hgrn_w_o[j]
        else:
            prep = _rwkv_prepare_weights(rwkv_mu[j], rwkv_w_rkv[j], rwkv_w0[j], rwkv_w_la[j], rwkv_w_lb[j],
                                         rwkv_a0[j], rwkv_a_la[j], rwkv_a_lb[j], rwkv_g_la[j], rwkv_g_lb[j])
            seq_kw = dict(n_prompt_rows=n_p_rows, prompt_len=l_p, sample_len=l_s)
            rkv = _rwkv_rkv(x, mods[l], prep["mu3"], prep["w_rkv"], seg, **seq_kw)
            lw, a, g = _rwkv_lora(x, mods[l], rwkv_mu[j], prep["lora"], seg, **seq_kw)
            params = (rwkv_k_k[j], rwkv_k_a[j], rwkv_r_k[j], rwkv_gn_g[j], rwkv_gn_b[j])
            o, s_ctx = _rwkv_rec(rkv, lw, a, g, params, None, None, n_seq=n_b, seq_len=l_p,
                                 row_block0=0, total_rows=total)
            (o,) = _rwkv_rec(rkv, lw, a, g, params, _pair_states(state_rwkv[:, j]), o, n_seq=n_s, seq_len=l_s,
                             row_block0=n_p_rows // l_s, total_rows=total)
            new_rwkv.append(_unpair_states(s_ctx))
            w_o = rwkv_w_o[j]
        x = _out_ln(o, x, mods[l], w_o.astype(BF16), ln_g[l, 0:1], ln_b[l, 0:1], seg, alpha)
        x = _mlp_ln(x, mods[l], ffn_w_up[l].astype(BF16), ffn_w_down[l].astype(BF16),
                    ln_g[l, 1:2], ln_b[l, 1:2], seg, alpha)

    y_prompt = x[:n_p_rows].reshape(n_b, l_p, d)
    y_sample = x[n_p_rows:].reshape(n_s, l_s, d)
    return (y_prompt, y_sample, jnp.stack(new_hgrn, axis=1), jnp.stack(new_rwkv, axis=1))
```

```python
import functools

import jax
import jax.numpy as jnp
from jax import lax
from jax.experimental import pallas as pl
from jax.experimental.pallas import tpu as pltpu

F32 = jnp.float32
BF16 = jnp.bfloat16
HIGHEST = lax.Precision.HIGHEST

LN_EPS = 1e-5
RMS_EPS = 1e-6
GN_EPS = 64e-5
DECAY_SCALE = 0.606531
EXP_CLIP = 80.0
POS_BASE = 10000.0
GRID_W = 64

LANES = 128
HGRN_CHUNK = 16
HGRN_SAFE_EXPONENT = 40.0
RWKV_HEAD = 64
VMEM_LIMIT = 56 * 1024 * 1024

NT_DIMS = (((1,), (1,)), ((), ()))
TN_DIMS = (((0,), (0,)), ((), ()))


def _dot(a, b, dims=None, precision=None):
    if dims is None:
        return jnp.dot(a, b, preferred_element_type=F32, precision=precision)
    return lax.dot_general(a, b, dims, preferred_element_type=F32, precision=precision)


def _bdot(a, b, dims=None):
    return _dot(a.astype(BF16), b.astype(BF16), dims)


def _hdot(a, b, dims=None):
    return _dot(a, b, dims, precision=HIGHEST)


def _split_bf16(x):
    hi = x.astype(BF16)
    return hi, (x - hi.astype(F32)).astype(BF16)


def _dot_x3(a, b, dims=None):
    ah, al = _split_bf16(a)
    bh, bl = _split_bf16(b)
    if dims == NT_DIMS:
        a_cat = jnp.concatenate([ah, al], axis=1)
        b_half = jnp.concatenate([bh, bl], axis=0)
        b_cat = jnp.concatenate([b_half, b_half], axis=1)
        n = b.shape[0]
    else:
        a_cat = jnp.concatenate([ah, al], axis=0 if dims == TN_DIMS else 1)
        b_half = jnp.concatenate([bh, bl], axis=1)
        b_cat = jnp.concatenate([b_half, b_half], axis=0)
        n = b.shape[1]
    r = _dot(a_cat, b_cat, dims)
    return r[:, :n] + r[:, n:]


def _dot_x2l(a, b):
    ah, al = _split_bf16(a)
    b = b.astype(BF16)
    return _dot(jnp.concatenate([ah, al], axis=1), jnp.concatenate([b, b], axis=0))


def _dot_x2r(a, b):
    bh, bl = _split_bf16(b)
    n = b.shape[1]
    r = _dot(a.astype(BF16), jnp.concatenate([bh, bl], axis=1))
    return r[:, :n] + r[:, n:]


def _sigmoid(x):
    return jax.nn.sigmoid(x)


def _layer_norm(x, g, b):
    mu = jnp.mean(x, -1, keepdims=True)
    xc = x - mu
    var = jnp.mean(xc * xc, -1, keepdims=True)
    return xc * lax.rsqrt(var + LN_EPS) * g + b


def _seg_index(i, tm, n_prompt_rows, sample_len):
    start = i * tm
    return jnp.where(start < n_prompt_rows, 0, 1 + (start - n_prompt_rows) // sample_len)


def _adaln_kernel(c_ref, w_ref, b_ref, o_ref):
    c = c_ref[...]
    s = c * _sigmoid(c)
    o_ref[...] = _bdot(s, w_ref[...]) + b_ref[...]


def _adaln(cond8, ada_w, ada_b, tn=1536):
    depth, d, n = ada_w.shape
    return pl.pallas_call(
        _adaln_kernel,
        grid=(depth, n // tn),
        in_specs=[
            pl.BlockSpec((8, d), lambda l, j: (0, 0)),
            pl.BlockSpec((None, d, tn), lambda l, j: (l, 0, j)),
            pl.BlockSpec((None, 1, tn), lambda l, j: (l, 0, j)),
        ],
        out_specs=pl.BlockSpec((None, 8, tn), lambda l, j: (l, 0, j)),
        out_shape=jax.ShapeDtypeStruct((depth, 8, n), F32),
        compiler_params=pltpu.CompilerParams(
            dimension_semantics=("parallel", "parallel"), vmem_limit_bytes=VMEM_LIMIT),
        name="adaln",
    )(cond8, ada_w, ada_b.reshape(depth, 1, n))


def _modmm_kernel(x_ref, mod_ref, w_ref, o_ref, xb_ref):
    @pl.when(pl.program_id(1) == 0)
    def _():
        m = mod_ref[...]
        xb_ref[...] = (x_ref[...] * (1.0 + m[1:2]) + m[0:1]).astype(BF16)

    o_ref[...] = _dot(xb_ref[...], w_ref[...])


def _modmm(x, mods, w, seg, tm=512, tn=1024):
    t, d = x.shape
    n = w.shape[1]
    return pl.pallas_call(
        _modmm_kernel,
        grid=(t // tm, n // tn),
        in_specs=[
            pl.BlockSpec((tm, d), lambda i, j: (i, 0)),
            pl.BlockSpec((None, 6, d), lambda i, j: (seg(i, tm), 0, 0)),
            pl.BlockSpec((d, tn), lambda i, j: (0, j)),
        ],
        out_specs=pl.BlockSpec((tm, tn), lambda i, j: (i, j)),
        out_shape=jax.ShapeDtypeStruct((t, n), F32),
        scratch_shapes=[pltpu.VMEM((tm, d), BF16)],
        compiler_params=pltpu.CompilerParams(
            dimension_semantics=("parallel", "arbitrary"), vmem_limit_bytes=VMEM_LIMIT),
        name="modmm",
    )(x, mods, w)


def _out_ln_kernel(o_ref, x_ref, mod_ref, w_ref, g_ref, b_ref, y_ref, *, alpha):
    y = _dot(o_ref[...], w_ref[...])
    gate = mod_ref[...][2:3]
    y_ref[...] = _layer_norm(alpha * x_ref[...] + gate * y, g_ref[...], b_ref[...])


def _out_ln(o, x, mods, w, g, b, seg, alpha, tm=512):
    t, d = x.shape
    return pl.pallas_call(
        functools.partial(_out_ln_kernel, alpha=alpha),
        grid=(t // tm,),
        in_specs=[
            pl.BlockSpec((tm, d), lambda i: (i, 0)),
            pl.BlockSpec((tm, d), lambda i: (i, 0)),
            pl.BlockSpec((None, 6, d), lambda i: (seg(i, tm), 0, 0)),
            pl.BlockSpec((d, d), lambda i: (0, 0)),
            pl.BlockSpec((1, d), lambda i: (0, 0)),
            pl.BlockSpec((1, d), lambda i: (0, 0)),
        ],
        out_specs=pl.BlockSpec((tm, d), lambda i: (i, 0)),
        out_shape=jax.ShapeDtypeStruct((t, d), F32),
        compiler_params=pltpu.CompilerParams(
            dimension_semantics=("parallel",), vmem_limit_bytes=VMEM_LIMIT),
        name="out_ln",
    )(o, x, mods, w, g, b)


def _mlp_ln_kernel(x_ref, mod_ref, wu_ref, wd_ref, g_ref, b_ref, y_ref, xb_ref, acc_ref, *, alpha):
    f = pl.program_id(1)

    @pl.when(f == 0)
    def _():
        m = mod_ref[...]
        xb_ref[...] = (x_ref[...] * (1.0 + m[4:5]) + m[3:4]).astype(BF16)
        acc_ref[...] = jnp.zeros_like(acc_ref)

    h = jnp.maximum(_dot(xb_ref[...], wu_ref[...]), 0.0)
    acc_ref[...] += _dot((h * h).astype(BF16), wd_ref[...])

    @pl.when(f == pl.num_programs(1) - 1)
    def _():
        gate = mod_ref[...][5:6]
        y_ref[...] = _layer_norm(alpha * x_ref[...] + gate * acc_ref[...], g_ref[...], b_ref[...])


def _mlp_ln(x, mods, w_up, w_down, g, b, seg, alpha, tm=512, tf=1024):
    t, d = x.shape
    dff = w_up.shape[1]
    return pl.pallas_call(
        functools.partial(_mlp_ln_kernel, alpha=alpha),
        grid=(t // tm, dff // tf),
        in_specs=[
            pl.BlockSpec((tm, d), lambda i, f: (i, 0)),
            pl.BlockSpec((None, 6, d), lambda i, f: (seg(i, tm), 0, 0)),
            pl.BlockSpec((d, tf), lambda i, f: (0, f)),
            pl.BlockSpec((tf, d), lambda i, f: (f, 0)),
            pl.BlockSpec((1, d), lambda i, f: (0, 0)),
            pl.BlockSpec((1, d), lambda i, f: (0, 0)),
        ],
        out_specs=pl.BlockSpec((tm, d), lambda i, f: (i, 0)),
        out_shape=jax.ShapeDtypeStruct((t, d), F32),
        scratch_shapes=[pltpu.VMEM((tm, d), BF16), pltpu.VMEM((tm, d), F32)],
        compiler_params=pltpu.CompilerParams(
            dimension_semantics=("parallel", "arbitrary"), vmem_limit_bytes=VMEM_LIMIT),
        name="mlp_ln",
    )(x, mods, w_up, w_down, g, b)


def _hgrn_precompute(entries, r_i, c_i):
    pos = r_i & (HGRN_CHUNK - 1)
    same_chunk = (r_i >> 4) == (c_i >> 4)
    blk = same_chunk.astype(F32)
    pre = []
    for qr, zz, v, lbz, fwd in entries:
        causal = same_chunk & ((c_i <= r_i) if fwd else (c_i >= r_i))
        q = qr * _sigmoid(qr)
        e_abs = jnp.exp(-jnp.abs(zz))
        inv_1p = 1.0 / (1.0 + e_abs)
        log_f = jnp.minimum(zz, 0.0) + jnp.log((1.0 + lbz * jnp.exp(jnp.minimum(-zz, EXP_CLIP))) * inv_1p)
        k = (1.0 - lbz) * jnp.where(zz > 0.0, e_abs, 1.0) * inv_1p
        sums = _dot_x2r(jnp.concatenate([causal.astype(F32), blk], axis=0), log_f)
        cum = sums[:LANES]
        ctot = sums[LANES:]
        pre.append(dict(q=q, k=k, v=v, cum=cum, ctot=ctot, fwd=fwd, causal=causal,
                        q_in=q * jnp.exp(cum), k_out=k * jnp.exp(ctot - cum), dec=jnp.exp(ctot)))

    def scores_factored():
        out = []
        for e in pre:
            half = 0.5 * e["ctot"]
            qk = _dot_x3(e["q"] * jnp.exp(e["cum"] - half), e["k"] * jnp.exp(half - e["cum"]), NT_DIMS)
            out.append(jnp.where(e["causal"], qk, 0.0))
        return out

    def scores_pairwise():
        out = []
        for e in pre:
            fwd, q, k, cum = e["fwd"], e["q"], e["k"], e["cum"]
            scores = jnp.zeros((LANES, LANES), F32)
            for d in range(HGRN_CHUNK):
                if d == 0:
                    kr, cr = k, cum
                else:
                    sh = d if fwd else LANES - d
                    kr = pltpu.roll(k, sh, 0)
                    cr = pltpu.roll(cum, sh, 0)
                valid = (pos >= d) if fwd else (pos <= HGRN_CHUNK - 1 - d)
                ex = jnp.exp(jnp.where(valid, cum - cr, 0.0))
                s = jnp.sum(q * kr * ex, axis=-1, keepdims=True)
                tgt = (c_i == r_i - d) if fwd else (c_i == r_i + d)
                scores = scores + jnp.where(tgt & valid, s, 0.0)
            out.append(scores)
        return out

    lowest = pre[0]["ctot"]
    for e in pre[1:]:
        lowest = jnp.minimum(lowest, e["ctot"])
    scores = lax.cond(jnp.min(lowest) >= -2.0 * HGRN_SAFE_EXPONENT, scores_factored, scores_pairwise)

    chunk_of_lane = c_i >> 4
    out = []
    for e, sc in zip(pre, scores):
        o_intra = _bdot(sc, e["v"])
        v_t = e["v"].T
        lhs = jnp.concatenate(
            [jnp.where(chunk_of_lane == c, v_t, 0.0) for c in range(LANES // HGRN_CHUNK)], axis=0)
        upd = _bdot(lhs, e["k_out"])
        out.append((e["q_in"], o_intra, upd, e["dec"]))
    return out


def _hgrn_rec_kernel(*refs, seq_len, zero_init, want_state, aliased):
    qf_ref, zf_ref, vf_ref, qb_ref, zb_ref, vb_ref, gate_ref, lb_ref, ng_ref = refs[:9]
    pos = 9
    s0_ref = None
    if not zero_init:
        s0_ref = refs[pos]
        pos += 1
    if aliased:
        pos += 1
    o_ref = refs[pos]
    pos += 1
    sfin_ref = None
    if want_state:
        sfin_ref = refs[pos]
        pos += 1
    osum_ref, st_ref = refs[pos:pos + 2]

    n_tiles = seq_len // LANES
    n_chunks = LANES // HGRN_CHUNK
    unroll = 2
    assert n_tiles % unroll == 0 and (n_tiles == unroll or (n_tiles // 2) % unroll == 0)
    r_i = lax.broadcasted_iota(jnp.int32, (LANES, LANES), 0)
    c_i = lax.broadcasted_iota(jnp.int32, (LANES, LANES), 1)
    lb = lb_ref[...]
    data = ((qf_ref, zf_ref, vf_ref), (qb_ref, zb_ref, vb_ref))

    for z in range(2):
        if zero_init:
            st_ref[z] = jnp.zeros((LANES, LANES), F32)
        else:
            st_ref[z] = s0_ref[z].T

    def finish(rows, o):
        o = o * lax.rsqrt(jnp.mean(o * o, -1, keepdims=True) + RMS_EPS)
        g = gate_ref[rows, :]
        o_ref[rows, :] = (o * ng_ref[...] * (g * _sigmoid(g))).astype(o_ref.dtype)

    def block(i, visit):
        slots = []
        for z in range(2):
            for u in range(unroll):
                t = i * unroll + u
                slots.append((z, t if z == 0 else n_tiles - 1 - t))
        entries = []
        for z, t in slots:
            rows = pl.ds(pl.multiple_of(t * LANES, LANES), LANES)
            q_ref, z_ref, v_ref = data[z]
            entries.append((q_ref[rows, :], z_ref[rows, :], v_ref[rows, :], lb[z:z + 1], z == 0))
        pre = _hgrn_precompute(entries, r_i, c_i)

        st = [st_ref[0], st_ref[1]]
        inter = [[None] * n_chunks for _ in slots]
        for u in range(unroll):
            for step in range(n_chunks):
                for z in range(2):
                    idx = z * unroll + u
                    q_in, _, upd, dec = pre[idx]
                    c = step if z == 0 else n_chunks - 1 - step
                    lo = c * HGRN_CHUNK
                    inter[idx][c] = _bdot(q_in[lo:lo + HGRN_CHUNK], st[z], NT_DIMS)
                    st[z] = st[z] * dec[lo:lo + 1] + upd[c * LANES:(c + 1) * LANES]
        st_ref[0] = st[0]
        st_ref[1] = st[1]
        outs = [p[1] + jnp.concatenate(o, axis=0) for p, o in zip(pre, inter)]

        if visit == "both":
            for u in range(unroll):
                finish(pl.ds(u * LANES, LANES), outs[u] + outs[unroll + (n_tiles - 1 - u)])
            return
        for (z, t), out in zip(slots, outs):
            rows = pl.ds(pl.multiple_of(t * LANES, LANES), LANES)
            if visit == "first":
                osum_ref[rows, :] = out
            else:
                finish(rows, osum_ref[rows, :] + out)

    if n_tiles == unroll:
        block(0, "both")
    else:
        def body(i, carry, visit):
            block(i, visit)
            return carry

        half = n_tiles // 2 // unroll
        lax.fori_loop(0, half, functools.partial(body, visit="first"), 0)
        lax.fori_loop(half, 2 * half, functools.partial(body, visit="second"), 0)
    if want_state:
        for z in range(2):
            sfin_ref[z] = st_ref[z].T


def _hgrn_rec(proj, lb, norm_g, s0, prev_out, *, n_seq, seq_len, row_block0, n_heads, total_rows):
    d = n_heads * LANES
    zero_init = s0 is None
    want_state = s0 is None
    aliased = prev_out is not None

    def col(block):
        return lambda b, h: (row_block0 + b, block * n_heads + h)

    blk = (seq_len, LANES)
    in_specs = [pl.BlockSpec(blk, col(0)), pl.BlockSpec(blk, col(1)), pl.BlockSpec(blk, col(2)),
                pl.BlockSpec(blk, col(3)), pl.BlockSpec(blk, col(4)), pl.BlockSpec(blk, col(5)),
                pl.BlockSpec(blk, col(6)),
                pl.BlockSpec((2, LANES), lambda b, h: (0, h)),
                pl.BlockSpec((1, LANES), lambda b, h: (0, h))]
    args = [proj] * 7 + [lb, norm_g.reshape(1, d)]
    if not zero_init:
        in_specs.append(pl.BlockSpec((None, 2, None, LANES, LANES), lambda b, h: (b, 0, h, 0, 0)))
        args.append(s0)
    io_alias = {}
    if aliased:
        in_specs.append(pl.BlockSpec(memory_space=pl.ANY))
        io_alias = {len(args): 0}
        args.append(prev_out)
    out_specs = [pl.BlockSpec(blk, lambda b, h: (row_block0 + b, h))]
    out_shape = [jax.ShapeDtypeStruct((total_rows, d), BF16)]
    if want_state:
        out_specs.append(pl.BlockSpec((None, 2, None, LANES, LANES), lambda b, h: (b, 0, h, 0, 0)))
        out_shape.append(jax.ShapeDtypeStruct((n_seq, 2, n_heads, LANES, LANES), F32))
    res = pl.pallas_call(
        functools.partial(_hgrn_rec_kernel, seq_len=seq_len, zero_init=zero_init,
                          want_state=want_state, aliased=aliased),
        grid=(n_seq, n_heads),
        in_specs=in_specs,
        out_specs=out_specs,
        out_shape=out_shape,
        scratch_shapes=[pltpu.VMEM((seq_len, LANES), F32), pltpu.VMEM((2, LANES, LANES), F32)],
        input_output_aliases=io_alias,
        compiler_params=pltpu.CompilerParams(
            dimension_semantics=("parallel", "parallel"), vmem_limit_bytes=VMEM_LIMIT),
        name="hgrn_rec",
    )(*args)
    return res


def _int_mod(x, n):
    return x & (n - 1) if n & (n - 1) == 0 else lax.rem(x, n)


def _token_shift(x_ref, xp_ref, xn_ref, mod_ref, tile_start, n_prompt_rows, prompt_len, sample_len):
    m = mod_ref[...]
    sh, sc = m[0:1], 1.0 + m[1:2]
    h = x_ref[...] * sc + sh
    h_before = xp_ref[7:8, :] * sc + sh
    h_after = xn_ref[0:1, :] * sc + sh
    tm = h.shape[0]
    rr = lax.broadcasted_iota(jnp.int32, (tm, 1), 0)
    grow = tile_start + rr
    in_prompt = grow < n_prompt_rows
    pos = jnp.where(in_prompt, _int_mod(grow, prompt_len), _int_mod(grow - n_prompt_rows, sample_len))
    last = jnp.where(in_prompt, prompt_len - 1, sample_len - 1)
    prev = jnp.where(rr == 0, h_before, pltpu.roll(h, 1, 0))
    prev = jnp.where(pos == 0, 0.0, prev)
    nxt = jnp.where(rr == tm - 1, h_after, pltpu.roll(h, tm - 1, 0))
    nxt = jnp.where(pos == last, 0.0, nxt)
    return h, 0.5 * (prev + nxt) - h


def _rwkv_rkv_kernel(x_ref, xp_ref, xn_ref, mod_ref, mu_ref, w_ref, o_ref, h_ref, xx_ref, *, tm, seq_info):
    @pl.when(pl.program_id(1) == 0)
    def _():
        h, xx = _token_shift(x_ref, xp_ref, xn_ref, mod_ref, pl.program_id(0) * tm, *seq_info)
        h_ref[...] = h
        xx_ref[...] = xx

    xs = h_ref[...] + xx_ref[...] * mu_ref[...]
    o_ref[...] = _dot(xs.astype(BF16), w_ref[...])


def _halo_specs(tm, d, t, n_grid_axes):
    nb = t // 8

    def before(i, *_):
        return (jnp.maximum(i * (tm // 8) - 1, 0), 0)

    def after(i, *_):
        return (jnp.minimum((i + 1) * (tm // 8), nb - 1), 0)

    del n_grid_axes
    return pl.BlockSpec((8, d), before), pl.BlockSpec((8, d), after)


def _rwkv_rkv(x, mods, mu3, w_rkv, seg, *, n_prompt_rows, prompt_len, sample_len, tm=256):
    t, d = x.shape
    n = w_rkv.shape[2]
    before, after = _halo_specs(tm, d, t, 2)
    return pl.pallas_call(
        functools.partial(_rwkv_rkv_kernel, tm=tm, seq_info=(n_prompt_rows, prompt_len, sample_len)),
        grid=(t // tm, 3),
        in_specs=[
            pl.BlockSpec((tm, d), lambda i, j: (i, 0)),
            before, after,
            pl.BlockSpec((None, 6, d), lambda i, j: (seg(i, tm), 0, 0)),
            pl.BlockSpec((None, 1, d), lambda i, j: (j, 0, 0)),
            pl.BlockSpec((None, d, n), lambda i, j: (j, 0, 0)),
        ],
        out_specs=pl.BlockSpec((None, tm, n), lambda i, j: (j, i, 0)),
        out_shape=jax.ShapeDtypeStruct((3, t, n), F32),
        scratch_shapes=[pltpu.VMEM((tm, d), F32), pltpu.VMEM((tm, d), F32)],
        compiler_params=pltpu.CompilerParams(
            dimension_semantics=("parallel", "arbitrary"), vmem_limit_bytes=VMEM_LIMIT),
        name="rwkv_rkv",
    )(x, x, x, mods, mu3, w_rkv)


def _rwkv_lora_kernel(x_ref, xp_ref, xn_ref, mod_ref, mu_ref, wla_ref, wlb_ref, w0_ref, ala_ref, alb_ref,
                      a0_ref, gla_ref, glb_ref, lw_ref, a_ref, g_ref, *, tm, seq_info):
    h, xx = _token_shift(x_ref, xp_ref, xn_ref, mod_ref, pl.program_id(0) * tm, *seq_info)
    mu = mu_ref[...]
    xs_w = (h + xx * mu[1:2]).astype(BF16)
    xs_a = (h + xx * mu[4:5]).astype(BF16)
    xs_g = (h + xx * mu[5:6]).astype(BF16)
    zw = w0_ref[...] + _bdot(jnp.tanh(_dot(xs_w, wla_ref[...])), wlb_ref[...])
    lw_ref[...] = -DECAY_SCALE * _sigmoid(zw)
    za = a0_ref[...] + _bdot(_dot(xs_a, ala_ref[...]), alb_ref[...])
    a_ref[...] = _sigmoid(za)
    g_ref[...] = _bdot(_sigmoid(_dot(xs_g, gla_ref[...])), glb_ref[...])


def _rwkv_lora(x, mods, mu, weights, seg, *, n_prompt_rows, prompt_len, sample_len, tm=256):
    t, d = x.shape
    before, after = _halo_specs(tm, d, t, 1)

    def whole(arr):
        return pl.BlockSpec(arr.shape, lambda i: (0,) * arr.ndim)

    return pl.pallas_call(
        functools.partial(_rwkv_lora_kernel, tm=tm, seq_info=(n_prompt_rows, prompt_len, sample_len)),
        grid=(t // tm,),
        in_specs=[pl.BlockSpec((tm, d), lambda i: (i, 0)), before, after,
                  pl.BlockSpec((None, 6, d), lambda i: (seg(i, tm), 0, 0)), whole(mu)]
                 + [whole(w) for w in weights],
        out_specs=[pl.BlockSpec((tm, 2 * d), lambda i: (i, 0)),
                   pl.BlockSpec((tm, 2 * d), lambda i: (i, 0)),
                   pl.BlockSpec((tm, d), lambda i: (i, 0))],
        out_shape=[jax.ShapeDtypeStruct((t, 2 * d), F32), jax.ShapeDtypeStruct((t, 2 * d), F32),
                   jax.ShapeDtypeStruct((t, d), F32)],
        compiler_params=pltpu.CompilerParams(
            dimension_semantics=("parallel",), vmem_limit_bytes=VMEM_LIMIT),
        name="rwkv_lora",
    )(x, x, x, mods, mu, *weights)


def _rwkv_precompute(entries, consts):
    bd, eye, head_masks, r_i, c_i = consts
    n = len(entries)
    pre = []
    for r, k, v, lw, a, (kkp, kap, rkp), fwd in entries:
        tri = ((c_i <= r_i) if fwd else (c_i >= r_i)).astype(F32)
        kk = k * kkp
        kk = kk / jnp.maximum(jnp.sqrt(_dot_x2l(kk * kk, bd)), 1e-12)
        k2 = k * (1.0 + (a - 1.0) * kap)
        bonus = _dot_x2l(r * k2 * rkp, bd) * v
        b = kk * a
        cl = _dot_x2r(tri, lw)
        cle = cl - lw
        cm = cl[LANES // 2:LANES // 2 + 1]
        ct = cl[LANES - 1:LANES] if fwd else cl[0:1]
        e_inv = jnp.exp(cm - cl)
        e_out = jnp.exp(ct - cl)
        pre.append(dict(
            v=v, bonus=bonus, fwd=fwd,
            kkt=kk * jnp.exp(cle - cm), rt=r * jnp.exp(cl - cm), kh=k2 * e_inv, bh=b * e_inv,
            kkd=kk * jnp.exp(cle), rd=r * jnp.exp(cl), kg=k2 * e_out, bg=b * e_out, gc=jnp.exp(ct)))

    for e in pre:
        lhs = jnp.concatenate([e["kkt"] * head_masks[0], e["kkt"] * head_masks[1],
                               e["rt"] * head_masks[0], e["rt"] * head_masks[1]], axis=0)
        e["gk"] = _dot_x3(lhs, e["kh"], NT_DIMS)
        e["gb"] = _dot_x3(lhs, e["bh"], NT_DIMS)

    chains = []
    for e in pre:
        strict = (c_i < r_i) if e["fwd"] else (c_i > r_i)
        incl = (c_i <= r_i) if e["fwd"] else (c_i >= r_i)
        for hd in range(2):
            lo = hd * LANES
            chains.append(dict(
                e=e, hd=hd,
                a_k=jnp.where(strict, e["gk"][lo:lo + LANES], 0.0),
                a_b=jnp.where(strict, e["gb"][lo:lo + LANES], 0.0),
                b_k=jnp.where(incl, e["gk"][2 * LANES + lo:3 * LANES + lo], 0.0),
                b_b=jnp.where(incl, e["gb"][2 * LANES + lo:3 * LANES + lo], 0.0)))

    for c in chains:
        c["m"] = eye - jnp.where((r_i >> 1) == (c_i >> 1), c["a_b"], 0.0)
    for log_k in range(1, 7):
        off = ((r_i >> (log_k + 1)) == (c_i >> (log_k + 1))) & ((r_i >> log_k) != (c_i >> log_k))
        for c in chains:
            c["t"] = _dot_x3(jnp.where(off, c["a_b"], 0.0), c["m"])
        for c in chains:
            c["m"] = c["m"] - _dot_x3(c["m"], c["t"])

    for c in chains:
        e = c["e"]
        c["v_h"] = e["v"] * head_masks[c["hd"]]
        c["kkd_h"] = e["kkd"] * head_masks[c["hd"]]
        c["akv"] = _dot_x3(c["a_k"], c["v_h"])
    for c in chains:
        c["wt"] = _dot_x3(c["m"], c["kkd_h"])
        c["u0"] = _dot_x3(c["m"], c["akv"])
    for c in chains:
        c["y0"] = _dot_x3(c["b_k"], c["v_h"]) - _dot_x3(c["b_b"], c["u0"])
        c["rp"] = _dot_x3(c["b_b"], c["wt"])

    out = []
    for i, e in enumerate(pre):
        c0, c1 = chains[2 * i], chains[2 * i + 1]
        wt = c0["wt"] + c1["wt"]
        u0 = c0["u0"] + c1["u0"]
        q0 = bd * (_dot_x3(e["v"], e["kg"], TN_DIMS) - _dot_x3(u0.T, e["bg"]))
        p_mat = _dot_x3(wt, e["bg"], TN_DIMS)
        out.append((e["rd"] - c0["rp"] - c1["rp"], c0["y0"] + c1["y0"], e["gc"], q0, p_mat, e["bonus"]))
    assert len(out) == n
    return out


def _rwkv_rec_kernel(*refs, seq_len, zero_init, want_state, aliased):
    data = (refs[0:5], refs[5:10])
    g_ref, kk_ref, ka_ref, rk_ref, gg_ref, gb_ref = refs[10:16]
    pos = 16
    s0_ref = None
    if not zero_init:
        s0_ref = refs[pos]
        pos += 1
    if aliased:
        pos += 1
    o_ref = refs[pos]
    pos += 1
    sfin_ref = None
    if want_state:
        sfin_ref = refs[pos]
        pos += 1
    osum_ref, st_ref = refs[pos:pos + 2]

    n_tiles = seq_len // LANES
    unroll = 2
    assert n_tiles % unroll == 0 and (n_tiles == unroll or (n_tiles // 2) % unroll == 0)
    r_i = lax.broadcasted_iota(jnp.int32, (LANES, LANES), 0)
    c_i = lax.broadcasted_iota(jnp.int32, (LANES, LANES), 1)
    bd = ((r_i >> 6) == (c_i >> 6)).astype(F32)
    eye = (r_i == c_i).astype(F32)
    lane = lax.broadcasted_iota(jnp.int32, (1, LANES), 1)
    head_masks = ((lane < RWKV_HEAD).astype(F32), (lane >= RWKV_HEAD).astype(F32))
    consts = (bd, eye, head_masks, r_i, c_i)
    inv_n = 1.0 / RWKV_HEAD

    for z in range(2):
        st_ref[z] = jnp.zeros((LANES, LANES), F32) if zero_init else s0_ref[z]

    def block(i, visit):
        slots = []
        for z in range(2):
            for u in range(unroll):
                t = i * unroll + u
                slots.append((z, t if z == 0 else n_tiles - 1 - t))
        entries = []
        for z, t in slots:
            rows = pl.ds(pl.multiple_of(t * LANES, LANES), LANES)
            r_ref, k_ref, v_ref, lw_ref, a_ref = data[z]
            params = (kk_ref[z:z + 1, :], ka_ref[z:z + 1, :], rk_ref[z:z + 1, :])
            entries.append((r_ref[rows, :], k_ref[rows, :], v_ref[rows, :], lw_ref[rows, :], a_ref[rows, :],
                            params, z == 0))
        pre = _rwkv_precompute(entries, consts)

        ys = []
        st = [st_ref[0], st_ref[1]]
        for (z, _), (rp, y0, gc, q0, p_mat, _) in zip(slots, pre):
            ys.append(_dot_x3(rp, st[z], NT_DIMS) + y0)
            st[z] = st[z] * gc + q0 - bd * _dot_x3(st[z], p_mat)
        st_ref[0] = st[0]
        st_ref[1] = st[1]

        outs = []
        for (z, _), y, (_, _, _, _, _, bonus) in zip(slots, ys, pre):
            mean = _dot_x2l(y, bd) * inv_n
            yc = y - mean
            var = _dot_x2l(yc * yc, bd) * inv_n
            outs.append(yc * lax.rsqrt(var + GN_EPS) * gg_ref[z:z + 1, :] + gb_ref[z:z + 1, :] + bonus)

        if visit == "both":
            for u in range(unroll):
                rows = pl.ds(u * LANES, LANES)
                total = outs[u] + outs[unroll + (n_tiles - 1 - u)]
                o_ref[rows, :] = (total * g_ref[rows, :]).astype(o_ref.dtype)
            return
        for (z, t), out in zip(slots, outs):
            rows = pl.ds(pl.multiple_of(t * LANES, LANES), LANES)
            if visit == "first":
                osum_ref[rows, :] = out
            else:
                o_ref[rows, :] = ((osum_ref[rows, :] + out) * g_ref[rows, :]).astype(o_ref.dtype)

    if n_tiles == unroll:
        block(0, "both")
    else:
        def body(i, carry, visit):
            block(i, visit)
            return carry

        half = n_tiles // 2 // unroll
        lax.fori_loop(0, half, functools.partial(body, visit="first"), 0)
        lax.fori_loop(half, 2 * half, functools.partial(body, visit="second"), 0)
    if want_state:
        for z in range(2):
            sfin_ref[z] = st_ref[z]


def _rwkv_rec(rkv, lw, a, g, params, s0, prev_out, *, n_seq, seq_len, row_block0, total_rows):
    d = g.shape[1]
    n_pairs = d // LANES
    zero_init = s0 is None
    want_state = s0 is None
    aliased = prev_out is not None
    blk = (seq_len, LANES)

    def dir_specs(z):
        col = lambda b, p: (row_block0 + b, z * n_pairs + p)
        return ([pl.BlockSpec((None,) + blk, lambda b, p, j=j: (j, row_block0 + b, z * n_pairs + p))
                 for j in range(3)] + [pl.BlockSpec(blk, col), pl.BlockSpec(blk, col)])

    in_specs = dir_specs(0) + dir_specs(1)
    in_specs.append(pl.BlockSpec(blk, lambda b, p: (row_block0 + b, p)))
    in_specs += [pl.BlockSpec((2, LANES), lambda b, p: (0, p)) for _ in range(5)]
    args = [rkv, rkv, rkv, lw, a] * 2 + [g] + list(params)
    state_spec = pl.BlockSpec((None, 2, None, LANES, LANES), lambda b, p: (b, 0, p, 0, 0))
    if not zero_init:
        in_specs.append(state_spec)
        args.append(s0)
    io_alias = {}
    if aliased:
        in_specs.append(pl.BlockSpec(memory_space=pl.ANY))
        io_alias = {len(args): 0}
        args.append(prev_out)
    out_specs = [pl.BlockSpec(blk, lambda b, p: (row_block0 + b, p))]
    out_shape = [jax.ShapeDtypeStruct((total_rows, d), BF16)]
    if want_state:
        out_specs.append(state_spec)
        out_shape.append(jax.ShapeDtypeStruct((n_seq, 2, n_pairs, LANES, LANES), F32))
    return pl.pallas_call(
        functools.partial(_rwkv_rec_kernel, seq_len=seq_len, zero_init=zero_init,
                          want_state=want_state, aliased=aliased),
        grid=(n_seq, n_pairs),
        in_specs=in_specs,
        out_specs=out_specs,
        out_shape=out_shape,
        scratch_shapes=[pltpu.VMEM((seq_len, LANES), F32), pltpu.VMEM((2, LANES, LANES), F32)],
        input_output_aliases=io_alias,
        compiler_params=pltpu.CompilerParams(
            dimension_semantics=("parallel", "parallel"), vmem_limit_bytes=VMEM_LIMIT),
        name="rwkv_rec",
    )(*args)


def _rwkv_prepare_weights(mu, w_rkv, w0, w_la, w_lb, a0, a_la, a_lb, g_la, g_lb):
    d = mu.shape[1]
    rank_w = w_la.shape[2]
    rank_a = a_la.shape[2]
    rank_g = g_la.shape[1]
    rank_g_pad = -(-rank_g // LANES) * LANES

    def block_diag(w):
        rank = w.shape[1]
        out = jnp.zeros((2, rank, 2, d), w.dtype)
        out = out.at[0, :, 0, :].set(w[0]).at[1, :, 1, :].set(w[1])
        return out.reshape(2 * rank, 2 * d)

    lora = (
        w_la.reshape(d, 2 * rank_w).astype(BF16), block_diag(w_lb).astype(BF16), w0.reshape(1, 2 * d),
        a_la.reshape(d, 2 * rank_a).astype(BF16), block_diag(a_lb).astype(BF16), a0.reshape(1, 2 * d),
        jnp.pad(g_la, ((0, 0), (0, rank_g_pad - rank_g))).astype(BF16),
        jnp.pad(g_lb, ((0, rank_g_pad - rank_g), (0, 0))).astype(BF16),
    )
    mu3 = jnp.stack([mu[0], mu[2], mu[3]]).reshape(3, 1, d)
    return {"mu3": mu3, "w_rkv": w_rkv.astype(BF16), "lora": lora}


def _pair_states(s):
    n, _, h, hn, _ = s.shape
    s = s.reshape(n, 2, h // 2, 2, hn, hn)
    out = jnp.zeros((n, 2, h // 2, 2, hn, 2, hn), s.dtype)
    out = out.at[:, :, :, 0, :, 0, :].set(s[:, :, :, 0]).at[:, :, :, 1, :, 1, :].set(s[:, :, :, 1])
    return out.reshape(n, 2, h // 2, 2 * hn, 2 * hn)


def _unpair_states(sp):
    n, _, hp, hn2, _ = sp.shape
    hn = hn2 // 2
    sp = sp.reshape(n, 2, hp, 2, hn, 2, hn)
    return jnp.stack([sp[:, :, :, 0, :, 0, :], sp[:, :, :, 1, :, 1, :]], axis=3).reshape(n, 2, 2 * hp, hn, hn)


def _embed_kernel(xp_ref, xs_ref, pos_ref, o_ref, *, n_prompt_tiles):
    i = pl.program_id(0)

    @pl.when(i < n_prompt_tiles)
    def _():
        o_ref[...] = xp_ref[...]

    @pl.when(i >= n_prompt_tiles)
    def _():
        o_ref[...] = xs_ref[...] + pos_ref[...]


def _grid_pos_embed(n_tokens, d):
    rows = n_tokens // GRID_W
    quarter = d // 4
    omega = 1.0 / (POS_BASE ** (jnp.arange(quarter, dtype=F32) / quarter))
    r = jnp.arange(rows, dtype=F32)[:, None] * omega
    cc = jnp.arange(GRID_W, dtype=F32)[:, None] * omega
    row_emb = jnp.concatenate([jnp.sin(r), jnp.cos(r)], -1)
    col_emb = jnp.concatenate([jnp.sin(cc), jnp.cos(cc)], -1)
    emb = jnp.concatenate([
        jnp.broadcast_to(row_emb[:, None, :], (rows, GRID_W, d // 2)),
        jnp.broadcast_to(col_emb[None, :, :], (rows, GRID_W, d // 2))], -1)
    return emb.reshape(rows * GRID_W, d)


def _embed(xp, xs, pos, sample_len, tm=512):
    n_p, d = xp.shape
    n_s = xs.shape[0]
    npt = n_p // tm
    pos_tiles = sample_len // tm
    return pl.pallas_call(
        functools.partial(_embed_kernel, n_prompt_tiles=npt),
        grid=((n_p + n_s) // tm,),
        in_specs=[
            pl.BlockSpec((tm, d), lambda i: (jnp.minimum(i, npt - 1), 0)),
            pl.BlockSpec((tm, d), lambda i: (jnp.maximum(i - npt, 0), 0)),
            pl.BlockSpec((tm, d), lambda i: (lax.rem(jnp.maximum(i - npt, 0), pos_tiles), 0)),
        ],
        out_specs=pl.BlockSpec((tm, d), lambda i: (i, 0)),
        out_shape=jax.ShapeDtypeStruct((n_p + n_s, d), F32),
        compiler_params=pltpu.CompilerParams(
            dimension_semantics=("parallel",), vmem_limit_bytes=VMEM_LIMIT),
        name="embed",
    )(xp, xs, pos)


def kernel(x_prompt, x_sample, state_hgrn, state_rwkv, c, c_ctx, ada_w, ada_b, ln_g, ln_b, ffn_w_up, ffn_w_down, hgrn_w_in, hgrn_lb, hgrn_norm_g, hgrn_w_o, rwkv_mu, rwkv_w_rkv, rwkv_w0, rwkv_w_la, rwkv_w_lb, rwkv_a0, rwkv_a_la, rwkv_a_lb, rwkv_g_la, rwkv_g_lb, rwkv_k_k, rwkv_k_a, rwkv_r_k, rwkv_gn_g, rwkv_gn_b, rwkv_w_o):
    n_b, l_p, d = x_prompt.shape
    n_s, l_s, _ = x_sample.shape
    depth = ada_w.shape[0]
    n_p_rows = n_b * l_p
    total = n_p_rows + n_s * l_s
    assert n_p_rows % l_s == 0 and l_p % LANES == 0 and l_s % LANES == 0
    alpha = (2 * depth) ** 0.25
    a_heads = d // LANES

    def seg(i, tm):
        return _seg_index(i, tm, n_p_rows, l_s)

    x = _embed(x_prompt.reshape(n_p_rows, d), x_sample.reshape(n_s * l_s, d), _grid_pos_embed(l_s, d), l_s)

    cond8 = jnp.zeros((8, d), F32).at[0].set(c_ctx).at[1:1 + n_s].set(c)
    mods = _adaln(cond8, ada_w, ada_b).reshape(depth, 8, 6, d)

    lb_soft = jax.nn.softmax(hgrn_lb.astype(F32), axis=0)
    lower_bounds = jnp.cumsum(lb_soft, axis=0) - lb_soft[0]

    new_hgrn = []
    new_rwkv = []
    for l in range(depth):
        j = l // 2
        if l % 2 == 0:
            proj = _modmm(x, mods[l], hgrn_w_in[j].astype(BF16), seg)
            o, s_ctx = _hgrn_rec(proj, lower_bounds[j], hgrn_norm_g[j], None, None, n_seq=n_b, seq_len=l_p,
                                 row_block0=0, n_heads=a_heads, total_rows=total)
            (o,) = _hgrn_rec(proj, lower_bounds[j], hgrn_norm_g[j], state_hgrn[:, j], o, n_seq=n_s, seq_len=l_s,
                             row_block0=n_p_rows // l_s, n_heads=a_heads, total_rows=total)
            new_hgrn.append(s_ctx)
            w_o = hgrn_w_o[j]
        else:
            prep = _rwkv_prepare_weights(rwkv_mu[j], rwkv_w_rkv[j], rwkv_w0[j], rwkv_w_la[j], rwkv_w_lb[j],
                                         rwkv_a0[j], rwkv_a_la[j], rwkv_a_lb[j], rwkv_g_la[j], rwkv_g_lb[j])
            seq_kw = dict(n_prompt_rows=n_p_rows, prompt_len=l_p, sample_len=l_s)
            rkv = _rwkv_rkv(x, mods[l], prep["mu3"], prep["w_rkv"], seg, **seq_kw)
            lw, a, g = _rwkv_lora(x, mods[l], rwkv_mu[j], prep["lora"], seg, **seq_kw)
            params = (rwkv_k_k[j], rwkv_k_a[j], rwkv_r_k[j], rwkv_gn_g[j], rwkv_gn_b[j])
            o, s_ctx = _rwkv_rec(rkv, lw, a, g, params, None, None, n_seq=n_b, seq_len=l_p,
                                 row_block0=0, total_rows=total)
            (o,) = _rwkv_rec(rkv, lw, a, g, params, _pair_states(state_rwkv[:, j]), o, n_seq=n_s, seq_len=l_s,
                             row_block0=n_p_rows // l_s, total_rows=total)
            new_rwkv.append(_unpair_states(s_ctx))
            w_o = rwkv_w_o[j]
        x = _out_ln(o, x, mods[l], w_o.astype(BF16), ln_g[l, 0:1], ln_b[l, 0:1], seg, alpha)
        x = _mlp_ln(x, mods[l], ffn_w_up[l].astype(BF16), ffn_w_down[l].astype(BF16),
                    ln_g[l, 1:2], ln_b[l, 1:2], seg, alpha)

    y_prompt = x[:n_p_rows].reshape(n_b, l_p, d)
    y_sample = x[n_p_rows:].reshape(n_s, l_s, d)
    return (y_prompt, y_sample, jnp.stack(new_hgrn, axis=1), jnp.stack(new_rwkv, axis=1))
```

```python
import functools

import jax
import jax.numpy as jnp
from jax import lax
from jax.experimental import pallas as pl
from jax.experimental.pallas import tpu as pltpu

F32 = jnp.float32
BF16 = jnp.bfloat16
HIGHEST = lax.Precision.HIGHEST

LN_EPS = 1e-5
RMS_EPS = 1e-6
GN_EPS = 64e-5
DECAY_SCALE = 0.606531
EXP_CLIP = 80.0
POS_BASE = 10000.0
GRID_W = 64

LANES = 128
HGRN_CHUNK = 16
HGRN_SAFE_EXPONENT = 40.0
RWKV_HEAD = 64
VMEM_LIMIT = 56 * 1024 * 1024

NT_DIMS = (((1,), (1,)), ((), ()))
TN_DIMS = (((0,), (0,)), ((), ()))


def _dot(a, b, dims=None, precision=None):
    if dims is None:
        return jnp.dot(a, b, preferred_element_type=F32, precision=precision)
    return lax.dot_general(a, b, dims, preferred_element_type=F32, precision=precision)


def _bdot(a, b, dims=None):
    return _dot(a.astype(BF16), b.astype(BF16), dims)


def _hdot(a, b, dims=None):
    return _dot(a, b, dims, precision=HIGHEST)


def _split_bf16(x):
    hi = x.astype(BF16)
    return hi, (x - hi.astype(F32)).astype(BF16)


def _dot_x3(a, b, dims=None):
    ah, al = _split_bf16(a)
    bh, bl = _split_bf16(b)
    if dims == NT_DIMS:
        a_cat = jnp.concatenate([ah, al], axis=1)
        b_half = jnp.concatenate([bh, bl], axis=0)
        b_cat = jnp.concatenate([b_half, b_half], axis=1)
        n = b.shape[0]
    else:
        a_cat = jnp.concatenate([ah, al], axis=0 if dims == TN_DIMS else 1)
        b_half = jnp.concatenate([bh, bl], axis=1)
        b_cat = jnp.concatenate([b_half, b_half], axis=0)
        n = b.shape[1]
    r = _dot(a_cat, b_cat, dims)
    return r[:, :n] + r[:, n:]


def _dot_x2l(a, b):
    ah, al = _split_bf16(a)
    b = b.astype(BF16)
    return _dot(jnp.concatenate([ah, al], axis=1), jnp.concatenate([b, b], axis=0))


def _dot_x2r(a, b):
    bh, bl = _split_bf16(b)
    n = b.shape[1]
    r = _dot(a.astype(BF16), jnp.concatenate([bh, bl], axis=1))
    return r[:, :n] + r[:, n:]


def _sigmoid(x):
    return jax.nn.sigmoid(x)


def _layer_norm(x, g, b):
    mu = jnp.mean(x, -1, keepdims=True)
    xc = x - mu
    var = jnp.mean(xc * xc, -1, keepdims=True)
    return xc * lax.rsqrt(var + LN_EPS) * g + b


def _seg_index(i, tm, n_prompt_rows, sample_len):
    start = i * tm
    return jnp.where(start < n_prompt_rows, 0, 1 + (start - n_prompt_rows) // sample_len)


def _adaln_kernel(c_ref, w_ref, b_ref, o_ref):
    c = c_ref[...]
    s = c * _sigmoid(c)
    o_ref[...] = _bdot(s, w_ref[...]) + b_ref[...]


def _adaln(cond8, ada_w, ada_b, tn=1536):
    depth, d, n = ada_w.shape
    return pl.pallas_call(
        _adaln_kernel,
        grid=(depth, n // tn),
        in_specs=[
            pl.BlockSpec((8, d), lambda l, j: (0, 0)),
            pl.BlockSpec((None, d, tn), lambda l, j: (l, 0, j)),
            pl.BlockSpec((None, 1, tn), lambda l, j: (l, 0, j)),
        ],
        out_specs=pl.BlockSpec((None, 8, tn), lambda l, j: (l, 0, j)),
        out_shape=jax.ShapeDtypeStruct((depth, 8, n), F32),
        compiler_params=pltpu.CompilerParams(
            dimension_semantics=("parallel", "parallel"), vmem_limit_bytes=VMEM_LIMIT),
        name="adaln",
    )(cond8, ada_w, ada_b.reshape(depth, 1, n))


def _modmm_kernel(x_ref, mod_ref, w_ref, o_ref, xb_ref):
    j = pl.program_id(1)

    @pl.when(j == 0)
    def _():
        m = mod_ref[...]
        xb_ref[...] = (x_ref[...] * (1.0 + m[1:2]) + m[0:1]).astype(BF16)

    o_ref[...] = _dot(xb_ref[...], w_ref[j])


def _modmm(x, mods, w, seg, tm=512, tn=1024):
    t, d = x.shape
    n = w.shape[1]
    panels = w.reshape(d, n // tn, tn).transpose(1, 0, 2)
    return pl.pallas_call(
        _modmm_kernel,
        grid=(t // tm, n // tn),
        in_specs=[
            pl.BlockSpec((tm, d), lambda i, j: (i, 0)),
            pl.BlockSpec((None, 6, d), lambda i, j: (seg(i, tm), 0, 0)),
            pl.BlockSpec((n // tn, d, tn), lambda i, j: (0, 0, 0)),
        ],
        out_specs=pl.BlockSpec((tm, tn), lambda i, j: (i, j)),
        out_shape=jax.ShapeDtypeStruct((t, n), F32),
        scratch_shapes=[pltpu.VMEM((tm, d), BF16)],
        compiler_params=pltpu.CompilerParams(
            dimension_semantics=("parallel", "arbitrary"), vmem_limit_bytes=VMEM_LIMIT),
        name="modmm",
    )(x, mods, panels)


def _post_mixer_kernel(o_ref, x_ref, mod_ref, wo_ref, wu_ref, wd_ref, g_ref, b_ref,
                       y_ref, x1_ref, xb_ref, acc_ref, *, alpha):
    f = pl.program_id(1)

    @pl.when(f == 0)
    def _():
        m = mod_ref[...]
        y = _dot(o_ref[...], wo_ref[...])
        x1 = _layer_norm(alpha * x_ref[...] + m[2:3] * y, g_ref[0:1], b_ref[0:1])
        x1_ref[...] = x1
        xb_ref[...] = (x1 * (1.0 + m[4:5]) + m[3:4]).astype(BF16)
        acc_ref[...] = jnp.zeros_like(acc_ref)

    h = jnp.maximum(_dot(xb_ref[...], wu_ref[...]), 0.0)
    acc_ref[...] += _dot((h * h).astype(BF16), wd_ref[...])

    @pl.when(f == pl.num_programs(1) - 1)
    def _():
        gate = mod_ref[...][5:6]
        y_ref[...] = _layer_norm(alpha * x1_ref[...] + gate * acc_ref[...], g_ref[1:2], b_ref[1:2])


def _post_mixer(o, x, mods, w_o, w_up, w_down, ln_g, ln_b, seg, alpha, tm=512, tf=1024):
    t, d = x.shape
    dff = w_up.shape[1]
    both = pl.BlockSpec((2, d), lambda i, f: (0, 0))
    return pl.pallas_call(
        functools.partial(_post_mixer_kernel, alpha=alpha),
        grid=(t // tm, dff // tf),
        in_specs=[
            pl.BlockSpec((tm, d), lambda i, f: (i, 0)),
            pl.BlockSpec((tm, d), lambda i, f: (i, 0)),
            pl.BlockSpec((None, 6, d), lambda i, f: (seg(i, tm), 0, 0)),
            pl.BlockSpec((d, d), lambda i, f: (0, 0)),
            pl.BlockSpec((d, tf), lambda i, f: (0, f)),
            pl.BlockSpec((tf, d), lambda i, f: (f, 0)),
            both, both,
        ],
        out_specs=pl.BlockSpec((tm, d), lambda i, f: (i, 0)),
        out_shape=jax.ShapeDtypeStruct((t, d), F32),
        scratch_shapes=[pltpu.VMEM((tm, d), F32), pltpu.VMEM((tm, d), BF16), pltpu.VMEM((tm, d), F32)],
        compiler_params=pltpu.CompilerParams(
            dimension_semantics=("parallel", "arbitrary"), vmem_limit_bytes=VMEM_LIMIT),
        name="post_mixer",
    )(o, x, mods, w_o, w_up, w_down, ln_g, ln_b)


def _hgrn_precompute(entries, r_i, c_i):
    pos = r_i & (HGRN_CHUNK - 1)
    same_chunk = (r_i >> 4) == (c_i >> 4)
    blk = same_chunk.astype(F32)
    pre = []
    for qr, zz, v, lbz, fwd in entries:
        causal = same_chunk & ((c_i <= r_i) if fwd else (c_i >= r_i))
        q = qr * _sigmoid(qr)
        e_abs = jnp.exp(-jnp.abs(zz))
        inv_1p = 1.0 / (1.0 + e_abs)
        log_f = jnp.minimum(zz, 0.0) + jnp.log((1.0 + lbz * jnp.exp(jnp.minimum(-zz, EXP_CLIP))) * inv_1p)
        k = (1.0 - lbz) * jnp.where(zz > 0.0, e_abs, 1.0) * inv_1p
        sums = _dot_x2r(jnp.concatenate([causal.astype(F32), blk], axis=0), log_f)
        cum = sums[:LANES]
        ctot = sums[LANES:]
        pre.append(dict(q=q, k=k, v=v, cum=cum, ctot=ctot, fwd=fwd, causal=causal,
                        q_in=q * jnp.exp(cum), k_out=k * jnp.exp(ctot - cum), dec=jnp.exp(ctot)))

    def scores_factored():
        out = []
        for e in pre:
            half = 0.5 * e["ctot"]
            qk = _dot_x3(e["q"] * jnp.exp(e["cum"] - half), e["k"] * jnp.exp(half - e["cum"]), NT_DIMS)
            out.append(jnp.where(e["causal"], qk, 0.0))
        return out

    def scores_pairwise():
        out = []
        for e in pre:
            fwd, q, k, cum = e["fwd"], e["q"], e["k"], e["cum"]
            scores = jnp.zeros((LANES, LANES), F32)
            for d in range(HGRN_CHUNK):
                if d == 0:
                    kr, cr = k, cum
                else:
                    sh = d if fwd else LANES - d
                    kr = pltpu.roll(k, sh, 0)
                    cr = pltpu.roll(cum, sh, 0)
                valid = (pos >= d) if fwd else (pos <= HGRN_CHUNK - 1 - d)
                ex = jnp.exp(jnp.where(valid, cum - cr, 0.0))
                s = jnp.sum(q * kr * ex, axis=-1, keepdims=True)
                tgt = (c_i == r_i - d) if fwd else (c_i == r_i + d)
                scores = scores + jnp.where(tgt & valid, s, 0.0)
            out.append(scores)
        return out

    lowest = pre[0]["ctot"]
    for e in pre[1:]:
        lowest = jnp.minimum(lowest, e["ctot"])
    scores = lax.cond(jnp.min(lowest) >= -2.0 * HGRN_SAFE_EXPONENT, scores_factored, scores_pairwise)

    chunk_of_lane = c_i >> 4
    out = []
    for e, sc in zip(pre, scores):
        o_intra = _bdot(sc, e["v"])
        v_t = e["v"].T
        lhs = jnp.concatenate(
            [jnp.where(chunk_of_lane == c, v_t, 0.0) for c in range(LANES // HGRN_CHUNK)], axis=0)
        upd = _bdot(lhs, e["k_out"])
        out.append((e["q_in"], o_intra, upd, e["dec"]))
    return out


def _hgrn_rec_kernel(*refs, seq_len, zero_init, want_state, aliased):
    qf_ref, zf_ref, vf_ref, qb_ref, zb_ref, vb_ref, gate_ref, lb_ref, ng_ref = refs[:9]
    pos = 9
    s0_ref = None
    if not zero_init:
        s0_ref = refs[pos]
        pos += 1
    if aliased:
        pos += 1
    o_ref = refs[pos]
    pos += 1
    sfin_ref = None
    if want_state:
        sfin_ref = refs[pos]
        pos += 1
    osum_ref, st_ref = refs[pos:pos + 2]

    n_tiles = seq_len // LANES
    n_chunks = LANES // HGRN_CHUNK
    unroll = 2
    assert n_tiles % unroll == 0 and (n_tiles == unroll or (n_tiles // 2) % unroll == 0)
    r_i = lax.broadcasted_iota(jnp.int32, (LANES, LANES), 0)
    c_i = lax.broadcasted_iota(jnp.int32, (LANES, LANES), 1)
    lb = lb_ref[...]
    data = ((qf_ref, zf_ref, vf_ref), (qb_ref, zb_ref, vb_ref))

    for z in range(2):
        if zero_init:
            st_ref[z] = jnp.zeros((LANES, LANES), F32)
        else:
            st_ref[z] = s0_ref[z].T

    def finish(rows, o):
        o = o * lax.rsqrt(jnp.mean(o * o, -1, keepdims=True) + RMS_EPS)
        g = gate_ref[rows, :]
        o_ref[rows, :] = (o * ng_ref[...] * (g * _sigmoid(g))).astype(o_ref.dtype)

    def block(i, visit):
        slots = []
        for z in range(2):
            for u in range(unroll):
                t = i * unroll + u
                slots.append((z, t if z == 0 else n_tiles - 1 - t))
        entries = []
        for z, t in slots:
            rows = pl.ds(pl.multiple_of(t * LANES, LANES), LANES)
            q_ref, z_ref, v_ref = data[z]
            entries.append((q_ref[rows, :], z_ref[rows, :], v_ref[rows, :], lb[z:z + 1], z == 0))
        pre = _hgrn_precompute(entries, r_i, c_i)

        st = [st_ref[0], st_ref[1]]
        inter = [[None] * n_chunks for _ in slots]
        for u in range(unroll):
            for step in range(n_chunks):
                for z in range(2):
                    idx = z * unroll + u
                    q_in, _, upd, dec = pre[idx]
                    c = step if z == 0 else n_chunks - 1 - step
                    lo = c * HGRN_CHUNK
                    inter[idx][c] = _bdot(q_in[lo:lo + HGRN_CHUNK], st[z], NT_DIMS)
                    st[z] = st[z] * dec[lo:lo + 1] + upd[c * LANES:(c + 1) * LANES]
        st_ref[0] = st[0]
        st_ref[1] = st[1]
        outs = [p[1] + jnp.concatenate(o, axis=0) for p, o in zip(pre, inter)]

        if visit == "both":
            for u in range(unroll):
                finish(pl.ds(u * LANES, LANES), outs[u] + outs[unroll + (n_tiles - 1 - u)])
            return
        for (z, t), out in zip(slots, outs):
            rows = pl.ds(pl.multiple_of(t * LANES, LANES), LANES)
            if visit == "first":
                osum_ref[rows, :] = out
            else:
                finish(rows, osum_ref[rows, :] + out)

    if n_tiles == unroll:
        block(0, "both")
    else:
        def body(i, carry, visit):
            block(i, visit)
            return carry

        half = n_tiles // 2 // unroll
        lax.fori_loop(0, half, functools.partial(body, visit="first"), 0)
        lax.fori_loop(half, 2 * half, functools.partial(body, visit="second"), 0)
    if want_state:
        for z in range(2):
            sfin_ref[z] = st_ref[z].T


def _hgrn_rec(proj, lb, norm_g, s0, prev_out, *, n_seq, seq_len, row_block0, n_heads, total_rows):
    d = n_heads * LANES
    zero_init = s0 is None
    want_state = s0 is None
    aliased = prev_out is not None

    def col(block):
        return lambda b, h: (row_block0 + b, block * n_heads + h)

    blk = (seq_len, LANES)
    in_specs = [pl.BlockSpec(blk, col(0)), pl.BlockSpec(blk, col(1)), pl.BlockSpec(blk, col(2)),
                pl.BlockSpec(blk, col(3)), pl.BlockSpec(blk, col(4)), pl.BlockSpec(blk, col(5)),
                pl.BlockSpec(blk, col(6)),
                pl.BlockSpec((2, LANES), lambda b, h: (0, h)),
                pl.BlockSpec((1, LANES), lambda b, h: (0, h))]
    args = [proj] * 7 + [lb, norm_g.reshape(1, d)]
    if not zero_init:
        in_specs.append(pl.BlockSpec((None, 2, None, LANES, LANES), lambda b, h: (b, 0, h, 0, 0)))
        args.append(s0)
    io_alias = {}
    if aliased:
        in_specs.append(pl.BlockSpec(memory_space=pl.ANY))
        io_alias = {len(args): 0}
        args.append(prev_out)
    out_specs = [pl.BlockSpec(blk, lambda b, h: (row_block0 + b, h))]
    out_shape = [jax.ShapeDtypeStruct((total_rows, d), BF16)]
    if want_state:
        out_specs.append(pl.BlockSpec((None, 2, None, LANES, LANES), lambda b, h: (b, 0, h, 0, 0)))
        out_shape.append(jax.ShapeDtypeStruct((n_seq, 2, n_heads, LANES, LANES), F32))
    res = pl.pallas_call(
        functools.partial(_hgrn_rec_kernel, seq_len=seq_len, zero_init=zero_init,
                          want_state=want_state, aliased=aliased),
        grid=(n_seq, n_heads),
        in_specs=in_specs,
        out_specs=out_specs,
        out_shape=out_shape,
        scratch_shapes=[pltpu.VMEM((seq_len, LANES), F32), pltpu.VMEM((2, LANES, LANES), F32)],
        input_output_aliases=io_alias,
        compiler_params=pltpu.CompilerParams(
            dimension_semantics=("parallel", "parallel"), vmem_limit_bytes=VMEM_LIMIT),
        name="hgrn_rec",
    )(*args)
    return res


def _int_mod(x, n):
    return x & (n - 1) if n & (n - 1) == 0 else lax.rem(x, n)


def _token_shift(x_ref, xp_ref, xn_ref, mod_ref, tile_start, n_prompt_rows, prompt_len, sample_len):
    m = mod_ref[...]
    sh, sc = m[0:1], 1.0 + m[1:2]
    h = x_ref[...] * sc + sh
    h_before = xp_ref[7:8, :] * sc + sh
    h_after = xn_ref[0:1, :] * sc + sh
    tm = h.shape[0]
    rr = lax.broadcasted_iota(jnp.int32, (tm, 1), 0)
    grow = tile_start + rr
    in_prompt = grow < n_prompt_rows
    pos = jnp.where(in_prompt, _int_mod(grow, prompt_len), _int_mod(grow - n_prompt_rows, sample_len))
    last = jnp.where(in_prompt, prompt_len - 1, sample_len - 1)
    prev = jnp.where(rr == 0, h_before, pltpu.roll(h, 1, 0))
    prev = jnp.where(pos == 0, 0.0, prev)
    nxt = jnp.where(rr == tm - 1, h_after, pltpu.roll(h, tm - 1, 0))
    nxt = jnp.where(pos == last, 0.0, nxt)
    return h, 0.5 * (prev + nxt) - h


def _rwkv_rkv_kernel(x_ref, xp_ref, xn_ref, mod_ref, mu_ref, w_ref, o_ref, h_ref, xx_ref, *, tm, seq_info):
    @pl.when(pl.program_id(1) == 0)
    def _():
        h, xx = _token_shift(x_ref, xp_ref, xn_ref, mod_ref, pl.program_id(0) * tm, *seq_info)
        h_ref[...] = h
        xx_ref[...] = xx

    xs = h_ref[...] + xx_ref[...] * mu_ref[...]
    o_ref[...] = _dot(xs.astype(BF16), w_ref[pl.program_id(1)])


def _halo_specs(tm, d, t, n_grid_axes):
    nb = t // 8

    def before(i, *_):
        return (jnp.maximum(i * (tm // 8) - 1, 0), 0)

    def after(i, *_):
        return (jnp.minimum((i + 1) * (tm // 8), nb - 1), 0)

    del n_grid_axes
    return pl.BlockSpec((8, d), before), pl.BlockSpec((8, d), after)


def _rwkv_rkv(x, mods, mu3, w_rkv, seg, *, n_prompt_rows, prompt_len, sample_len, tm=256):
    t, d = x.shape
    n = w_rkv.shape[2]
    before, after = _halo_specs(tm, d, t, 2)
    return pl.pallas_call(
        functools.partial(_rwkv_rkv_kernel, tm=tm, seq_info=(n_prompt_rows, prompt_len, sample_len)),
        grid=(t // tm, 3),
        in_specs=[
            pl.BlockSpec((tm, d), lambda i, j: (i, 0)),
            before, after,
            pl.BlockSpec((None, 6, d), lambda i, j: (seg(i, tm), 0, 0)),
            pl.BlockSpec((None, 1, d), lambda i, j: (j, 0, 0)),
            pl.BlockSpec((3, d, n), lambda i, j: (0, 0, 0)),
        ],
        out_specs=pl.BlockSpec((None, tm, n), lambda i, j: (j, i, 0)),
        out_shape=jax.ShapeDtypeStruct((3, t, n), F32),
        scratch_shapes=[pltpu.VMEM((tm, d), F32), pltpu.VMEM((tm, d), F32)],
        compiler_params=pltpu.CompilerParams(
            dimension_semantics=("parallel", "arbitrary"), vmem_limit_bytes=VMEM_LIMIT),
        name="rwkv_rkv",
    )(x, x, x, mods, mu3, w_rkv)


def _rwkv_lora_kernel(x_ref, xp_ref, xn_ref, mod_ref, mu_ref, wla_ref, wlb_ref, w0_ref, ala_ref, alb_ref,
                      a0_ref, gla_ref, glb_ref, lw_ref, a_ref, g_ref, *, tm, seq_info):
    h, xx = _token_shift(x_ref, xp_ref, xn_ref, mod_ref, pl.program_id(0) * tm, *seq_info)
    mu = mu_ref[...]
    xs_w = (h + xx * mu[1:2]).astype(BF16)
    xs_a = (h + xx * mu[4:5]).astype(BF16)
    xs_g = (h + xx * mu[5:6]).astype(BF16)
    zw = w0_ref[...] + _bdot(jnp.tanh(_dot(xs_w, wla_ref[...])), wlb_ref[...])
    lw_ref[...] = -DECAY_SCALE * _sigmoid(zw)
    za = a0_ref[...] + _bdot(_dot(xs_a, ala_ref[...]), alb_ref[...])
    a_ref[...] = _sigmoid(za)
    g_ref[...] = _bdot(_sigmoid(_dot(xs_g, gla_ref[...])), glb_ref[...])


def _rwkv_lora(x, mods, mu, weights, seg, *, n_prompt_rows, prompt_len, sample_len, tm=256):
    t, d = x.shape
    before, after = _halo_specs(tm, d, t, 1)

    def whole(arr):
        return pl.BlockSpec(arr.shape, lambda i: (0,) * arr.ndim)

    return pl.pallas_call(
        functools.partial(_rwkv_lora_kernel, tm=tm, seq_info=(n_prompt_rows, prompt_len, sample_len)),
        grid=(t // tm,),
        in_specs=[pl.BlockSpec((tm, d), lambda i: (i, 0)), before, after,
                  pl.BlockSpec((None, 6, d), lambda i: (seg(i, tm), 0, 0)), whole(mu)]
                 + [whole(w) for w in weights],
        out_specs=[pl.BlockSpec((tm, 2 * d), lambda i: (i, 0)),
                   pl.BlockSpec((tm, 2 * d), lambda i: (i, 0)),
                   pl.BlockSpec((tm, d), lambda i: (i, 0))],
        out_shape=[jax.ShapeDtypeStruct((t, 2 * d), F32), jax.ShapeDtypeStruct((t, 2 * d), F32),
                   jax.ShapeDtypeStruct((t, d), F32)],
        compiler_params=pltpu.CompilerParams(
            dimension_semantics=("parallel",), vmem_limit_bytes=VMEM_LIMIT),
        name="rwkv_lora",
    )(x, x, x, mods, mu, *weights)


def _rwkv_precompute(entries, consts):
    bd, eye, head_masks, r_i, c_i = consts
    n = len(entries)
    pre = []
    for r, k, v, lw, a, (kkp, kap, rkp), fwd in entries:
        tri = ((c_i <= r_i) if fwd else (c_i >= r_i)).astype(F32)
        kk = k * kkp
        kk = kk / jnp.maximum(jnp.sqrt(_dot_x2l(kk * kk, bd)), 1e-12)
        k2 = k * (1.0 + (a - 1.0) * kap)
        bonus = _dot_x2l(r * k2 * rkp, bd) * v
        b = kk * a
        cl = _dot_x2r(tri, lw)
        cle = cl - lw
        cm = cl[LANES // 2:LANES // 2 + 1]
        ct = cl[LANES - 1:LANES] if fwd else cl[0:1]
        e_inv = jnp.exp(cm - cl)
        e_out = jnp.exp(ct - cl)
        pre.append(dict(
            v=v, bonus=bonus, fwd=fwd,
            kkt=kk * jnp.exp(cle - cm), rt=r * jnp.exp(cl - cm), kh=k2 * e_inv, bh=b * e_inv,
            kkd=kk * jnp.exp(cle), rd=r * jnp.exp(cl), kg=k2 * e_out, bg=b * e_out, gc=jnp.exp(ct)))

    for e in pre:
        lhs = jnp.concatenate([e["kkt"] * head_masks[0], e["kkt"] * head_masks[1],
                               e["rt"] * head_masks[0], e["rt"] * head_masks[1]], axis=0)
        e["gk"] = _dot_x3(lhs, e["kh"], NT_DIMS)
        e["gb"] = _dot_x3(lhs, e["bh"], NT_DIMS)

    chains = []
    for e in pre:
        strict = (c_i < r_i) if e["fwd"] else (c_i > r_i)
        incl = (c_i <= r_i) if e["fwd"] else (c_i >= r_i)
        for hd in range(2):
            lo = hd * LANES
            chains.append(dict(
                e=e, hd=hd,
                a_k=jnp.where(strict, e["gk"][lo:lo + LANES], 0.0),
                a_b=jnp.where(strict, e["gb"][lo:lo + LANES], 0.0),
                b_k=jnp.where(incl, e["gk"][2 * LANES + lo:3 * LANES + lo], 0.0),
                b_b=jnp.where(incl, e["gb"][2 * LANES + lo:3 * LANES + lo], 0.0)))

    for c in chains:
        c["m"] = eye - jnp.where((r_i >> 1) == (c_i >> 1), c["a_b"], 0.0)
    for log_k in range(1, 7):
        off = ((r_i >> (log_k + 1)) == (c_i >> (log_k + 1))) & ((r_i >> log_k) != (c_i >> log_k))
        for c in chains:
            c["t"] = _dot_x3(jnp.where(off, c["a_b"], 0.0), c["m"])
        for c in chains:
            c["m"] = c["m"] - _dot_x3(c["m"], c["t"])

    for c in chains:
        e = c["e"]
        c["v_h"] = e["v"] * head_masks[c["hd"]]
        c["kkd_h"] = e["kkd"] * head_masks[c["hd"]]
        c["akv"] = _dot_x3(c["a_k"], c["v_h"])
    for c in chains:
        c["wt"] = _dot_x3(c["m"], c["kkd_h"])
        c["u0"] = _dot_x3(c["m"], c["akv"])
    for c in chains:
        c["y0"] = _dot_x3(c["b_k"], c["v_h"]) - _dot_x3(c["b_b"], c["u0"])
        c["rp"] = _dot_x3(c["b_b"], c["wt"])

    out = []
    for i, e in enumerate(pre):
        c0, c1 = chains[2 * i], chains[2 * i + 1]
        wt = c0["wt"] + c1["wt"]
        u0 = c0["u0"] + c1["u0"]
        q0 = bd * (_dot_x3(e["v"], e["kg"], TN_DIMS) - _dot_x3(u0.T, e["bg"]))
        p_mat = _dot_x3(wt, e["bg"], TN_DIMS)
        out.append((e["rd"] - c0["rp"] - c1["rp"], c0["y0"] + c1["y0"], e["gc"], q0, p_mat, e["bonus"]))
    assert len(out) == n
    return out


def _rwkv_rec_kernel(*refs, seq_len, zero_init, want_state, aliased):
    data = (refs[0:5], refs[5:10])
    g_ref, kk_ref, ka_ref, rk_ref, gg_ref, gb_ref = refs[10:16]
    pos = 16
    s0_ref = None
    if not zero_init:
        s0_ref = refs[pos]
        pos += 1
    if aliased:
        pos += 1
    o_ref = refs[pos]
    pos += 1
    sfin_ref = None
    if want_state:
        sfin_ref = refs[pos]
        pos += 1
    osum_ref, st_ref = refs[pos:pos + 2]

    n_tiles = seq_len // LANES
    unroll = 2
    assert n_tiles % unroll == 0 and (n_tiles == unroll or (n_tiles // 2) % unroll == 0)
    r_i = lax.broadcasted_iota(jnp.int32, (LANES, LANES), 0)
    c_i = lax.broadcasted_iota(jnp.int32, (LANES, LANES), 1)
    bd = ((r_i >> 6) == (c_i >> 6)).astype(F32)
    eye = (r_i == c_i).astype(F32)
    lane = lax.broadcasted_iota(jnp.int32, (1, LANES), 1)
    head_masks = ((lane < RWKV_HEAD).astype(F32), (lane >= RWKV_HEAD).astype(F32))
    consts = (bd, eye, head_masks, r_i, c_i)
    inv_n = 1.0 / RWKV_HEAD

    for z in range(2):
        st_ref[z] = jnp.zeros((LANES, LANES), F32) if zero_init else s0_ref[z]

    def block(i, visit):
        slots = []
        for z in range(2):
            for u in range(unroll):
                t = i * unroll + u
                slots.append((z, t if z == 0 else n_tiles - 1 - t))
        entries = []
        for z, t in slots:
            rows = pl.ds(pl.multiple_of(t * LANES, LANES), LANES)
            r_ref, k_ref, v_ref, lw_ref, a_ref = data[z]
            params = (kk_ref[z:z + 1, :], ka_ref[z:z + 1, :], rk_ref[z:z + 1, :])
            entries.append((r_ref[rows, :], k_ref[rows, :], v_ref[rows, :], lw_ref[rows, :], a_ref[rows, :],
                            params, z == 0))
        pre = _rwkv_precompute(entries, consts)

        ys = []
        st = [st_ref[0], st_ref[1]]
        for (z, _), (rp, y0, gc, q0, p_mat, _) in zip(slots, pre):
            ys.append(_dot_x3(rp, st[z], NT_DIMS) + y0)
            st[z] = st[z] * gc + q0 - bd * _dot_x3(st[z], p_mat)
        st_ref[0] = st[0]
        st_ref[1] = st[1]

        outs = []
        for (z, _), y, (_, _, _, _, _, bonus) in zip(slots, ys, pre):
            mean = _dot_x2l(y, bd) * inv_n
            yc = y - mean
            var = _dot_x2l(yc * yc, bd) * inv_n
            outs.append(yc * lax.rsqrt(var + GN_EPS) * gg_ref[z:z + 1, :] + gb_ref[z:z + 1, :] + bonus)

        if visit == "both":
            for u in range(unroll):
                rows = pl.ds(u * LANES, LANES)
                total = outs[u] + outs[unroll + (n_tiles - 1 - u)]
                o_ref[rows, :] = (total * g_ref[rows, :]).astype(o_ref.dtype)
            return
        for (z, t), out in zip(slots, outs):
            rows = pl.ds(pl.multiple_of(t * LANES, LANES), LANES)
            if visit == "first":
                osum_ref[rows, :] = out
            else:
                o_ref[rows, :] = ((osum_ref[rows, :] + out) * g_ref[rows, :]).astype(o_ref.dtype)

    if n_tiles == unroll:
        block(0, "both")
    else:
        def body(i, carry, visit):
            block(i, visit)
            return carry

        half = n_tiles // 2 // unroll
        lax.fori_loop(0, half, functools.partial(body, visit="first"), 0)
        lax.fori_loop(half, 2 * half, functools.partial(body, visit="second"), 0)
    if want_state:
        for z in range(2):
            st = st_ref[z]
            sfin_ref[z, 0] = st[:RWKV_HEAD, :RWKV_HEAD]
            sfin_ref[z, 1] = pltpu.roll(st, RWKV_HEAD, 1)[RWKV_HEAD:, :RWKV_HEAD]


def _rwkv_rec(rkv, lw, a, g, params, s0, prev_out, *, n_seq, seq_len, row_block0, total_rows):
    d = g.shape[1]
    n_pairs = d // LANES
    zero_init = s0 is None
    want_state = s0 is None
    aliased = prev_out is not None
    blk = (seq_len, LANES)

    def dir_specs(z):
        col = lambda b, p: (row_block0 + b, z * n_pairs + p)
        return ([pl.BlockSpec((None,) + blk, lambda b, p, j=j: (j, row_block0 + b, z * n_pairs + p))
                 for j in range(3)] + [pl.BlockSpec(blk, col), pl.BlockSpec(blk, col)])

    in_specs = dir_specs(0) + dir_specs(1)
    in_specs.append(pl.BlockSpec(blk, lambda b, p: (row_block0 + b, p)))
    in_specs += [pl.BlockSpec((2, LANES), lambda b, p: (0, p)) for _ in range(5)]
    args = [rkv, rkv, rkv, lw, a] * 2 + [g] + list(params)
    state_spec = pl.BlockSpec((None, 2, None, LANES, LANES), lambda b, p: (b, 0, p, 0, 0))
    if not zero_init:
        in_specs.append(state_spec)
        args.append(s0)
    io_alias = {}
    if aliased:
        in_specs.append(pl.BlockSpec(memory_space=pl.ANY))
        io_alias = {len(args): 0}
        args.append(prev_out)
    out_specs = [pl.BlockSpec(blk, lambda b, p: (row_block0 + b, p))]
    out_shape = [jax.ShapeDtypeStruct((total_rows, d), BF16)]
    if want_state:
        out_specs.append(pl.BlockSpec((None, 2, 2, RWKV_HEAD, RWKV_HEAD), lambda b, p: (b, 0, p, 0, 0)))
        out_shape.append(jax.ShapeDtypeStruct((n_seq, 2, 2 * n_pairs, RWKV_HEAD, RWKV_HEAD), F32))
    return pl.pallas_call(
        functools.partial(_rwkv_rec_kernel, seq_len=seq_len, zero_init=zero_init,
                          want_state=want_state, aliased=aliased),
        grid=(n_seq, n_pairs),
        in_specs=in_specs,
        out_specs=out_specs,
        out_shape=out_shape,
        scratch_shapes=[pltpu.VMEM((seq_len, LANES), F32), pltpu.VMEM((2, LANES, LANES), F32)],
        input_output_aliases=io_alias,
        compiler_params=pltpu.CompilerParams(
            dimension_semantics=("parallel", "parallel"), vmem_limit_bytes=VMEM_LIMIT),
        name="rwkv_rec",
    )(*args)


def _rwkv_prepare_weights(mu, w_rkv, w0, w_la, w_lb, a0, a_la, a_lb, g_la, g_lb):
    d = mu.shape[1]
    rank_w = w_la.shape[2]
    rank_a = a_la.shape[2]
    rank_g = g_la.shape[1]
    rank_g_pad = -(-rank_g // LANES) * LANES

    def block_diag(w):
        rank = w.shape[1]
        out = jnp.zeros((2, rank, 2, d), w.dtype)
        out = out.at[0, :, 0, :].set(w[0]).at[1, :, 1, :].set(w[1])
        return out.reshape(2 * rank, 2 * d)

    lora = (
        w_la.reshape(d, 2 * rank_w).astype(BF16), block_diag(w_lb).astype(BF16), w0.reshape(1, 2 * d),
        a_la.reshape(d, 2 * rank_a).astype(BF16), block_diag(a_lb).astype(BF16), a0.reshape(1, 2 * d),
        jnp.pad(g_la, ((0, 0), (0, rank_g_pad - rank_g))).astype(BF16),
        jnp.pad(g_lb, ((0, rank_g_pad - rank_g), (0, 0))).astype(BF16),
    )
    mu3 = jnp.stack([mu[0], mu[2], mu[3]]).reshape(3, 1, d)
    return {"mu3": mu3, "w_rkv": w_rkv.astype(BF16), "lora": lora}


def _pair_states(s):
    n, _, h, hn, _ = s.shape
    s = s.reshape(n, 2, h // 2, 2, hn, hn)
    out = jnp.zeros((n, 2, h // 2, 2, hn, 2, hn), s.dtype)
    out = out.at[:, :, :, 0, :, 0, :].set(s[:, :, :, 0]).at[:, :, :, 1, :, 1, :].set(s[:, :, :, 1])
    return out.reshape(n, 2, h // 2, 2 * hn, 2 * hn)


def _embed_kernel(xp_ref, xs_ref, row_ref, col_ref, o_ref, *, n_prompt_tiles):
    i = pl.program_id(0)

    @pl.when(i < n_prompt_tiles)
    def _():
        o_ref[...] = xp_ref[...]

    @pl.when(i >= n_prompt_tiles)
    def _():
        half = o_ref.shape[1] // 2
        for grp in range(o_ref.shape[0] // GRID_W):
            rows = pl.ds(grp * GRID_W, GRID_W)
            o_ref[rows, :half] = xs_ref[rows, :half] + row_ref[grp:grp + 1, :]
            o_ref[rows, half:] = xs_ref[rows, half:] + col_ref[...]


def _grid_pos_tables(n_tokens, d):
    quarter = d // 4
    omega = 1.0 / (POS_BASE ** (jnp.arange(quarter, dtype=F32) / quarter))
    r = jnp.arange(n_tokens // GRID_W, dtype=F32)[:, None] * omega
    cc = jnp.arange(GRID_W, dtype=F32)[:, None] * omega
    return (jnp.concatenate([jnp.sin(r), jnp.cos(r)], -1), jnp.concatenate([jnp.sin(cc), jnp.cos(cc)], -1))


def _embed(xp, xs, sample_len, tm=512):
    n_p, d = xp.shape
    n_s = xs.shape[0]
    npt = n_p // tm
    pos_tiles = sample_len // tm
    grid_rows = tm // GRID_W
    row_emb, col_emb = _grid_pos_tables(sample_len, d)
    return pl.pallas_call(
        functools.partial(_embed_kernel, n_prompt_tiles=npt),
        grid=((n_p + n_s) // tm,),
        in_specs=[
            pl.BlockSpec((tm, d), lambda i: (jnp.minimum(i, npt - 1), 0)),
            pl.BlockSpec((tm, d), lambda i: (jnp.maximum(i - npt, 0), 0)),
            pl.BlockSpec((grid_rows, d // 2), lambda i: (lax.rem(jnp.maximum(i - npt, 0), pos_tiles), 0)),
            pl.BlockSpec((GRID_W, d // 2), lambda i: (0, 0)),
        ],
        out_specs=pl.BlockSpec((tm, d), lambda i: (i, 0)),
        out_shape=jax.ShapeDtypeStruct((n_p + n_s, d), F32),
        compiler_params=pltpu.CompilerParams(
            dimension_semantics=("parallel",), vmem_limit_bytes=VMEM_LIMIT),
        name="embed",
    )(xp, xs, row_emb, col_emb)


def kernel(x_prompt, x_sample, state_hgrn, state_rwkv, c, c_ctx, ada_w, ada_b, ln_g, ln_b, ffn_w_up, ffn_w_down, hgrn_w_in, hgrn_lb, hgrn_norm_g, hgrn_w_o, rwkv_mu, rwkv_w_rkv, rwkv_w0, rwkv_w_la, rwkv_w_lb, rwkv_a0, rwkv_a_la, rwkv_a_lb, rwkv_g_la, rwkv_g_lb, rwkv_k_k, rwkv_k_a, rwkv_r_k, rwkv_gn_g, rwkv_gn_b, rwkv_w_o):
    n_b, l_p, d = x_prompt.shape
    n_s, l_s, _ = x_sample.shape
    depth = ada_w.shape[0]
    n_p_rows = n_b * l_p
    total = n_p_rows + n_s * l_s
    assert n_p_rows % l_s == 0 and l_p % LANES == 0 and l_s % LANES == 0
    alpha = (2 * depth) ** 0.25
    a_heads = d // LANES

    def seg(i, tm):
        return _seg_index(i, tm, n_p_rows, l_s)

    x = _embed(x_prompt.reshape(n_p_rows, d), x_sample.reshape(n_s * l_s, d), l_s)

    cond8 = jnp.zeros((8, d), F32).at[0].set(c_ctx).at[1:1 + n_s].set(c)
    mods = _adaln(cond8, ada_w, ada_b).reshape(depth, 8, 6, d)

    lb_soft = jax.nn.softmax(hgrn_lb.astype(F32), axis=0)
    lower_bounds = jnp.cumsum(lb_soft, axis=0) - lb_soft[0]

    new_hgrn = []
    new_rwkv = []
    for l in range(depth):
        j = l // 2
        if l % 2 == 0:
            proj = _modmm(x, mods[l], hgrn_w_in[j].astype(BF16), seg)
            o, s_ctx = _hgrn_rec(proj, lower_bounds[j], hgrn_norm_g[j], None, None, n_seq=n_b, seq_len=l_p,
                                 row_block0=0, n_heads=a_heads, total_rows=total)
            (o,) = _hgrn_rec(proj, lower_bounds[j], hgrn_norm_g[j], state_hgrn[:, j], o, n_seq=n_s, seq_len=l_s,
                             row_block0=n_p_rows // l_s, n_heads=a_heads, total_rows=total)
            new_hgrn.append(s_ctx)
            w_o = hgrn_w_o[j]
        else:
            prep = _rwkv_prepare_weights(rwkv_mu[j], rwkv_w_rkv[j], rwkv_w0[j], rwkv_w_la[j], rwkv_w_lb[j],
                                         rwkv_a0[j], rwkv_a_la[j], rwkv_a_lb[j], rwkv_g_la[j], rwkv_g_lb[j])
            seq_kw = dict(n_prompt_rows=n_p_rows, prompt_len=l_p, sample_len=l_s)
            rkv = _rwkv_rkv(x, mods[l], prep["mu3"], prep["w_rkv"], seg, **seq_kw)
            lw, a, g = _rwkv_lora(x, mods[l], rwkv_mu[j], prep["lora"], seg, **seq_kw)
            params = (rwkv_k_k[j], rwkv_k_a[j], rwkv_r_k[j], rwkv_gn_g[j], rwkv_gn_b[j])
            o, s_ctx = _rwkv_rec(rkv, lw, a, g, params, None, None, n_seq=n_b, seq_len=l_p,
                                 row_block0=0, total_rows=total)
            (o,) = _rwkv_rec(rkv, lw, a, g, params, _pair_states(state_rwkv[:, j]), o, n_seq=n_s, seq_len=l_s,
                             row_block0=n_p_rows // l_s, total_rows=total)
            new_rwkv.append(s_ctx)
            w_o = rwkv_w_o[j]
        x = _post_mixer(o, x, mods[l], w_o.astype(BF16), ffn_w_up[l].astype(BF16), ffn_w_down[l].astype(BF16),
                        ln_g[l], ln_b[l], seg, alpha)

    y_prompt = x[:n_p_rows].reshape(n_b, l_p, d)
    y_sample = x[n_p_rows:].reshape(n_s, l_s, d)
    return (y_prompt, y_sample, jnp.stack(new_hgrn, axis=1), jnp.stack(new_rwkv, axis=1))
```

```python
import functools

import jax
import jax.numpy as jnp
from jax import lax
from jax.experimental import pallas as pl
from jax.experimental.pallas import tpu as pltpu

F32 = jnp.float32
BF16 = jnp.bfloat16
HIGHEST = lax.Precision.HIGHEST

LN_EPS = 1e-5
RMS_EPS = 1e-6
GN_EPS = 64e-5
DECAY_SCALE = 0.606531
EXP_CLIP = 80.0
POS_BASE = 10000.0
GRID_W = 64

LANES = 128
HGRN_CHUNK = 32
HGRN_CHUNK_LOG2 = HGRN_CHUNK.bit_length() - 1
HGRN_SAFE_EXPONENT = 40.0
RWKV_HEAD = 64
VMEM_LIMIT = 56 * 1024 * 1024

NT_DIMS = (((1,), (1,)), ((), ()))
TN_DIMS = (((0,), (0,)), ((), ()))


def _dot(a, b, dims=None, precision=None):
    if dims is None:
        return jnp.dot(a, b, preferred_element_type=F32, precision=precision)
    return lax.dot_general(a, b, dims, preferred_element_type=F32, precision=precision)


def _bdot(a, b, dims=None):
    return _dot(a.astype(BF16), b.astype(BF16), dims)


def _hdot(a, b, dims=None):
    return _dot(a, b, dims, precision=HIGHEST)


def _split_bf16(x):
    hi = x.astype(BF16)
    return hi, (x - hi.astype(F32)).astype(BF16)


def _dot_x3(a, b, dims=None):
    ah, al = _split_bf16(a)
    bh, bl = _split_bf16(b)
    if dims == NT_DIMS:
        a_cat = jnp.concatenate([ah, al], axis=1)
        b_half = jnp.concatenate([bh, bl], axis=0)
        b_cat = jnp.concatenate([b_half, b_half], axis=1)
        n = b.shape[0]
    else:
        a_cat = jnp.concatenate([ah, al], axis=0 if dims == TN_DIMS else 1)
        b_half = jnp.concatenate([bh, bl], axis=1)
        b_cat = jnp.concatenate([b_half, b_half], axis=0)
        n = b.shape[1]
    r = _dot(a_cat, b_cat, dims)
    return r[:, :n] + r[:, n:]


def _dot_x2l(a, b):
    ah, al = _split_bf16(a)
    b = b.astype(BF16)
    return _dot(jnp.concatenate([ah, al], axis=1), jnp.concatenate([b, b], axis=0))


def _dot_x2r(a, b):
    bh, bl = _split_bf16(b)
    n = b.shape[1]
    r = _dot(a.astype(BF16), jnp.concatenate([bh, bl], axis=1))
    return r[:, :n] + r[:, n:]


def _sigmoid(x):
    return jax.nn.sigmoid(x)


def _layer_norm(x, g, b):
    mu = jnp.mean(x, -1, keepdims=True)
    xc = x - mu
    var = jnp.mean(xc * xc, -1, keepdims=True)
    return xc * lax.rsqrt(var + LN_EPS) * g + b


def _seg_index(i, tm, n_prompt_rows, sample_len):
    start = i * tm
    return jnp.where(start < n_prompt_rows, 0, 1 + (start - n_prompt_rows) // sample_len)


def _adaln_kernel(c_ref, w_ref, b_ref, o_ref):
    c = c_ref[...]
    s = c * _sigmoid(c)
    o_ref[...] = _bdot(s, w_ref[...]) + b_ref[...]


def _adaln(cond8, ada_w, ada_b, tn=1536):
    depth, d, n = ada_w.shape
    return pl.pallas_call(
        _adaln_kernel,
        grid=(depth, n // tn),
        in_specs=[
            pl.BlockSpec((8, d), lambda l, j: (0, 0)),
            pl.BlockSpec((None, d, tn), lambda l, j: (l, 0, j)),
            pl.BlockSpec((None, 1, tn), lambda l, j: (l, 0, j)),
        ],
        out_specs=pl.BlockSpec((None, 8, tn), lambda l, j: (l, 0, j)),
        out_shape=jax.ShapeDtypeStruct((depth, 8, n), F32),
        compiler_params=pltpu.CompilerParams(
            dimension_semantics=("parallel", "parallel"), vmem_limit_bytes=VMEM_LIMIT),
        name="adaln",
    )(cond8, ada_w, ada_b.reshape(depth, 1, n))


def _modmm_kernel(x_ref, mod_ref, w_ref, o_ref, xb_ref):
    j = pl.program_id(1)

    @pl.when(j == 0)
    def _():
        m = mod_ref[...]
        xb_ref[...] = (x_ref[...] * (1.0 + m[1:2]) + m[0:1]).astype(BF16)

    o_ref[...] = _dot(xb_ref[...], w_ref[j])


def _modmm(x, mods, w, seg, tm=512, tn=1024):
    t, d = x.shape
    n = w.shape[1]
    panels = w.reshape(d, n // tn, tn).transpose(1, 0, 2)
    return pl.pallas_call(
        _modmm_kernel,
        grid=(t // tm, n // tn),
        in_specs=[
            pl.BlockSpec((tm, d), lambda i, j: (i, 0)),
            pl.BlockSpec((None, 6, d), lambda i, j: (seg(i, tm), 0, 0)),
            pl.BlockSpec((n // tn, d, tn), lambda i, j: (0, 0, 0)),
        ],
        out_specs=pl.BlockSpec((tm, tn), lambda i, j: (i, j)),
        out_shape=jax.ShapeDtypeStruct((t, n), F32),
        scratch_shapes=[pltpu.VMEM((tm, d), BF16)],
        compiler_params=pltpu.CompilerParams(
            dimension_semantics=("parallel", "arbitrary"), vmem_limit_bytes=VMEM_LIMIT),
        name="modmm",
    )(x, mods, panels)


def _post_mixer_kernel(o_ref, x_ref, mod_ref, wo_ref, wu_ref, wd_ref, g_ref, b_ref,
                       y_ref, x1_ref, xb_ref, acc_ref, *, alpha):
    f = pl.program_id(1)

    @pl.when(f == 0)
    def _():
        m = mod_ref[...]
        y = _dot(o_ref[...], wo_ref[...])
        x1 = _layer_norm(alpha * x_ref[...] + m[2:3] * y, g_ref[0:1], b_ref[0:1])
        x1_ref[...] = x1
        xb_ref[...] = (x1 * (1.0 + m[4:5]) + m[3:4]).astype(BF16)
        acc_ref[...] = jnp.zeros_like(acc_ref)

    h = jnp.maximum(_dot(xb_ref[...], wu_ref[...]), 0.0)
    acc_ref[...] += _dot((h * h).astype(BF16), wd_ref[...])

    @pl.when(f == pl.num_programs(1) - 1)
    def _():
        gate = mod_ref[...][5:6]
        y_ref[...] = _layer_norm(alpha * x1_ref[...] + gate * acc_ref[...], g_ref[1:2], b_ref[1:2])


def _post_mixer(o, x, mods, w_o, w_up, w_down, ln_g, ln_b, seg, alpha, tm=512, tf=1024):
    t, d = x.shape
    dff = w_up.shape[1]
    both = pl.BlockSpec((2, d), lambda i, f: (0, 0))
    return pl.pallas_call(
        functools.partial(_post_mixer_kernel, alpha=alpha),
        grid=(t // tm, dff // tf),
        in_specs=[
            pl.BlockSpec((tm, d), lambda i, f: (i, 0)),
            pl.BlockSpec((tm, d), lambda i, f: (i, 0)),
            pl.BlockSpec((None, 6, d), lambda i, f: (seg(i, tm), 0, 0)),
            pl.BlockSpec((d, d), lambda i, f: (0, 0)),
            pl.BlockSpec((d, tf), lambda i, f: (0, f)),
            pl.BlockSpec((tf, d), lambda i, f: (f, 0)),
            both, both,
        ],
        out_specs=pl.BlockSpec((tm, d), lambda i, f: (i, 0)),
        out_shape=jax.ShapeDtypeStruct((t, d), F32),
        scratch_shapes=[pltpu.VMEM((tm, d), F32), pltpu.VMEM((tm, d), BF16), pltpu.VMEM((tm, d), F32)],
        compiler_params=pltpu.CompilerParams(
            dimension_semantics=("parallel", "arbitrary"), vmem_limit_bytes=VMEM_LIMIT),
        name="post_mixer",
    )(o, x, mods, w_o, w_up, w_down, ln_g, ln_b)


def _hgrn_precompute(entries, r_i, c_i):
    pos = r_i & (HGRN_CHUNK - 1)
    same_chunk = (r_i >> HGRN_CHUNK_LOG2) == (c_i >> HGRN_CHUNK_LOG2)
    blk = same_chunk.astype(F32)
    pre = []
    for qr, zz, v, lbz, fwd in entries:
        causal = same_chunk & ((c_i <= r_i) if fwd else (c_i >= r_i))
        q = qr * _sigmoid(qr)
        e_abs = jnp.exp(-jnp.abs(zz))
        inv_1p = 1.0 / (1.0 + e_abs)
        log_f = jnp.minimum(zz, 0.0) + jnp.log((1.0 + lbz * jnp.exp(jnp.minimum(-zz, EXP_CLIP))) * inv_1p)
        k = (1.0 - lbz) * jnp.where(zz > 0.0, e_abs, 1.0) * inv_1p
        sums = _dot_x2r(jnp.concatenate([causal.astype(F32), blk], axis=0), log_f)
        cum = sums[:LANES]
        ctot = sums[LANES:]
        pre.append(dict(q=q, k=k, v=v, cum=cum, ctot=ctot, fwd=fwd, causal=causal,
                        q_in=q * jnp.exp(cum), k_out=k * jnp.exp(ctot - cum), dec=jnp.exp(ctot)))

    def scores_factored():
        out = []
        for e in pre:
            half = 0.5 * e["ctot"]
            qk = _dot_x3(e["q"] * jnp.exp(e["cum"] - half), e["k"] * jnp.exp(half - e["cum"]), NT_DIMS)
            out.append(jnp.where(e["causal"], qk, 0.0))
        return out

    def scores_pairwise():
        out = []
        for e in pre:
            fwd, q, k, cum = e["fwd"], e["q"], e["k"], e["cum"]
            scores = jnp.zeros((LANES, LANES), F32)
            for d in range(HGRN_CHUNK):
                if d == 0:
                    kr, cr = k, cum
                else:
                    sh = d if fwd else LANES - d
                    kr = pltpu.roll(k, sh, 0)
                    cr = pltpu.roll(cum, sh, 0)
                valid = (pos >= d) if fwd else (pos <= HGRN_CHUNK - 1 - d)
                ex = jnp.exp(jnp.where(valid, cum - cr, 0.0))
                s = jnp.sum(q * kr * ex, axis=-1, keepdims=True)
                tgt = (c_i == r_i - d) if fwd else (c_i == r_i + d)
                scores = scores + jnp.where(tgt & valid, s, 0.0)
            out.append(scores)
        return out

    lowest = pre[0]["ctot"]
    for e in pre[1:]:
        lowest = jnp.minimum(lowest, e["ctot"])
    scores = lax.cond(jnp.min(lowest) >= -2.0 * HGRN_SAFE_EXPONENT, scores_factored, scores_pairwise)

    chunk_of_lane = c_i >> HGRN_CHUNK_LOG2
    out = []
    for e, sc in zip(pre, scores):
        o_intra = _bdot(sc, e["v"])
        v_t = e["v"].T
        lhs = jnp.concatenate(
            [jnp.where(chunk_of_lane == c, v_t, 0.0) for c in range(LANES // HGRN_CHUNK)], axis=0)
        upd = _bdot(lhs, e["k_out"])
        out.append((e["q_in"], o_intra, upd, e["dec"]))
    return out


def _hgrn_rec_kernel(*refs, seq_len, zero_init, want_state, aliased):
    qf_ref, zf_ref, vf_ref, qb_ref, zb_ref, vb_ref, gate_ref, lb_ref, ng_ref = refs[:9]
    pos = 9
    s0_ref = None
    if not zero_init:
        s0_ref = refs[pos]
        pos += 1
    if aliased:
        pos += 1
    o_ref = refs[pos]
    pos += 1
    sfin_ref = None
    if want_state:
        sfin_ref = refs[pos]
        pos += 1
    osum_ref, st_ref = refs[pos:pos + 2]

    n_tiles = seq_len // LANES
    n_chunks = LANES // HGRN_CHUNK
    unroll = 2
    assert n_tiles % unroll == 0 and (n_tiles == unroll or (n_tiles // 2) % unroll == 0)
    r_i = lax.broadcasted_iota(jnp.int32, (LANES, LANES), 0)
    c_i = lax.broadcasted_iota(jnp.int32, (LANES, LANES), 1)
    lb = lb_ref[...]
    data = ((qf_ref, zf_ref, vf_ref), (qb_ref, zb_ref, vb_ref))

    for z in range(2):
        if zero_init:
            st_ref[z] = jnp.zeros((LANES, LANES), F32)
        else:
            st_ref[z] = s0_ref[z].T

    def finish(rows, o):
        o = o * lax.rsqrt(jnp.mean(o * o, -1, keepdims=True) + RMS_EPS)
        g = gate_ref[rows, :]
        o_ref[rows, :] = (o * ng_ref[...] * (g * _sigmoid(g))).astype(o_ref.dtype)

    def block(i, visit):
        slots = []
        for z in range(2):
            for u in range(unroll):
                t = i * unroll + u
                slots.append((z, t if z == 0 else n_tiles - 1 - t))
        entries = []
        for z, t in slots:
            rows = pl.ds(pl.multiple_of(t * LANES, LANES), LANES)
            q_ref, z_ref, v_ref = data[z]
            entries.append((q_ref[rows, :], z_ref[rows, :], v_ref[rows, :], lb[z:z + 1], z == 0))
        pre = _hgrn_precompute(entries, r_i, c_i)

        st = [st_ref[0], st_ref[1]]
        inter = [[None] * n_chunks for _ in slots]
        for u in range(unroll):
            for step in range(n_chunks):
                for z in range(2):
                    idx = z * unroll + u
                    q_in, _, upd, dec = pre[idx]
                    c = step if z == 0 else n_chunks - 1 - step
                    lo = c * HGRN_CHUNK
                    inter[idx][c] = _bdot(q_in[lo:lo + HGRN_CHUNK], st[z], NT_DIMS)
                    st[z] = st[z] * dec[lo:lo + 1] + upd[c * LANES:(c + 1) * LANES]
        st_ref[0] = st[0]
        st_ref[1] = st[1]
        outs = [p[1] + jnp.concatenate(o, axis=0) for p, o in zip(pre, inter)]

        if visit == "both":
            for u in range(unroll):
                finish(pl.ds(u * LANES, LANES), outs[u] + outs[unroll + (n_tiles - 1 - u)])
            return
        for (z, t), out in zip(slots, outs):
            rows = pl.ds(pl.multiple_of(t * LANES, LANES), LANES)
            if visit == "first":
                osum_ref[rows, :] = out
            else:
                finish(rows, osum_ref[rows, :] + out)

    if n_tiles == unroll:
        block(0, "both")
    else:
        def body(i, carry, visit):
            block(i, visit)
            return carry

        half = n_tiles // 2 // unroll
        lax.fori_loop(0, half, functools.partial(body, visit="first"), 0)
        lax.fori_loop(half, 2 * half, functools.partial(body, visit="second"), 0)
    if want_state:
        for z in range(2):
            sfin_ref[z] = st_ref[z].T


def _hgrn_rec(proj, lb, norm_g, s0, prev_out, *, n_seq, seq_len, row_block0, n_heads, total_rows):
    d = n_heads * LANES
    zero_init = s0 is None
    want_state = s0 is None
    aliased = prev_out is not None

    def col(block):
        return lambda b, h: (row_block0 + b, block * n_heads + h)

    blk = (seq_len, LANES)
    in_specs = [pl.BlockSpec(blk, col(0)), pl.BlockSpec(blk, col(1)), pl.BlockSpec(blk, col(2)),
                pl.BlockSpec(blk, col(3)), pl.BlockSpec(blk, col(4)), pl.BlockSpec(blk, col(5)),
                pl.BlockSpec(blk, col(6)),
                pl.BlockSpec((2, LANES), lambda b, h: (0, h)),
                pl.BlockSpec((1, LANES), lambda b, h: (0, h))]
    args = [proj] * 7 + [lb, norm_g.reshape(1, d)]
    if not zero_init:
        in_specs.append(pl.BlockSpec((None, 2, None, LANES, LANES), lambda b, h: (b, 0, h, 0, 0)))
        args.append(s0)
    io_alias = {}
    if aliased:
        in_specs.append(pl.BlockSpec(memory_space=pl.ANY))
        io_alias = {len(args): 0}
        args.append(prev_out)
    out_specs = [pl.BlockSpec(blk, lambda b, h: (row_block0 + b, h))]
    out_shape = [jax.ShapeDtypeStruct((total_rows, d), BF16)]
    if want_state:
        out_specs.append(pl.BlockSpec((None, 2, None, LANES, LANES), lambda b, h: (b, 0, h, 0, 0)))
        out_shape.append(jax.ShapeDtypeStruct((n_seq, 2, n_heads, LANES, LANES), F32))
    res = pl.pallas_call(
        functools.partial(_hgrn_rec_kernel, seq_len=seq_len, zero_init=zero_init,
                          want_state=want_state, aliased=aliased),
        grid=(n_seq, n_heads),
        in_specs=in_specs,
        out_specs=out_specs,
        out_shape=out_shape,
        scratch_shapes=[pltpu.VMEM((seq_len, LANES), F32), pltpu.VMEM((2, LANES, LANES), F32)],
        input_output_aliases=io_alias,
        compiler_params=pltpu.CompilerParams(
            dimension_semantics=("parallel", "parallel"), vmem_limit_bytes=VMEM_LIMIT),
        name="hgrn_rec",
    )(*args)
    return res


def _int_mod(x, n):
    return x & (n - 1) if n & (n - 1) == 0 else lax.rem(x, n)


def _token_shift(x_ref, xp_ref, xn_ref, mod_ref, tile_start, n_prompt_rows, prompt_len, sample_len):
    m = mod_ref[...]
    sh, sc = m[0:1], 1.0 + m[1:2]
    h = x_ref[...] * sc + sh
    h_before = xp_ref[7:8, :] * sc + sh
    h_after = xn_ref[0:1, :] * sc + sh
    tm = h.shape[0]
    rr = lax.broadcasted_iota(jnp.int32, (tm, 1), 0)
    grow = tile_start + rr
    in_prompt = grow < n_prompt_rows
    pos = jnp.where(in_prompt, _int_mod(grow, prompt_len), _int_mod(grow - n_prompt_rows, sample_len))
    last = jnp.where(in_prompt, prompt_len - 1, sample_len - 1)
    prev = jnp.where(rr == 0, h_before, pltpu.roll(h, 1, 0))
    prev = jnp.where(pos == 0, 0.0, prev)
    nxt = jnp.where(rr == tm - 1, h_after, pltpu.roll(h, tm - 1, 0))
    nxt = jnp.where(pos == last, 0.0, nxt)
    return h, 0.5 * (prev + nxt) - h


def _rwkv_rkv_kernel(x_ref, xp_ref, xn_ref, mod_ref, mu_ref, w_ref, o_ref, h_ref, xx_ref, *, tm, seq_info):
    @pl.when(pl.program_id(1) == 0)
    def _():
        h, xx = _token_shift(x_ref, xp_ref, xn_ref, mod_ref, pl.program_id(0) * tm, *seq_info)
        h_ref[...] = h
        xx_ref[...] = xx

    xs = h_ref[...] + xx_ref[...] * mu_ref[...]
    o_ref[...] = _dot(xs.astype(BF16), w_ref[pl.program_id(1)])


def _halo_specs(tm, d, t, n_grid_axes):
    nb = t // 8

    def before(i, *_):
        return (jnp.maximum(i * (tm // 8) - 1, 0), 0)

    def after(i, *_):
        return (jnp.minimum((i + 1) * (tm // 8), nb - 1), 0)

    del n_grid_axes
    return pl.BlockSpec((8, d), before), pl.BlockSpec((8, d), after)


def _rwkv_rkv(x, mods, mu3, w_rkv, seg, *, n_prompt_rows, prompt_len, sample_len, tm=256):
    t, d = x.shape
    n = w_rkv.shape[2]
    before, after = _halo_specs(tm, d, t, 2)
    return pl.pallas_call(
        functools.partial(_rwkv_rkv_kernel, tm=tm, seq_info=(n_prompt_rows, prompt_len, sample_len)),
        grid=(t // tm, 3),
        in_specs=[
            pl.BlockSpec((tm, d), lambda i, j: (i, 0)),
            before, after,
            pl.BlockSpec((None, 6, d), lambda i, j: (seg(i, tm), 0, 0)),
            pl.BlockSpec((None, 1, d), lambda i, j: (j, 0, 0)),
            pl.BlockSpec((3, d, n), lambda i, j: (0, 0, 0)),
        ],
        out_specs=pl.BlockSpec((None, tm, n), lambda i, j: (j, i, 0)),
        out_shape=jax.ShapeDtypeStruct((3, t, n), F32),
        scratch_shapes=[pltpu.VMEM((tm, d), F32), pltpu.VMEM((tm, d), F32)],
        compiler_params=pltpu.CompilerParams(
            dimension_semantics=("parallel", "arbitrary"), vmem_limit_bytes=VMEM_LIMIT),
        name="rwkv_rkv",
    )(x, x, x, mods, mu3, w_rkv)


def _rwkv_lora_kernel(x_ref, xp_ref, xn_ref, mod_ref, mu_ref, wla_ref, wlb_ref, w0_ref, ala_ref, alb_ref,
                      a0_ref, gla_ref, glb_ref, lw_ref, a_ref, g_ref, *, tm, seq_info):
    h, xx = _token_shift(x_ref, xp_ref, xn_ref, mod_ref, pl.program_id(0) * tm, *seq_info)
    mu = mu_ref[...]
    xs_w = (h + xx * mu[1:2]).astype(BF16)
    xs_a = (h + xx * mu[4:5]).astype(BF16)
    xs_g = (h + xx * mu[5:6]).astype(BF16)
    zw = w0_ref[...] + _bdot(jnp.tanh(_dot(xs_w, wla_ref[...])), wlb_ref[...])
    lw_ref[...] = -DECAY_SCALE * _sigmoid(zw)
    za = a0_ref[...] + _bdot(_dot(xs_a, ala_ref[...]), alb_ref[...])
    a_ref[...] = _sigmoid(za)
    g_ref[...] = _bdot(_sigmoid(_dot(xs_g, gla_ref[...])), glb_ref[...])


def _rwkv_lora(x, mods, mu, weights, seg, *, n_prompt_rows, prompt_len, sample_len, tm=256):
    t, d = x.shape
    before, after = _halo_specs(tm, d, t, 1)

    def whole(arr):
        return pl.BlockSpec(arr.shape, lambda i: (0,) * arr.ndim)

    return pl.pallas_call(
        functools.partial(_rwkv_lora_kernel, tm=tm, seq_info=(n_prompt_rows, prompt_len, sample_len)),
        grid=(t // tm,),
        in_specs=[pl.BlockSpec((tm, d), lambda i: (i, 0)), before, after,
                  pl.BlockSpec((None, 6, d), lambda i: (seg(i, tm), 0, 0)), whole(mu)]
                 + [whole(w) for w in weights],
        out_specs=[pl.BlockSpec((tm, 2 * d), lambda i: (i, 0)),
                   pl.BlockSpec((tm, 2 * d), lambda i: (i, 0)),
                   pl.BlockSpec((tm, d), lambda i: (i, 0))],
        out_shape=[jax.ShapeDtypeStruct((t, 2 * d), F32), jax.ShapeDtypeStruct((t, 2 * d), F32),
                   jax.ShapeDtypeStruct((t, d), F32)],
        compiler_params=pltpu.CompilerParams(
            dimension_semantics=("parallel",), vmem_limit_bytes=VMEM_LIMIT),
        name="rwkv_lora",
    )(x, x, x, mods, mu, *weights)


def _rwkv_precompute(entries, consts):
    bd, eye, head_masks, r_i, c_i = consts
    n = len(entries)
    pre = []
    for r, k, v, lw, a, (kkp, kap, rkp), fwd in entries:
        tri = ((c_i <= r_i) if fwd else (c_i >= r_i)).astype(F32)
        kk = k * kkp
        kk = kk / jnp.maximum(jnp.sqrt(_dot_x2l(kk * kk, bd)), 1e-12)
        k2 = k * (1.0 + (a - 1.0) * kap)
        bonus = _dot_x2l(r * k2 * rkp, bd) * v
        b = kk * a
        cl = _dot_x2r(tri, lw)
        cle = cl - lw
        cm = cl[LANES // 2:LANES // 2 + 1]
        ct = cl[LANES - 1:LANES] if fwd else cl[0:1]
        e_inv = jnp.exp(cm - cl)
        e_out = jnp.exp(ct - cl)
        pre.append(dict(
            v=v, bonus=bonus, fwd=fwd,
            kkt=kk * jnp.exp(cle - cm), rt=r * jnp.exp(cl - cm), kh=k2 * e_inv, bh=b * e_inv,
            kkd=kk * jnp.exp(cle), rd=r * jnp.exp(cl), kg=k2 * e_out, bg=b * e_out, gc=jnp.exp(ct)))

    for e in pre:
        lhs = jnp.concatenate([e["kkt"] * head_masks[0], e["kkt"] * head_masks[1],
                               e["rt"] * head_masks[0], e["rt"] * head_masks[1]], axis=0)
        e["gk"] = _dot_x3(lhs, e["kh"], NT_DIMS)
        e["gb"] = _dot_x3(lhs, e["bh"], NT_DIMS)

    chains = []
    for e in pre:
        strict = (c_i < r_i) if e["fwd"] else (c_i > r_i)
        incl = (c_i <= r_i) if e["fwd"] else (c_i >= r_i)
        for hd in range(2):
            lo = hd * LANES
            chains.append(dict(
                e=e, hd=hd,
                a_k=jnp.where(strict, e["gk"][lo:lo + LANES], 0.0),
                a_b=jnp.where(strict, e["gb"][lo:lo + LANES], 0.0),
                b_k=jnp.where(incl, e["gk"][2 * LANES + lo:3 * LANES + lo], 0.0),
                b_b=jnp.where(incl, e["gb"][2 * LANES + lo:3 * LANES + lo], 0.0)))

    for c in chains:
        c["m"] = eye - jnp.where((r_i >> 1) == (c_i >> 1), c["a_b"], 0.0)
    for log_k in range(1, 7):
        k = 1 << log_k
        off = ((r_i >> (log_k + 1)) == (c_i >> (log_k + 1))) & ((r_i >> log_k) != (c_i >> log_k))
        if k < 8:
            for c in chains:
                c["t"] = _dot_x3(jnp.where(off, c["a_b"], 0.0), c["m"])
            for c in chains:
                c["m"] = c["m"] - _dot_x3(c["m"], c["t"])
            continue

        def take(x, fwd):
            first = k if fwd else 0
            return jnp.concatenate([x[lo:lo + k] for lo in range(first, LANES, 2 * k)], axis=0)

        def spread(rows, fwd):
            zero = jnp.zeros((k, LANES), F32)
            parts = []
            for j in range(LANES // (2 * k)):
                piece = rows[j * k:(j + 1) * k]
                parts += [zero, piece] if fwd else [piece, zero]
            return jnp.concatenate(parts, axis=0)

        for c in chains:
            fwd = c["e"]["fwd"]
            c["t"] = spread(_dot_x3(take(jnp.where(off, c["a_b"], 0.0), fwd), c["m"]), fwd)
        for c in chains:
            fwd = c["e"]["fwd"]
            c["m"] = c["m"] - spread(_dot_x3(take(c["m"], fwd), c["t"]), fwd)

    for c in chains:
        e = c["e"]
        c["v_h"] = e["v"] * head_masks[c["hd"]]
        c["kkd_h"] = e["kkd"] * head_masks[c["hd"]]
        c["akv"] = _dot_x3(c["a_k"], c["v_h"])
    for c in chains:
        c["wt"] = _dot_x3(c["m"], c["kkd_h"])
        c["u0"] = _dot_x3(c["m"], c["akv"])
    for c in chains:
        c["y0"] = _dot_x3(c["b_k"], c["v_h"]) - _dot_x3(c["b_b"], c["u0"])
        c["rp"] = _dot_x3(c["b_b"], c["wt"])

    out = []
    for i, e in enumerate(pre):
        c0, c1 = chains[2 * i], chains[2 * i + 1]
        wt = c0["wt"] + c1["wt"]
        u0 = c0["u0"] + c1["u0"]
        q0 = bd * (_dot_x3(e["v"], e["kg"], TN_DIMS) - _dot_x3(u0.T, e["bg"]))
        p_mat = _dot_x3(wt, e["bg"], TN_DIMS)
        out.append((e["rd"] - c0["rp"] - c1["rp"], c0["y0"] + c1["y0"], e["gc"], q0, p_mat, e["bonus"]))
    assert len(out) == n
    return out


def _rwkv_rec_kernel(*refs, seq_len, zero_init, want_state, aliased):
    data = (refs[0:5], refs[5:10])
    g_ref, kk_ref, ka_ref, rk_ref, gg_ref, gb_ref = refs[10:16]
    pos = 16
    s0_ref = None
    if not zero_init:
        s0_ref = refs[pos]
        pos += 1
    if aliased:
        pos += 1
    o_ref = refs[pos]
    pos += 1
    sfin_ref = None
    if want_state:
        sfin_ref = refs[pos]
        pos += 1
    osum_ref, st_ref = refs[pos:pos + 2]

    n_tiles = seq_len // LANES
    unroll = 2
    assert n_tiles % unroll == 0 and (n_tiles == unroll or (n_tiles // 2) % unroll == 0)
    r_i = lax.broadcasted_iota(jnp.int32, (LANES, LANES), 0)
    c_i = lax.broadcasted_iota(jnp.int32, (LANES, LANES), 1)
    bd = ((r_i >> 6) == (c_i >> 6)).astype(F32)
    eye = (r_i == c_i).astype(F32)
    lane = lax.broadcasted_iota(jnp.int32, (1, LANES), 1)
    head_masks = ((lane < RWKV_HEAD).astype(F32), (lane >= RWKV_HEAD).astype(F32))
    consts = (bd, eye, head_masks, r_i, c_i)
    inv_n = 1.0 / RWKV_HEAD

    for z in range(2):
        st_ref[z] = jnp.zeros((LANES, LANES), F32) if zero_init else s0_ref[z]

    def block(i, visit):
        slots = []
        for z in range(2):
            for u in range(unroll):
                t = i * unroll + u
                slots.append((z, t if z == 0 else n_tiles - 1 - t))
        entries = []
        for z, t in slots:
            rows = pl.ds(pl.multiple_of(t * LANES, LANES), LANES)
            r_ref, k_ref, v_ref, lw_ref, a_ref = data[z]
            params = (kk_ref[z:z + 1, :], ka_ref[z:z + 1, :], rk_ref[z:z + 1, :])
            entries.append((r_ref[rows, :], k_ref[rows, :], v_ref[rows, :], lw_ref[rows, :], a_ref[rows, :],
                            params, z == 0))
        pre = _rwkv_precompute(entries, consts)

        ys = []
        st = [st_ref[0], st_ref[1]]
        for (z, _), (rp, y0, gc, q0, p_mat, _) in zip(slots, pre):
            ys.append(_dot_x3(rp, st[z], NT_DIMS) + y0)
            st[z] = st[z] * gc + q0 - bd * _dot_x3(st[z], p_mat)
        st_ref[0] = st[0]
        st_ref[1] = st[1]

        outs = []
        for (z, _), y, (_, _, _, _, _, bonus) in zip(slots, ys, pre):
            mean = _dot_x2l(y, bd) * inv_n
            yc = y - mean
            var = _dot_x2l(yc * yc, bd) * inv_n
            outs.append(yc * lax.rsqrt(var + GN_EPS) * gg_ref[z:z + 1, :] + gb_ref[z:z + 1, :] + bonus)

        if visit == "both":
            for u in range(unroll):
                rows = pl.ds(u * LANES, LANES)
                total = outs[u] + outs[unroll + (n_tiles - 1 - u)]
                o_ref[rows, :] = (total * g_ref[rows, :]).astype(o_ref.dtype)
            return
        for (z, t), out in zip(slots, outs):
            rows = pl.ds(pl.multiple_of(t * LANES, LANES), LANES)
            if visit == "first":
                osum_ref[rows, :] = out
            else:
                o_ref[rows, :] = ((osum_ref[rows, :] + out) * g_ref[rows, :]).astype(o_ref.dtype)

    if n_tiles == unroll:
        block(0, "both")
    else:
        def body(i, carry, visit):
            block(i, visit)
            return carry

        half = n_tiles // 2 // unroll
        lax.fori_loop(0, half, functools.partial(body, visit="first"), 0)
        lax.fori_loop(half, 2 * half, functools.partial(body, visit="second"), 0)
    if want_state:
        for z in range(2):
            st = st_ref[z]
            sfin_ref[z, 0] = st[:RWKV_HEAD, :RWKV_HEAD]
            sfin_ref[z, 1] = pltpu.roll(st, RWKV_HEAD, 1)[RWKV_HEAD:, :RWKV_HEAD]


def _rwkv_rec(rkv, lw, a, g, params, s0, prev_out, *, n_seq, seq_len, row_block0, total_rows):
    d = g.shape[1]
    n_pairs = d // LANES
    zero_init = s0 is None
    want_state = s0 is None
    aliased = prev_out is not None
    blk = (seq_len, LANES)

    def dir_specs(z):
        col = lambda b, p: (row_block0 + b, z * n_pairs + p)
        return ([pl.BlockSpec((None,) + blk, lambda b, p, j=j: (j, row_block0 + b, z * n_pairs + p))
                 for j in range(3)] + [pl.BlockSpec(blk, col), pl.BlockSpec(blk, col)])

    in_specs = dir_specs(0) + dir_specs(1)
    in_specs.append(pl.BlockSpec(blk, lambda b, p: (row_block0 + b, p)))
    in_specs += [pl.BlockSpec((2, LANES), lambda b, p: (0, p)) for _ in range(5)]
    args = [rkv, rkv, rkv, lw, a] * 2 + [g] + list(params)
    state_spec = pl.BlockSpec((None, 2, None, LANES, LANES), lambda b, p: (b, 0, p, 0, 0))
    if not zero_init:
        in_specs.append(state_spec)
        args.append(s0)
    io_alias = {}
    if aliased:
        in_specs.append(pl.BlockSpec(memory_space=pl.ANY))
        io_alias = {len(args): 0}
        args.append(prev_out)
    out_specs = [pl.BlockSpec(blk, lambda b, p: (row_block0 + b, p))]
    out_shape = [jax.ShapeDtypeStruct((total_rows, d), BF16)]
    if want_state:
        out_specs.append(pl.BlockSpec((None, 2, 2, RWKV_HEAD, RWKV_HEAD), lambda b, p: (b, 0, p, 0, 0)))
        out_shape.append(jax.ShapeDtypeStruct((n_seq, 2, 2 * n_pairs, RWKV_HEAD, RWKV_HEAD), F32))
    return pl.pallas_call(
        functools.partial(_rwkv_rec_kernel, seq_len=seq_len, zero_init=zero_init,
                          want_state=want_state, aliased=aliased),
        grid=(n_seq, n_pairs),
        in_specs=in_specs,
        out_specs=out_specs,
        out_shape=out_shape,
        scratch_shapes=[pltpu.VMEM((seq_len, LANES), F32), pltpu.VMEM((2, LANES, LANES), F32)],
        input_output_aliases=io_alias,
        compiler_params=pltpu.CompilerParams(
            dimension_semantics=("parallel", "parallel"), vmem_limit_bytes=VMEM_LIMIT),
        name="rwkv_rec",
    )(*args)


def _rwkv_prepare_weights(mu, w_rkv, w0, w_la, w_lb, a0, a_la, a_lb, g_la, g_lb):
    d = mu.shape[1]
    rank_w = w_la.shape[2]
    rank_a = a_la.shape[2]
    rank_g = g_la.shape[1]
    rank_g_pad = -(-rank_g // LANES) * LANES

    def block_diag(w):
        rank = w.shape[1]
        out = jnp.zeros((2, rank, 2, d), w.dtype)
        out = out.at[0, :, 0, :].set(w[0]).at[1, :, 1, :].set(w[1])
        return out.reshape(2 * rank, 2 * d)

    lora = (
        w_la.reshape(d, 2 * rank_w).astype(BF16), block_diag(w_lb).astype(BF16), w0.reshape(1, 2 * d),
        a_la.reshape(d, 2 * rank_a).astype(BF16), block_diag(a_lb).astype(BF16), a0.reshape(1, 2 * d),
        jnp.pad(g_la, ((0, 0), (0, rank_g_pad - rank_g))).astype(BF16),
        jnp.pad(g_lb, ((0, rank_g_pad - rank_g), (0, 0))).astype(BF16),
    )
    mu3 = jnp.stack([mu[0], mu[2], mu[3]]).reshape(3, 1, d)
    return {"mu3": mu3, "w_rkv": w_rkv.astype(BF16), "lora": lora}


def _pair_states(s):
    n, _, h, hn, _ = s.shape
    s = s.reshape(n, 2, h // 2, 2, hn, hn)
    out = jnp.zeros((n, 2, h // 2, 2, hn, 2, hn), s.dtype)
    out = out.at[:, :, :, 0, :, 0, :].set(s[:, :, :, 0]).at[:, :, :, 1, :, 1, :].set(s[:, :, :, 1])
    return out.reshape(n, 2, h // 2, 2 * hn, 2 * hn)


def _embed_kernel(xp_ref, xs_ref, row_ref, col_ref, o_ref, *, n_prompt_tiles):
    i = pl.program_id(0)

    @pl.when(i < n_prompt_tiles)
    def _():
        o_ref[...] = xp_ref[...]

    @pl.when(i >= n_prompt_tiles)
    def _():
        half = o_ref.shape[1] // 2
        for grp in range(o_ref.shape[0] // GRID_W):
            rows = pl.ds(grp * GRID_W, GRID_W)
            o_ref[rows, :half] = xs_ref[rows, :half] + row_ref[grp:grp + 1, :]
            o_ref[rows, half:] = xs_ref[rows, half:] + col_ref[...]


def _grid_pos_tables(n_tokens, d):
    quarter = d // 4
    omega = 1.0 / (POS_BASE ** (jnp.arange(quarter, dtype=F32) / quarter))
    r = jnp.arange(n_tokens // GRID_W, dtype=F32)[:, None] * omega
    cc = jnp.arange(GRID_W, dtype=F32)[:, None] * omega
    return (jnp.concatenate([jnp.sin(r), jnp.cos(r)], -1), jnp.concatenate([jnp.sin(cc), jnp.cos(cc)], -1))


def _embed(xp, xs, sample_len, tm=512):
    n_p, d = xp.shape
    n_s = xs.shape[0]
    npt = n_p // tm
    pos_tiles = sample_len // tm
    grid_rows = tm // GRID_W
    row_emb, col_emb = _grid_pos_tables(sample_len, d)
    return pl.pallas_call(
        functools.partial(_embed_kernel, n_prompt_tiles=npt),
        grid=((n_p + n_s) // tm,),
        in_specs=[
            pl.BlockSpec((tm, d), lambda i: (jnp.minimum(i, npt - 1), 0)),
            pl.BlockSpec((tm, d), lambda i: (jnp.maximum(i - npt, 0), 0)),
            pl.BlockSpec((grid_rows, d // 2), lambda i: (lax.rem(jnp.maximum(i - npt, 0), pos_tiles), 0)),
            pl.BlockSpec((GRID_W, d // 2), lambda i: (0, 0)),
        ],
        out_specs=pl.BlockSpec((tm, d), lambda i: (i, 0)),
        out_shape=jax.ShapeDtypeStruct((n_p + n_s, d), F32),
        compiler_params=pltpu.CompilerParams(
            dimension_semantics=("parallel",), vmem_limit_bytes=VMEM_LIMIT),
        name="embed",
    )(xp, xs, row_emb, col_emb)


def kernel(x_prompt, x_sample, state_hgrn, state_rwkv, c, c_ctx, ada_w, ada_b, ln_g, ln_b, ffn_w_up, ffn_w_down, hgrn_w_in, hgrn_lb, hgrn_norm_g, hgrn_w_o, rwkv_mu, rwkv_w_rkv, rwkv_w0, rwkv_w_la, rwkv_w_lb, rwkv_a0, rwkv_a_la, rwkv_a_lb, rwkv_g_la, rwkv_g_lb, rwkv_k_k, rwkv_k_a, rwkv_r_k, rwkv_gn_g, rwkv_gn_b, rwkv_w_o):
    n_b, l_p, d = x_prompt.shape
    n_s, l_s, _ = x_sample.shape
    depth = ada_w.shape[0]
    n_p_rows = n_b * l_p
    total = n_p_rows + n_s * l_s
    assert n_p_rows % l_s == 0 and l_p % LANES == 0 and l_s % LANES == 0
    alpha = (2 * depth) ** 0.25
    a_heads = d // LANES

    def seg(i, tm):
        return _seg_index(i, tm, n_p_rows, l_s)

    x = _embed(x_prompt.reshape(n_p_rows, d), x_sample.reshape(n_s * l_s, d), l_s)

    cond8 = jnp.zeros((8, d), F32).at[0].set(c_ctx).at[1:1 + n_s].set(c)
    mods = _adaln(cond8, ada_w, ada_b).reshape(depth, 8, 6, d)

    lb_soft = jax.nn.softmax(hgrn_lb.astype(F32), axis=0)
    lower_bounds = jnp.cumsum(lb_soft, axis=0) - lb_soft[0]

    new_hgrn = []
    new_rwkv = []
    for l in range(depth):
        j = l // 2
        if l % 2 == 0:
            proj = _modmm(x, mods[l], hgrn_w_in[j].astype(BF16), seg)
            o, s_ctx = _hgrn_rec(proj, lower_bounds[j], hgrn_norm_g[j], None, None, n_seq=n_b, seq_len=l_p,
                                 row_block0=0, n_heads=a_heads, total_rows=total)
            (o,) = _hgrn_rec(proj, lower_bounds[j], hgrn_norm_g[j], state_hgrn[:, j], o, n_seq=n_s, seq_len=l_s,
                             row_block0=n_p_rows // l_s, n_heads=a_heads, total_rows=total)
            new_hgrn.append(s_ctx)
            w_o = hgrn_w_o[j]
        else:
            prep = _rwkv_prepare_weights(rwkv_mu[j], rwkv_w_rkv[j], rwkv_w0[j], rwkv_w_la[j], rwkv_w_lb[j],
                                         rwkv_a0[j], rwkv_a_la[j], rwkv_a_lb[j], rwkv_g_la[j], rwkv_g_lb[j])
            seq_kw = dict(n_prompt_rows=n_p_rows, prompt_len=l_p, sample_len=l_s)
            rkv = _rwkv_rkv(x, mods[l], prep["mu3"], prep["w_rkv"], seg, **seq_kw)
            lw, a, g = _rwkv_lora(x, mods[l], rwkv_mu[j], prep["lora"], seg, **seq_kw)
            params = (rwkv_k_k[j], rwkv_k_a[j], rwkv_r_k[j], rwkv_gn_g[j], rwkv_gn_b[j])
            o, s_ctx = _rwkv_rec(rkv, lw, a, g, params, None, None, n_seq=n_b, seq_len=l_p,
                                 row_block0=0, total_rows=total)
            (o,) = _rwkv_rec(rkv, lw, a, g, params, _pair_states(state_rwkv[:, j]), o, n_seq=n_s, seq_len=l_s,
                             row_block0=n_p_rows // l_s, total_rows=total)
            new_rwkv.append(s_ctx)
            w_o = rwkv_w_o[j]
        x = _post_mixer(o, x, mods[l], w_o.astype(BF16), ffn_w_up[l].astype(BF16), ffn_w_down[l].astype(BF16),
                        ln_g[l], ln_b[l], seg, alpha)

    y_prompt = x[:n_p_rows].reshape(n_b, l_p, d)
    y_sample = x[n_p_rows:].reshape(n_s, l_s, d)
    return (y_prompt, y_sample, jnp.stack(new_hgrn, axis=1), jnp.stack(new_rwkv, axis=1))
```

```python
import functools

import jax
import jax.numpy as jnp
from jax import lax
from jax.experimental import pallas as pl
from jax.experimental.pallas import tpu as pltpu

F32 = jnp.float32
BF16 = jnp.bfloat16
HIGHEST = lax.Precision.HIGHEST

LN_EPS = 1e-5
RMS_EPS = 1e-6
GN_EPS = 64e-5
DECAY_SCALE = 0.606531
EXP_CLIP = 80.0
POS_BASE = 10000.0
GRID_W = 64

LANES = 128
HGRN_CHUNK = 32
HGRN_CHUNK_LOG2 = HGRN_CHUNK.bit_length() - 1
HGRN_SAFE_EXPONENT = 75.0
RWKV_HEAD = 64
VMEM_LIMIT = 56 * 1024 * 1024

NT_DIMS = (((1,), (1,)), ((), ()))
TN_DIMS = (((0,), (0,)), ((), ()))


def _dot(a, b, dims=None, precision=None):
    if dims is None:
        return jnp.dot(a, b, preferred_element_type=F32, precision=precision)
    return lax.dot_general(a, b, dims, preferred_element_type=F32, precision=precision)


def _bdot(a, b, dims=None):
    return _dot(a.astype(BF16), b.astype(BF16), dims)


def _hdot(a, b, dims=None):
    return _dot(a, b, dims, precision=HIGHEST)


def _split_bf16(x):
    hi = x.astype(BF16)
    return hi, (x - hi.astype(F32)).astype(BF16)


def _dot_x3(a, b, dims=None):
    ah, al = _split_bf16(a)
    bh, bl = _split_bf16(b)
    if dims == NT_DIMS:
        a_cat = jnp.concatenate([ah, al], axis=1)
        b_half = jnp.concatenate([bh, bl], axis=0)
        b_cat = jnp.concatenate([b_half, b_half], axis=1)
        n = b.shape[0]
    else:
        a_cat = jnp.concatenate([ah, al], axis=0 if dims == TN_DIMS else 1)
        b_half = jnp.concatenate([bh, bl], axis=1)
        b_cat = jnp.concatenate([b_half, b_half], axis=0)
        n = b.shape[1]
    r = _dot(a_cat, b_cat, dims)
    return r[:, :n] + r[:, n:]


def _dot_x2l(a, b):
    ah, al = _split_bf16(a)
    b = b.astype(BF16)
    return _dot(jnp.concatenate([ah, al], axis=1), jnp.concatenate([b, b], axis=0))


def _dot_x2r(a, b):
    bh, bl = _split_bf16(b)
    n = b.shape[1]
    r = _dot(a.astype(BF16), jnp.concatenate([bh, bl], axis=1))
    return r[:, :n] + r[:, n:]


def _sigmoid(x):
    return jax.nn.sigmoid(x)


def _layer_norm(x, g, b):
    mu = jnp.mean(x, -1, keepdims=True)
    xc = x - mu
    var = jnp.mean(xc * xc, -1, keepdims=True)
    return xc * lax.rsqrt(var + LN_EPS) * g + b


def _seg_index(i, tm, n_prompt_rows, sample_len):
    start = i * tm
    return jnp.where(start < n_prompt_rows, 0, 1 + (start - n_prompt_rows) // sample_len)


def _adaln_kernel(c_ref, w_ref, b_ref, o_ref):
    c = c_ref[...]
    s = c * _sigmoid(c)
    o_ref[...] = _bdot(s, w_ref[...]) + b_ref[...]


def _adaln(cond8, ada_w, ada_b, tn=1536):
    depth, d, n = ada_w.shape
    return pl.pallas_call(
        _adaln_kernel,
        grid=(depth, n // tn),
        in_specs=[
            pl.BlockSpec((8, d), lambda l, j: (0, 0)),
            pl.BlockSpec((None, d, tn), lambda l, j: (l, 0, j)),
            pl.BlockSpec((None, 1, tn), lambda l, j: (l, 0, j)),
        ],
        out_specs=pl.BlockSpec((None, 8, tn), lambda l, j: (l, 0, j)),
        out_shape=jax.ShapeDtypeStruct((depth, 8, n), F32),
        compiler_params=pltpu.CompilerParams(
            dimension_semantics=("parallel", "parallel"), vmem_limit_bytes=VMEM_LIMIT),
        name="adaln",
    )(cond8, ada_w, ada_b.reshape(depth, 1, n))


def _modmm_kernel(x_ref, mod_ref, w_ref, o_ref, xb_ref):
    j = pl.program_id(1)

    @pl.when(j == 0)
    def _():
        m = mod_ref[...]
        xb_ref[...] = (x_ref[...] * (1.0 + m[1:2]) + m[0:1]).astype(BF16)

    o_ref[...] = _dot(xb_ref[...], w_ref[j])


def _modmm(x, mods, w, seg, tm=1024, tn=1024):
    t, d = x.shape
    n = w.shape[1]
    panels = w.reshape(d, n // tn, tn).transpose(1, 0, 2)
    return pl.pallas_call(
        _modmm_kernel,
        grid=(t // tm, n // tn),
        in_specs=[
            pl.BlockSpec((tm, d), lambda i, j: (i, 0)),
            pl.BlockSpec((None, 6, d), lambda i, j: (seg(i, tm), 0, 0)),
            pl.BlockSpec((n // tn, d, tn), lambda i, j: (0, 0, 0)),
        ],
        out_specs=pl.BlockSpec((tm, tn), lambda i, j: (i, j)),
        out_shape=jax.ShapeDtypeStruct((t, n), F32),
        scratch_shapes=[pltpu.VMEM((tm, d), BF16)],
        compiler_params=pltpu.CompilerParams(
            dimension_semantics=("parallel", "arbitrary"), vmem_limit_bytes=VMEM_LIMIT),
        name="modmm",
    )(x, mods, panels)


def _post_mixer_kernel(o_ref, x_ref, mod_ref, wo_ref, wu_ref, wd_ref, g_ref, b_ref,
                       y_ref, x1_ref, xb_ref, acc_ref, *, alpha):
    f = pl.program_id(1)

    @pl.when(f == 0)
    def _():
        m = mod_ref[...]
        y = _dot(o_ref[...], wo_ref[...])
        x1 = _layer_norm(alpha * x_ref[...] + m[2:3] * y, g_ref[0:1], b_ref[0:1])
        x1_ref[...] = x1
        xb_ref[...] = (x1 * (1.0 + m[4:5]) + m[3:4]).astype(BF16)
        acc_ref[...] = jnp.zeros_like(acc_ref)

    h = jnp.maximum(_dot(xb_ref[...], wu_ref[...]), 0.0)
    acc_ref[...] += _dot((h * h).astype(BF16), wd_ref[...])

    @pl.when(f == pl.num_programs(1) - 1)
    def _():
        gate = mod_ref[...][5:6]
        y_ref[...] = _layer_norm(alpha * x1_ref[...] + gate * acc_ref[...], g_ref[1:2], b_ref[1:2])


def _post_mixer(o, x, mods, w_o, w_up, w_down, ln_g, ln_b, seg, alpha, tm=1024, tf=1024):
    t, d = x.shape
    dff = w_up.shape[1]
    both = pl.BlockSpec((2, d), lambda i, f: (0, 0))
    return pl.pallas_call(
        functools.partial(_post_mixer_kernel, alpha=alpha),
        grid=(t // tm, dff // tf),
        in_specs=[
            pl.BlockSpec((tm, d), lambda i, f: (i, 0)),
            pl.BlockSpec((tm, d), lambda i, f: (i, 0)),
            pl.BlockSpec((None, 6, d), lambda i, f: (seg(i, tm), 0, 0)),
            pl.BlockSpec((d, d), lambda i, f: (0, 0)),
            pl.BlockSpec((d, tf), lambda i, f: (0, f)),
            pl.BlockSpec((tf, d), lambda i, f: (f, 0)),
            both, both,
        ],
        out_specs=pl.BlockSpec((tm, d), lambda i, f: (i, 0)),
        out_shape=jax.ShapeDtypeStruct((t, d), F32),
        scratch_shapes=[pltpu.VMEM((tm, d), F32), pltpu.VMEM((tm, d), BF16), pltpu.VMEM((tm, d), F32)],
        compiler_params=pltpu.CompilerParams(
            dimension_semantics=("parallel", "arbitrary"), vmem_limit_bytes=VMEM_LIMIT),
        name="post_mixer",
    )(o, x, mods, w_o, w_up, w_down, ln_g, ln_b)


def _hgrn_precompute(entries, r_i, c_i):
    pos = r_i & (HGRN_CHUNK - 1)
    same_chunk = (r_i >> HGRN_CHUNK_LOG2) == (c_i >> HGRN_CHUNK_LOG2)
    blk = same_chunk.astype(F32)
    pre = []
    for qr, zz, v, lbz, fwd in entries:
        causal = same_chunk & ((c_i <= r_i) if fwd else (c_i >= r_i))
        q = qr * _sigmoid(qr)
        e_abs = jnp.exp(-jnp.abs(zz))
        inv_1p = 1.0 / (1.0 + e_abs)
        log_f = jnp.minimum(zz, 0.0) + jnp.log((1.0 + lbz * jnp.exp(jnp.minimum(-zz, EXP_CLIP))) * inv_1p)
        k = (1.0 - lbz) * jnp.where(zz > 0.0, e_abs, 1.0) * inv_1p
        sums = _dot_x2r(jnp.concatenate([causal.astype(F32), blk], axis=0), log_f)
        cum = sums[:LANES]
        ctot = sums[LANES:]
        pre.append(dict(q=q, k=k, v=v, cum=cum, ctot=ctot, fwd=fwd, causal=causal,
                        q_in=q * jnp.exp(cum), k_out=k * jnp.exp(ctot - cum), dec=jnp.exp(ctot)))

    def scores_factored():
        out = []
        for e in pre:
            half = 0.5 * e["ctot"]
            qk = _dot_x3(e["q"] * jnp.exp(e["cum"] - half), e["k"] * jnp.exp(half - e["cum"]), NT_DIMS)
            out.append(jnp.where(e["causal"], qk, 0.0))
        return out

    def scores_pairwise():
        out = []
        for e in pre:
            fwd, q, k, cum = e["fwd"], e["q"], e["k"], e["cum"]
            scores = jnp.zeros((LANES, LANES), F32)
            for d in range(HGRN_CHUNK):
                if d == 0:
                    kr, cr = k, cum
                else:
                    sh = d if fwd else LANES - d
                    kr = pltpu.roll(k, sh, 0)
                    cr = pltpu.roll(cum, sh, 0)
                valid = (pos >= d) if fwd else (pos <= HGRN_CHUNK - 1 - d)
                ex = jnp.exp(jnp.where(valid, cum - cr, 0.0))
                s = jnp.sum(q * kr * ex, axis=-1, keepdims=True)
                tgt = (c_i == r_i - d) if fwd else (c_i == r_i + d)
                scores = scores + jnp.where(tgt & valid, s, 0.0)
            out.append(scores)
        return out

    lowest = pre[0]["ctot"]
    for e in pre[1:]:
        lowest = jnp.minimum(lowest, e["ctot"])
    scores = lax.cond(jnp.min(lowest) >= -2.0 * HGRN_SAFE_EXPONENT, scores_factored, scores_pairwise)

    chunk_of_lane = c_i >> HGRN_CHUNK_LOG2
    out = []
    for e, sc in zip(pre, scores):
        o_intra = _bdot(sc, e["v"])
        v_t = e["v"].T
        lhs = jnp.concatenate(
            [jnp.where(chunk_of_lane == c, v_t, 0.0) for c in range(LANES // HGRN_CHUNK)], axis=0)
        upd = _bdot(lhs, e["k_out"])
        out.append((e["q_in"], o_intra, upd, e["dec"]))
    return out


def _hgrn_rec_kernel(*refs, seq_len, zero_init, want_state, aliased):
    qf_ref, zf_ref, vf_ref, qb_ref, zb_ref, vb_ref, gate_ref, lb_ref, ng_ref = refs[:9]
    pos = 9
    s0_ref = None
    if not zero_init:
        s0_ref = refs[pos]
        pos += 1
    if aliased:
        pos += 1
    o_ref = refs[pos]
    pos += 1
    sfin_ref = None
    if want_state:
        sfin_ref = refs[pos]
        pos += 1
    osum_ref, st_ref = refs[pos:pos + 2]

    n_tiles = seq_len // LANES
    n_chunks = LANES // HGRN_CHUNK
    unroll = 2
    assert n_tiles % unroll == 0 and (n_tiles == unroll or (n_tiles // 2) % unroll == 0)
    r_i = lax.broadcasted_iota(jnp.int32, (LANES, LANES), 0)
    c_i = lax.broadcasted_iota(jnp.int32, (LANES, LANES), 1)
    lb = lb_ref[...]
    data = ((qf_ref, zf_ref, vf_ref), (qb_ref, zb_ref, vb_ref))

    for z in range(2):
        if zero_init:
            st_ref[z] = jnp.zeros((LANES, LANES), F32)
        else:
            st_ref[z] = s0_ref[z].T

    def finish(rows, o):
        o = o * lax.rsqrt(jnp.mean(o * o, -1, keepdims=True) + RMS_EPS)
        g = gate_ref[rows, :]
        o_ref[rows, :] = (o * ng_ref[...] * (g * _sigmoid(g))).astype(o_ref.dtype)

    def block(i, visit):
        slots = []
        for z in range(2):
            for u in range(unroll):
                t = i * unroll + u
                slots.append((z, t if z == 0 else n_tiles - 1 - t))
        entries = []
        for z, t in slots:
            rows = pl.ds(pl.multiple_of(t * LANES, LANES), LANES)
            q_ref, z_ref, v_ref = data[z]
            entries.append((q_ref[rows, :], z_ref[rows, :], v_ref[rows, :], lb[z:z + 1], z == 0))
        pre = _hgrn_precompute(entries, r_i, c_i)

        st = [st_ref[0], st_ref[1]]
        inter = [[None] * n_chunks for _ in slots]
        for u in range(unroll):
            for step in range(n_chunks):
                for z in range(2):
                    idx = z * unroll + u
                    q_in, _, upd, dec = pre[idx]
                    c = step if z == 0 else n_chunks - 1 - step
                    lo = c * HGRN_CHUNK
                    inter[idx][c] = _bdot(q_in[lo:lo + HGRN_CHUNK], st[z], NT_DIMS)
                    st[z] = st[z] * dec[lo:lo + 1] + upd[c * LANES:(c + 1) * LANES]
        st_ref[0] = st[0]
        st_ref[1] = st[1]
        outs = [p[1] + jnp.concatenate(o, axis=0) for p, o in zip(pre, inter)]

        if visit == "both":
            for u in range(unroll):
                finish(pl.ds(u * LANES, LANES), outs[u] + outs[unroll + (n_tiles - 1 - u)])
            return
        for (z, t), out in zip(slots, outs):
            rows = pl.ds(pl.multiple_of(t * LANES, LANES), LANES)
            if visit == "first":
                osum_ref[rows, :] = out
            else:
                finish(rows, osum_ref[rows, :] + out)

    if n_tiles == unroll:
        block(0, "both")
    else:
        def body(i, carry, visit):
            block(i, visit)
            return carry

        half = n_tiles // 2 // unroll
        lax.fori_loop(0, half, functools.partial(body, visit="first"), 0)
        lax.fori_loop(half, 2 * half, functools.partial(body, visit="second"), 0)
    if want_state:
        for z in range(2):
            sfin_ref[z] = st_ref[z].T


def _hgrn_rec(proj, lb, norm_g, s0, prev_out, *, n_seq, seq_len, row_block0, n_heads, total_rows):
    d = n_heads * LANES
    zero_init = s0 is None
    want_state = s0 is None
    aliased = prev_out is not None

    def col(block):
        return lambda b, h: (row_block0 + b, block * n_heads + h)

    blk = (seq_len, LANES)
    in_specs = [pl.BlockSpec(blk, col(0)), pl.BlockSpec(blk, col(1)), pl.BlockSpec(blk, col(2)),
                pl.BlockSpec(blk, col(3)), pl.BlockSpec(blk, col(4)), pl.BlockSpec(blk, col(5)),
                pl.BlockSpec(blk, col(6)),
                pl.BlockSpec((2, LANES), lambda b, h: (0, h)),
                pl.BlockSpec((1, LANES), lambda b, h: (0, h))]
    args = [proj] * 7 + [lb, norm_g.reshape(1, d)]
    if not zero_init:
        in_specs.append(pl.BlockSpec((None, 2, None, LANES, LANES), lambda b, h: (b, 0, h, 0, 0)))
        args.append(s0)
    io_alias = {}
    if aliased:
        in_specs.append(pl.BlockSpec(memory_space=pl.ANY))
        io_alias = {len(args): 0}
        args.append(prev_out)
    out_specs = [pl.BlockSpec(blk, lambda b, h: (row_block0 + b, h))]
    out_shape = [jax.ShapeDtypeStruct((total_rows, d), BF16)]
    if want_state:
        out_specs.append(pl.BlockSpec((None, 2, None, LANES, LANES), lambda b, h: (b, 0, h, 0, 0)))
        out_shape.append(jax.ShapeDtypeStruct((n_seq, 2, n_heads, LANES, LANES), F32))
    res = pl.pallas_call(
        functools.partial(_hgrn_rec_kernel, seq_len=seq_len, zero_init=zero_init,
                          want_state=want_state, aliased=aliased),
        grid=(n_seq, n_heads),
        in_specs=in_specs,
        out_specs=out_specs,
        out_shape=out_shape,
        scratch_shapes=[pltpu.VMEM((seq_len, LANES), F32), pltpu.VMEM((2, LANES, LANES), F32)],
        input_output_aliases=io_alias,
        compiler_params=pltpu.CompilerParams(
            dimension_semantics=("parallel", "parallel"), vmem_limit_bytes=VMEM_LIMIT),
        name="hgrn_rec",
    )(*args)
    return res


def _int_mod(x, n):
    return x & (n - 1) if n & (n - 1) == 0 else lax.rem(x, n)


def _token_shift(x_ref, xp_ref, xn_ref, mod_ref, tile_start, n_prompt_rows, prompt_len, sample_len):
    m = mod_ref[...]
    sh, sc = m[0:1], 1.0 + m[1:2]
    h = x_ref[...] * sc + sh
    h_before = xp_ref[7:8, :] * sc + sh
    h_after = xn_ref[0:1, :] * sc + sh
    tm = h.shape[0]
    rr = lax.broadcasted_iota(jnp.int32, (tm, 1), 0)
    grow = tile_start + rr
    in_prompt = grow < n_prompt_rows
    pos = jnp.where(in_prompt, _int_mod(grow, prompt_len), _int_mod(grow - n_prompt_rows, sample_len))
    last = jnp.where(in_prompt, prompt_len - 1, sample_len - 1)
    prev = jnp.where(rr == 0, h_before, pltpu.roll(h, 1, 0))
    prev = jnp.where(pos == 0, 0.0, prev)
    nxt = jnp.where(rr == tm - 1, h_after, pltpu.roll(h, tm - 1, 0))
    nxt = jnp.where(pos == last, 0.0, nxt)
    return h, 0.5 * (prev + nxt) - h


def _rwkv_rkv_kernel(x_ref, xp_ref, xn_ref, mod_ref, mu_ref, w_ref, o_ref, h_ref, xx_ref, *, tm, seq_info):
    @pl.when(pl.program_id(1) == 0)
    def _():
        h, xx = _token_shift(x_ref, xp_ref, xn_ref, mod_ref, pl.program_id(0) * tm, *seq_info)
        h_ref[...] = h
        xx_ref[...] = xx

    xs = h_ref[...] + xx_ref[...] * mu_ref[...]
    o_ref[...] = _dot(xs.astype(BF16), w_ref[pl.program_id(1)])


def _halo_specs(tm, d, t, n_grid_axes):
    nb = t // 8

    def before(i, *_):
        return (jnp.maximum(i * (tm // 8) - 1, 0), 0)

    def after(i, *_):
        return (jnp.minimum((i + 1) * (tm // 8), nb - 1), 0)

    del n_grid_axes
    return pl.BlockSpec((8, d), before), pl.BlockSpec((8, d), after)


def _rwkv_rkv(x, mods, mu3, w_rkv, seg, *, n_prompt_rows, prompt_len, sample_len, tm=512):
    t, d = x.shape
    n = w_rkv.shape[2]
    before, after = _halo_specs(tm, d, t, 2)
    return pl.pallas_call(
        functools.partial(_rwkv_rkv_kernel, tm=tm, seq_info=(n_prompt_rows, prompt_len, sample_len)),
        grid=(t // tm, 3),
        in_specs=[
            pl.BlockSpec((tm, d), lambda i, j: (i, 0)),
            before, after,
            pl.BlockSpec((None, 6, d), lambda i, j: (seg(i, tm), 0, 0)),
            pl.BlockSpec((None, 1, d), lambda i, j: (j, 0, 0)),
            pl.BlockSpec((3, d, n), lambda i, j: (0, 0, 0)),
        ],
        out_specs=pl.BlockSpec((None, tm, n), lambda i, j: (j, i, 0)),
        out_shape=jax.ShapeDtypeStruct((3, t, n), F32),
        scratch_shapes=[pltpu.VMEM((tm, d), F32), pltpu.VMEM((tm, d), F32)],
        compiler_params=pltpu.CompilerParams(
            dimension_semantics=("parallel", "arbitrary"), vmem_limit_bytes=VMEM_LIMIT),
        name="rwkv_rkv",
    )(x, x, x, mods, mu3, w_rkv)


def _rwkv_lora_kernel(x_ref, xp_ref, xn_ref, mod_ref, mu_ref, wla_ref, wlb_ref, w0_ref, ala_ref, alb_ref,
                      a0_ref, gla_ref, glb_ref, lw_ref, a_ref, g_ref, *, tm, seq_info):
    h, xx = _token_shift(x_ref, xp_ref, xn_ref, mod_ref, pl.program_id(0) * tm, *seq_info)
    mu = mu_ref[...]
    xs_w = (h + xx * mu[1:2]).astype(BF16)
    xs_a = (h + xx * mu[4:5]).astype(BF16)
    xs_g = (h + xx * mu[5:6]).astype(BF16)
    zw = w0_ref[...] + _bdot(jnp.tanh(_dot(xs_w, wla_ref[...])), wlb_ref[...])
    lw_ref[...] = -DECAY_SCALE * _sigmoid(zw)
    za = a0_ref[...] + _bdot(_dot(xs_a, ala_ref[...]), alb_ref[...])
    a_ref[...] = _sigmoid(za)
    g_ref[...] = _bdot(_sigmoid(_dot(xs_g, gla_ref[...])), glb_ref[...])


def _rwkv_lora(x, mods, mu, weights, seg, *, n_prompt_rows, prompt_len, sample_len, tm=256):
    t, d = x.shape
    before, after = _halo_specs(tm, d, t, 1)

    def whole(arr):
        return pl.BlockSpec(arr.shape, lambda i: (0,) * arr.ndim)

    return pl.pallas_call(
        functools.partial(_rwkv_lora_kernel, tm=tm, seq_info=(n_prompt_rows, prompt_len, sample_len)),
        grid=(t // tm,),
        in_specs=[pl.BlockSpec((tm, d), lambda i: (i, 0)), before, after,
                  pl.BlockSpec((None, 6, d), lambda i: (seg(i, tm), 0, 0)), whole(mu)]
                 + [whole(w) for w in weights],
        out_specs=[pl.BlockSpec((tm, 2 * d), lambda i: (i, 0)),
                   pl.BlockSpec((tm, 2 * d), lambda i: (i, 0)),
                   pl.BlockSpec((tm, d), lambda i: (i, 0))],
        out_shape=[jax.ShapeDtypeStruct((t, 2 * d), F32), jax.ShapeDtypeStruct((t, 2 * d), F32),
                   jax.ShapeDtypeStruct((t, d), F32)],
        compiler_params=pltpu.CompilerParams(
            dimension_semantics=("parallel",), vmem_limit_bytes=VMEM_LIMIT),
        name="rwkv_lora",
    )(x, x, x, mods, mu, *weights)


def _rwkv_precompute(entries, consts):
    bd, eye, head_masks, r_i, c_i = consts
    n = len(entries)
    pre = []
    for r, k, v, lw, a, (kkp, kap, rkp), fwd in entries:
        tri = ((c_i <= r_i) if fwd else (c_i >= r_i)).astype(F32)
        kk = k * kkp
        kk = kk / jnp.maximum(jnp.sqrt(_dot_x2l(kk * kk, bd)), 1e-12)
        k2 = k * (1.0 + (a - 1.0) * kap)
        bonus = _dot_x2l(r * k2 * rkp, bd) * v
        b = kk * a
        cl = _dot_x2r(tri, lw)
        cle = cl - lw
        cm = cl[LANES // 2:LANES // 2 + 1]
        ct = cl[LANES - 1:LANES] if fwd else cl[0:1]
        e_inv = jnp.exp(cm - cl)
        e_out = jnp.exp(ct - cl)
        pre.append(dict(
            v=v, bonus=bonus, fwd=fwd,
            kkt=kk * jnp.exp(cle - cm), rt=r * jnp.exp(cl - cm), kh=k2 * e_inv, bh=b * e_inv,
            kkd=kk * jnp.exp(cle), rd=r * jnp.exp(cl), kg=k2 * e_out, bg=b * e_out, gc=jnp.exp(ct)))

    for e in pre:
        lhs = jnp.concatenate([e["kkt"] * head_masks[0], e["kkt"] * head_masks[1],
                               e["rt"] * head_masks[0], e["rt"] * head_masks[1]], axis=0)
        e["gk"] = _dot_x3(lhs, e["kh"], NT_DIMS)
        e["gb"] = _dot_x3(lhs, e["bh"], NT_DIMS)

    chains = []
    for e in pre:
        strict = (c_i < r_i) if e["fwd"] else (c_i > r_i)
        incl = (c_i <= r_i) if e["fwd"] else (c_i >= r_i)
        for hd in range(2):
            lo = hd * LANES
            chains.append(dict(
                e=e, hd=hd,
                a_k=jnp.where(strict, e["gk"][lo:lo + LANES], 0.0),
                a_b=jnp.where(strict, e["gb"][lo:lo + LANES], 0.0),
                b_k=jnp.where(incl, e["gk"][2 * LANES + lo:3 * LANES + lo], 0.0),
                b_b=jnp.where(incl, e["gb"][2 * LANES + lo:3 * LANES + lo], 0.0)))

    for c in chains:
        c["m"] = eye - jnp.where((r_i >> 1) == (c_i >> 1), c["a_b"], 0.0)
    for log_k in range(1, 7):
        k = 1 << log_k
        off = ((r_i >> (log_k + 1)) == (c_i >> (log_k + 1))) & ((r_i >> log_k) != (c_i >> log_k))
        if k < 8:
            for c in chains:
                c["t"] = _dot_x3(jnp.where(off, c["a_b"], 0.0), c["m"])
            for c in chains:
                c["m"] = c["m"] - _dot_x3(c["m"], c["t"])
            continue

        def take(x, fwd):
            first = k if fwd else 0
            return jnp.concatenate([x[lo:lo + k] for lo in range(first, LANES, 2 * k)], axis=0)

        def spread(rows, fwd):
            zero = jnp.zeros((k, LANES), F32)
            parts = []
            for j in range(LANES // (2 * k)):
                piece = rows[j * k:(j + 1) * k]
                parts += [zero, piece] if fwd else [piece, zero]
            return jnp.concatenate(parts, axis=0)

        for c in chains:
            fwd = c["e"]["fwd"]
            c["t"] = spread(_dot_x3(take(jnp.where(off, c["a_b"], 0.0), fwd), c["m"]), fwd)
        for c in chains:
            fwd = c["e"]["fwd"]
            c["m"] = c["m"] - spread(_dot_x3(take(c["m"], fwd), c["t"]), fwd)

    for c in chains:
        e = c["e"]
        c["v_h"] = e["v"] * head_masks[c["hd"]]
        c["kkd_h"] = e["kkd"] * head_masks[c["hd"]]
        c["akv"] = _dot_x3(c["a_k"], c["v_h"])
    for c in chains:
        c["wt"] = _dot_x3(c["m"], c["kkd_h"])
        c["u0"] = _dot_x3(c["m"], c["akv"])
    for c in chains:
        c["y0"] = _dot_x3(c["b_k"], c["v_h"]) - _dot_x3(c["b_b"], c["u0"])
        c["rp"] = _dot_x3(c["b_b"], c["wt"])

    out = []
    for i, e in enumerate(pre):
        c0, c1 = chains[2 * i], chains[2 * i + 1]
        wt = c0["wt"] + c1["wt"]
        u0 = c0["u0"] + c1["u0"]
        q0 = bd * (_dot_x3(e["v"], e["kg"], TN_DIMS) - _dot_x3(u0.T, e["bg"]))
        p_mat = _dot_x3(wt, e["bg"], TN_DIMS)
        out.append((e["rd"] - c0["rp"] - c1["rp"], c0["y0"] + c1["y0"], e["gc"], q0, p_mat, e["bonus"]))
    assert len(out) == n
    return out


def _rwkv_rec_kernel(*refs, seq_len, zero_init, want_state, aliased):
    data = (refs[0:5], refs[5:10])
    g_ref, kk_ref, ka_ref, rk_ref, gg_ref, gb_ref = refs[10:16]
    pos = 16
    s0_ref = None
    if not zero_init:
        s0_ref = refs[pos]
        pos += 1
    if aliased:
        pos += 1
    o_ref = refs[pos]
    pos += 1
    sfin_ref = None
    if want_state:
        sfin_ref = refs[pos]
        pos += 1
    osum_ref, st_ref = refs[pos:pos + 2]

    n_tiles = seq_len // LANES
    unroll = 2
    assert n_tiles % unroll == 0 and (n_tiles == unroll or (n_tiles // 2) % unroll == 0)
    r_i = lax.broadcasted_iota(jnp.int32, (LANES, LANES), 0)
    c_i = lax.broadcasted_iota(jnp.int32, (LANES, LANES), 1)
    bd = ((r_i >> 6) == (c_i >> 6)).astype(F32)
    eye = (r_i == c_i).astype(F32)
    lane = lax.broadcasted_iota(jnp.int32, (1, LANES), 1)
    head_masks = ((lane < RWKV_HEAD).astype(F32), (lane >= RWKV_HEAD).astype(F32))
    consts = (bd, eye, head_masks, r_i, c_i)
    inv_n = 1.0 / RWKV_HEAD

    for z in range(2):
        st_ref[z] = jnp.zeros((LANES, LANES), F32) if zero_init else s0_ref[z]

    def block(i, visit):
        slots = []
        for z in range(2):
            for u in range(unroll):
                t = i * unroll + u
                slots.append((z, t if z == 0 else n_tiles - 1 - t))
        entries = []
        for z, t in slots:
            rows = pl.ds(pl.multiple_of(t * LANES, LANES), LANES)
            r_ref, k_ref, v_ref, lw_ref, a_ref = data[z]
            params = (kk_ref[z:z + 1, :], ka_ref[z:z + 1, :], rk_ref[z:z + 1, :])
            entries.append((r_ref[rows, :], k_ref[rows, :], v_ref[rows, :], lw_ref[rows, :], a_ref[rows, :],
                            params, z == 0))
        pre = _rwkv_precompute(entries, consts)

        ys = []
        st = [st_ref[0], st_ref[1]]
        for (z, _), (rp, y0, gc, q0, p_mat, _) in zip(slots, pre):
            ys.append(_dot_x3(rp, st[z], NT_DIMS) + y0)
            st[z] = st[z] * gc + q0 - bd * _dot_x3(st[z], p_mat)
        st_ref[0] = st[0]
        st_ref[1] = st[1]

        outs = []
        for (z, _), y, (_, _, _, _, _, bonus) in zip(slots, ys, pre):
            mean = _dot_x2l(y, bd) * inv_n
            yc = y - mean
            var = _dot_x2l(yc * yc, bd) * inv_n
            outs.append(yc * lax.rsqrt(var + GN_EPS) * gg_ref[z:z + 1, :] + gb_ref[z:z + 1, :] + bonus)

        if visit == "both":
            for u in range(unroll):
                rows = pl.ds(u * LANES, LANES)
                total = outs[u] + outs[unroll + (n_tiles - 1 - u)]
                o_ref[rows, :] = (total * g_ref[rows, :]).astype(o_ref.dtype)
            return
        for (z, t), out in zip(slots, outs):
            rows = pl.ds(pl.multiple_of(t * LANES, LANES), LANES)
            if visit == "first":
                osum_ref[rows, :] = out
            else:
                o_ref[rows, :] = ((osum_ref[rows, :] + out) * g_ref[rows, :]).astype(o_ref.dtype)

    if n_tiles == unroll:
        block(0, "both")
    else:
        def body(i, carry, visit):
            block(i, visit)
            return carry

        half = n_tiles // 2 // unroll
        lax.fori_loop(0, half, functools.partial(body, visit="first"), 0)
        lax.fori_loop(half, 2 * half, functools.partial(body, visit="second"), 0)
    if want_state:
        for z in range(2):
            st = st_ref[z]
            sfin_ref[z, 0] = st[:RWKV_HEAD, :RWKV_HEAD]
            sfin_ref[z, 1] = pltpu.roll(st, RWKV_HEAD, 1)[RWKV_HEAD:, :RWKV_HEAD]


def _rwkv_rec(rkv, lw, a, g, params, s0, prev_out, *, n_seq, seq_len, row_block0, total_rows):
    d = g.shape[1]
    n_pairs = d // LANES
    zero_init = s0 is None
    want_state = s0 is None
    aliased = prev_out is not None
    blk = (seq_len, LANES)

    def dir_specs(z):
        col = lambda b, p: (row_block0 + b, z * n_pairs + p)
        return ([pl.BlockSpec((None,) + blk, lambda b, p, j=j: (j, row_block0 + b, z * n_pairs + p))
                 for j in range(3)] + [pl.BlockSpec(blk, col), pl.BlockSpec(blk, col)])

    in_specs = dir_specs(0) + dir_specs(1)
    in_specs.append(pl.BlockSpec(blk, lambda b, p: (row_block0 + b, p)))
    in_specs += [pl.BlockSpec((2, LANES), lambda b, p: (0, p)) for _ in range(5)]
    args = [rkv, rkv, rkv, lw, a] * 2 + [g] + list(params)
    state_spec = pl.BlockSpec((None, 2, None, LANES, LANES), lambda b, p: (b, 0, p, 0, 0))
    if not zero_init:
        in_specs.append(state_spec)
        args.append(s0)
    io_alias = {}
    if aliased:
        in_specs.append(pl.BlockSpec(memory_space=pl.ANY))
        io_alias = {len(args): 0}
        args.append(prev_out)
    out_specs = [pl.BlockSpec(blk, lambda b, p: (row_block0 + b, p))]
    out_shape = [jax.ShapeDtypeStruct((total_rows, d), BF16)]
    if want_state:
        out_specs.append(pl.BlockSpec((None, 2, 2, RWKV_HEAD, RWKV_HEAD), lambda b, p: (b, 0, p, 0, 0)))
        out_shape.append(jax.ShapeDtypeStruct((n_seq, 2, 2 * n_pairs, RWKV_HEAD, RWKV_HEAD), F32))
    return pl.pallas_call(
        functools.partial(_rwkv_rec_kernel, seq_len=seq_len, zero_init=zero_init,
                          want_state=want_state, aliased=aliased),
        grid=(n_seq, n_pairs),
        in_specs=in_specs,
        out_specs=out_specs,
        out_shape=out_shape,
        scratch_shapes=[pltpu.VMEM((seq_len, LANES), F32), pltpu.VMEM((2, LANES, LANES), F32)],
        input_output_aliases=io_alias,
        compiler_params=pltpu.CompilerParams(
            dimension_semantics=("parallel", "parallel"), vmem_limit_bytes=VMEM_LIMIT),
        name="rwkv_rec",
    )(*args)


def _rwkv_prepare_weights(mu, w_rkv, w0, w_la, w_lb, a0, a_la, a_lb, g_la, g_lb):
    d = mu.shape[1]
    rank_w = w_la.shape[2]
    rank_a = a_la.shape[2]
    rank_g = g_la.shape[1]
    rank_g_pad = -(-rank_g // LANES) * LANES

    def block_diag(w):
        rank = w.shape[1]
        out = jnp.zeros((2, rank, 2, d), w.dtype)
        out = out.at[0, :, 0, :].set(w[0]).at[1, :, 1, :].set(w[1])
        return out.reshape(2 * rank, 2 * d)

    lora = (
        w_la.reshape(d, 2 * rank_w).astype(BF16), block_diag(w_lb).astype(BF16), w0.reshape(1, 2 * d),
        a_la.reshape(d, 2 * rank_a).astype(BF16), block_diag(a_lb).astype(BF16), a0.reshape(1, 2 * d),
        jnp.pad(g_la, ((0, 0), (0, rank_g_pad - rank_g))).astype(BF16),
        jnp.pad(g_lb, ((0, rank_g_pad - rank_g), (0, 0))).astype(BF16),
    )
    mu3 = jnp.stack([mu[0], mu[2], mu[3]]).reshape(3, 1, d)
    return {"mu3": mu3, "w_rkv": w_rkv.astype(BF16), "lora": lora}


def _pair_states(s):
    n, _, h, hn, _ = s.shape
    s = s.reshape(n, 2, h // 2, 2, hn, hn)
    out = jnp.zeros((n, 2, h // 2, 2, hn, 2, hn), s.dtype)
    out = out.at[:, :, :, 0, :, 0, :].set(s[:, :, :, 0]).at[:, :, :, 1, :, 1, :].set(s[:, :, :, 1])
    return out.reshape(n, 2, h // 2, 2 * hn, 2 * hn)


def _embed_kernel(xp_ref, xs_ref, row_ref, col_ref, o_ref, *, n_prompt_tiles):
    i = pl.program_id(0)

    @pl.when(i < n_prompt_tiles)
    def _():
        o_ref[...] = xp_ref[...]

    @pl.when(i >= n_prompt_tiles)
    def _():
        half = o_ref.shape[1] // 2
        for grp in range(o_ref.shape[0] // GRID_W):
            rows = pl.ds(grp * GRID_W, GRID_W)
            o_ref[rows, :half] = xs_ref[rows, :half] + row_ref[grp:grp + 1, :]
            o_ref[rows, half:] = xs_ref[rows, half:] + col_ref[...]


def _grid_pos_tables(n_tokens, d):
    quarter = d // 4
    omega = 1.0 / (POS_BASE ** (jnp.arange(quarter, dtype=F32) / quarter))
    r = jnp.arange(n_tokens // GRID_W, dtype=F32)[:, None] * omega
    cc = jnp.arange(GRID_W, dtype=F32)[:, None] * omega
    return (jnp.concatenate([jnp.sin(r), jnp.cos(r)], -1), jnp.concatenate([jnp.sin(cc), jnp.cos(cc)], -1))


def _embed(xp, xs, sample_len, tm=512):
    n_p, d = xp.shape
    n_s = xs.shape[0]
    npt = n_p // tm
    pos_tiles = sample_len // tm
    grid_rows = tm // GRID_W
    row_emb, col_emb = _grid_pos_tables(sample_len, d)
    return pl.pallas_call(
        functools.partial(_embed_kernel, n_prompt_tiles=npt),
        grid=((n_p + n_s) // tm,),
        in_specs=[
            pl.BlockSpec((tm, d), lambda i: (jnp.minimum(i, npt - 1), 0)),
            pl.BlockSpec((tm, d), lambda i: (jnp.maximum(i - npt, 0), 0)),
            pl.BlockSpec((grid_rows, d // 2), lambda i: (lax.rem(jnp.maximum(i - npt, 0), pos_tiles), 0)),
            pl.BlockSpec((GRID_W, d // 2), lambda i: (0, 0)),
        ],
        out_specs=pl.BlockSpec((tm, d), lambda i: (i, 0)),
        out_shape=jax.ShapeDtypeStruct((n_p + n_s, d), F32),
        compiler_params=pltpu.CompilerParams(
            dimension_semantics=("parallel",), vmem_limit_bytes=VMEM_LIMIT),
        name="embed",
    )(xp, xs, row_emb, col_emb)


def kernel(x_prompt, x_sample, state_hgrn, state_rwkv, c, c_ctx, ada_w, ada_b, ln_g, ln_b, ffn_w_up, ffn_w_down, hgrn_w_in, hgrn_lb, hgrn_norm_g, hgrn_w_o, rwkv_mu, rwkv_w_rkv, rwkv_w0, rwkv_w_la, rwkv_w_lb, rwkv_a0, rwkv_a_la, rwkv_a_lb, rwkv_g_la, rwkv_g_lb, rwkv_k_k, rwkv_k_a, rwkv_r_k, rwkv_gn_g, rwkv_gn_b, rwkv_w_o):
    n_b, l_p, d = x_prompt.shape
    n_s, l_s, _ = x_sample.shape
    depth = ada_w.shape[0]
    n_p_rows = n_b * l_p
    total = n_p_rows + n_s * l_s
    assert n_p_rows % l_s == 0 and l_p % LANES == 0 and l_s % LANES == 0
    alpha = (2 * depth) ** 0.25
    a_heads = d // LANES

    def seg(i, tm):
        return _seg_index(i, tm, n_p_rows, l_s)

    x = _embed(x_prompt.reshape(n_p_rows, d), x_sample.reshape(n_s * l_s, d), l_s)

    cond8 = jnp.zeros((8, d), F32).at[0].set(c_ctx).at[1:1 + n_s].set(c)
    mods = _adaln(cond8, ada_w, ada_b).reshape(depth, 8, 6, d)

    lb_soft = jax.nn.softmax(hgrn_lb.astype(F32), axis=0)
    lower_bounds = jnp.cumsum(lb_soft, axis=0) - lb_soft[0]

    new_hgrn = []
    new_rwkv = []
    for l in range(depth):
        j = l // 2
        if l % 2 == 0:
            proj = _modmm(x, mods[l], hgrn_w_in[j].astype(BF16), seg)
            o, s_ctx = _hgrn_rec(proj, lower_bounds[j], hgrn_norm_g[j], None, None, n_seq=n_b, seq_len=l_p,
                                 row_block0=0, n_heads=a_heads, total_rows=total)
            (o,) = _hgrn_rec(proj, lower_bounds[j], hgrn_norm_g[j], state_hgrn[:, j], o, n_seq=n_s, seq_len=l_s,
                             row_block0=n_p_rows // l_s, n_heads=a_heads, total_rows=total)
            new_hgrn.append(s_ctx)
            w_o = hgrn_w_o[j]
        else:
            prep = _rwkv_prepare_weights(rwkv_mu[j], rwkv_w_rkv[j], rwkv_w0[j], rwkv_w_la[j], rwkv_w_lb[j],
                                         rwkv_a0[j], rwkv_a_la[j], rwkv_a_lb[j], rwkv_g_la[j], rwkv_g_lb[j])
            seq_kw = dict(n_prompt_rows=n_p_rows, prompt_len=l_p, sample_len=l_s)
            rkv = _rwkv_rkv(x, mods[l], prep["mu3"], prep["w_rkv"], seg, **seq_kw)
            lw, a, g = _rwkv_lora(x, mods[l], rwkv_mu[j], prep["lora"], seg, **seq_kw)
            params = (rwkv_k_k[j], rwkv_k_a[j], rwkv_r_k[j], rwkv_gn_g[j], rwkv_gn_b[j])
            o, s_ctx = _rwkv_rec(rkv, lw, a, g, params, None, None, n_seq=n_b, seq_len=l_p,
                                 row_block0=0, total_rows=total)
            (o,) = _rwkv_rec(rkv, lw, a, g, params, _pair_states(state_rwkv[:, j]), o, n_seq=n_s, seq_len=l_s,
                             row_block0=n_p_rows // l_s, total_rows=total)
            new_rwkv.append(s_ctx)
            w_o = rwkv_w_o[j]
        x = _post_mixer(o, x, mods[l], w_o.astype(BF16), ffn_w_up[l].astype(BF16), ffn_w_down[l].astype(BF16),
                        ln_g[l], ln_b[l], seg, alpha)

    y_prompt = x[:n_p_rows].reshape(n_b, l_p, d)
    y_sample = x[n_p_rows:].reshape(n_s, l_s, d)
    return (y_prompt, y_sample, jnp.stack(new_hgrn, axis=1), jnp.stack(new_rwkv, axis=1))
```

```python
import functools

import jax
import jax.numpy as jnp
from jax import lax
from jax.experimental import pallas as pl
from jax.experimental.pallas import tpu as pltpu

F32 = jnp.float32
BF16 = jnp.bfloat16
HIGHEST = lax.Precision.HIGHEST

LN_EPS = 1e-5
RMS_EPS = 1e-6
GN_EPS = 64e-5
DECAY_SCALE = 0.606531
EXP_CLIP = 80.0
POS_BASE = 10000.0
GRID_W = 64

LANES = 128
HGRN_CHUNK = 32
HGRN_CHUNK_LOG2 = HGRN_CHUNK.bit_length() - 1
HGRN_SAFE_EXPONENT = 75.0
RWKV_HEAD = 64
VMEM_LIMIT = 56 * 1024 * 1024

NT_DIMS = (((1,), (1,)), ((), ()))
TN_DIMS = (((0,), (0,)), ((), ()))


def _dot(a, b, dims=None, precision=None):
    if dims is None:
        return jnp.dot(a, b, preferred_element_type=F32, precision=precision)
    return lax.dot_general(a, b, dims, preferred_element_type=F32, precision=precision)


def _bdot(a, b, dims=None):
    return _dot(a.astype(BF16), b.astype(BF16), dims)


def _hdot(a, b, dims=None):
    return _dot(a, b, dims, precision=HIGHEST)


def _split_bf16(x):
    hi = x.astype(BF16)
    return hi, (x - hi.astype(F32)).astype(BF16)


def _dot_x3(a, b, dims=None):
    ah, al = _split_bf16(a)
    bh, bl = _split_bf16(b)
    if dims == NT_DIMS:
        a_cat = jnp.concatenate([ah, al], axis=1)
        b_half = jnp.concatenate([bh, bl], axis=0)
        b_cat = jnp.concatenate([b_half, b_half], axis=1)
        n = b.shape[0]
    else:
        a_cat = jnp.concatenate([ah, al], axis=0 if dims == TN_DIMS else 1)
        b_half = jnp.concatenate([bh, bl], axis=1)
        b_cat = jnp.concatenate([b_half, b_half], axis=0)
        n = b.shape[1]
    r = _dot(a_cat, b_cat, dims)
    return r[:, :n] + r[:, n:]


def _dot_x2l(a, b):
    ah, al = _split_bf16(a)
    b = b.astype(BF16)
    return _dot(jnp.concatenate([ah, al], axis=1), jnp.concatenate([b, b], axis=0))


def _dot_x2r(a, b):
    bh, bl = _split_bf16(b)
    n = b.shape[1]
    r = _dot(a.astype(BF16), jnp.concatenate([bh, bl], axis=1))
    return r[:, :n] + r[:, n:]


def _sigmoid(x):
    return jax.nn.sigmoid(x)


def _layer_norm(x, g, b):
    mu = jnp.mean(x, -1, keepdims=True)
    xc = x - mu
    var = jnp.mean(xc * xc, -1, keepdims=True)
    return xc * lax.rsqrt(var + LN_EPS) * g + b


def _seg_index(i, tm, n_prompt_rows, sample_len):
    start = i * tm
    return jnp.where(start < n_prompt_rows, 0, 1 + (start - n_prompt_rows) // sample_len)


def _adaln_kernel(c_ref, w_ref, b_ref, o_ref):
    c = c_ref[...]
    s = c * _sigmoid(c)
    o_ref[...] = _bdot(s, w_ref[...]) + b_ref[...]


def _adaln(cond8, ada_w, ada_b, tn=1536):
    depth, d, n = ada_w.shape
    return pl.pallas_call(
        _adaln_kernel,
        grid=(depth, n // tn),
        in_specs=[
            pl.BlockSpec((8, d), lambda l, j: (0, 0)),
            pl.BlockSpec((None, d, tn), lambda l, j: (l, 0, j)),
            pl.BlockSpec((None, 1, tn), lambda l, j: (l, 0, j)),
        ],
        out_specs=pl.BlockSpec((None, 8, tn), lambda l, j: (l, 0, j)),
        out_shape=jax.ShapeDtypeStruct((depth, 8, n), F32),
        compiler_params=pltpu.CompilerParams(
            dimension_semantics=("parallel", "parallel"), vmem_limit_bytes=VMEM_LIMIT),
        name="adaln",
    )(cond8, ada_w, ada_b.reshape(depth, 1, n))


def _modmm_kernel(x_ref, mod_ref, w_ref, o_ref, xb_ref):
    j = pl.program_id(1)

    @pl.when(j == 0)
    def _():
        m = mod_ref[...]
        xb_ref[...] = (x_ref[...] * (1.0 + m[1:2]) + m[0:1]).astype(BF16)

    o_ref[...] = _dot(xb_ref[...], w_ref[j])


def _modmm(x, mods, w, seg, tm=1024, tn=1024):
    t, d = x.shape
    n = w.shape[1]
    panels = w.reshape(d, n // tn, tn).transpose(1, 0, 2)
    return pl.pallas_call(
        _modmm_kernel,
        grid=(t // tm, n // tn),
        in_specs=[
            pl.BlockSpec((tm, d), lambda i, j: (i, 0)),
            pl.BlockSpec((None, 6, d), lambda i, j: (seg(i, tm), 0, 0)),
            pl.BlockSpec((n // tn, d, tn), lambda i, j: (0, 0, 0)),
        ],
        out_specs=pl.BlockSpec((tm, tn), lambda i, j: (i, j)),
        out_shape=jax.ShapeDtypeStruct((t, n), F32),
        scratch_shapes=[pltpu.VMEM((tm, d), BF16)],
        compiler_params=pltpu.CompilerParams(
            dimension_semantics=("parallel", "arbitrary"), vmem_limit_bytes=VMEM_LIMIT),
        name="modmm",
    )(x, mods, panels)


def _post_mixer_kernel(o_ref, x_ref, mod_ref, wo_ref, wu_ref, wd_ref, g_ref, b_ref,
                       y_ref, x1_ref, xb_ref, acc_ref, *, alpha):
    f = pl.program_id(1)

    @pl.when(f == 0)
    def _():
        m = mod_ref[...]
        y = _dot(o_ref[...], wo_ref[...])
        x1 = _layer_norm(alpha * x_ref[...] + m[2:3] * y, g_ref[0:1], b_ref[0:1])
        x1_ref[...] = x1
        xb_ref[...] = (x1 * (1.0 + m[4:5]) + m[3:4]).astype(BF16)
        acc_ref[...] = jnp.zeros_like(acc_ref)

    h = jnp.maximum(_dot(xb_ref[...], wu_ref[...]), 0.0)
    acc_ref[...] += _dot((h * h).astype(BF16), wd_ref[...])

    @pl.when(f == pl.num_programs(1) - 1)
    def _():
        gate = mod_ref[...][5:6]
        y_ref[...] = _layer_norm(alpha * x1_ref[...] + gate * acc_ref[...], g_ref[1:2], b_ref[1:2])


def _post_mixer(o, x, mods, w_o, w_up, w_down, ln_g, ln_b, seg, alpha, tm=1024, tf=1024):
    t, d = x.shape
    dff = w_up.shape[1]
    both = pl.BlockSpec((2, d), lambda i, f: (0, 0))
    return pl.pallas_call(
        functools.partial(_post_mixer_kernel, alpha=alpha),
        grid=(t // tm, dff // tf),
        in_specs=[
            pl.BlockSpec((tm, d), lambda i, f: (i, 0)),
            pl.BlockSpec((tm, d), lambda i, f: (i, 0)),
            pl.BlockSpec((None, 6, d), lambda i, f: (seg(i, tm), 0, 0)),
            pl.BlockSpec((d, d), lambda i, f: (0, 0)),
            pl.BlockSpec((d, tf), lambda i, f: (0, f)),
            pl.BlockSpec((tf, d), lambda i, f: (f, 0)),
            both, both,
        ],
        out_specs=pl.BlockSpec((tm, d), lambda i, f: (i, 0)),
        out_shape=jax.ShapeDtypeStruct((t, d), F32),
        scratch_shapes=[pltpu.VMEM((tm, d), F32), pltpu.VMEM((tm, d), BF16), pltpu.VMEM((tm, d), F32)],
        compiler_params=pltpu.CompilerParams(
            dimension_semantics=("parallel", "arbitrary"), vmem_limit_bytes=VMEM_LIMIT),
        name="post_mixer",
    )(o, x, mods, w_o, w_up, w_down, ln_g, ln_b)


def _hgrn_precompute(entries, r_i, c_i):
    pos = r_i & (HGRN_CHUNK - 1)
    same_chunk = (r_i >> HGRN_CHUNK_LOG2) == (c_i >> HGRN_CHUNK_LOG2)
    blk = same_chunk.astype(F32)
    pre = []
    for qr, zz, v, lbz, fwd in entries:
        causal = same_chunk & ((c_i <= r_i) if fwd else (c_i >= r_i))
        q = qr * _sigmoid(qr)
        e_abs = jnp.exp(-jnp.abs(zz))
        inv_1p = 1.0 / (1.0 + e_abs)
        log_f = jnp.minimum(zz, 0.0) + jnp.log((1.0 + lbz * jnp.exp(jnp.minimum(-zz, EXP_CLIP))) * inv_1p)
        k = (1.0 - lbz) * jnp.where(zz > 0.0, e_abs, 1.0) * inv_1p
        sums = _dot_x2r(jnp.concatenate([causal.astype(F32), blk], axis=0), log_f)
        cum = sums[:LANES]
        ctot = sums[LANES:]
        pre.append(dict(q=q, k=k, v=v, cum=cum, ctot=ctot, fwd=fwd, causal=causal,
                        q_in=q * jnp.exp(cum), k_out=k * jnp.exp(ctot - cum), dec=jnp.exp(ctot)))

    def scores_factored():
        out = []
        for e in pre:
            half = 0.5 * e["ctot"]
            qk = _dot_x3(e["q"] * jnp.exp(e["cum"] - half), e["k"] * jnp.exp(half - e["cum"]), NT_DIMS)
            out.append(jnp.where(e["causal"], qk, 0.0))
        return out

    def scores_pairwise():
        out = []
        for e in pre:
            fwd, q, k, cum = e["fwd"], e["q"], e["k"], e["cum"]
            scores = jnp.zeros((LANES, LANES), F32)
            for d in range(HGRN_CHUNK):
                if d == 0:
                    kr, cr = k, cum
                else:
                    sh = d if fwd else LANES - d
                    kr = pltpu.roll(k, sh, 0)
                    cr = pltpu.roll(cum, sh, 0)
                valid = (pos >= d) if fwd else (pos <= HGRN_CHUNK - 1 - d)
                ex = jnp.exp(jnp.where(valid, cum - cr, 0.0))
                s = jnp.sum(q * kr * ex, axis=-1, keepdims=True)
                tgt = (c_i == r_i - d) if fwd else (c_i == r_i + d)
                scores = scores + jnp.where(tgt & valid, s, 0.0)
            out.append(scores)
        return out

    lowest = pre[0]["ctot"]
    for e in pre[1:]:
        lowest = jnp.minimum(lowest, e["ctot"])
    scores = lax.cond(jnp.min(lowest) >= -2.0 * HGRN_SAFE_EXPONENT, scores_factored, scores_pairwise)

    chunk_of_lane = c_i >> HGRN_CHUNK_LOG2
    out = []
    for e, sc in zip(pre, scores):
        o_intra = _bdot(sc, e["v"])
        v_t = e["v"].T
        lhs = jnp.concatenate(
            [jnp.where(chunk_of_lane == c, v_t, 0.0) for c in range(LANES // HGRN_CHUNK)], axis=0)
        upd = _bdot(lhs, e["k_out"])
        out.append((e["q_in"], o_intra, upd, e["dec"]))
    return out


def _hgrn_rec_kernel(*refs, seq_len, zero_init, want_state, aliased):
    qf_ref, zf_ref, vf_ref, qb_ref, zb_ref, vb_ref, gate_ref, lb_ref, ng_ref = refs[:9]
    pos = 9
    s0_ref = None
    if not zero_init:
        s0_ref = refs[pos]
        pos += 1
    if aliased:
        pos += 1
    o_ref = refs[pos]
    pos += 1
    sfin_ref = None
    if want_state:
        sfin_ref = refs[pos]
        pos += 1
    osum_ref, st_ref = refs[pos:pos + 2]

    n_tiles = seq_len // LANES
    n_chunks = LANES // HGRN_CHUNK
    unroll = 2
    assert n_tiles % unroll == 0 and (n_tiles == unroll or (n_tiles // 2) % unroll == 0)
    r_i = lax.broadcasted_iota(jnp.int32, (LANES, LANES), 0)
    c_i = lax.broadcasted_iota(jnp.int32, (LANES, LANES), 1)
    lb = lb_ref[...]
    data = ((qf_ref, zf_ref, vf_ref), (qb_ref, zb_ref, vb_ref))

    for z in range(2):
        if zero_init:
            st_ref[z] = jnp.zeros((LANES, LANES), F32)
        else:
            st_ref[z] = s0_ref[z].T

    def finish(rows, o):
        o = o * lax.rsqrt(jnp.mean(o * o, -1, keepdims=True) + RMS_EPS)
        g = gate_ref[rows, :]
        o_ref[rows, :] = (o * ng_ref[...] * (g * _sigmoid(g))).astype(o_ref.dtype)

    def block(i, visit):
        slots = []
        for z in range(2):
            for u in range(unroll):
                t = i * unroll + u
                slots.append((z, t if z == 0 else n_tiles - 1 - t))
        entries = []
        for z, t in slots:
            rows = pl.ds(pl.multiple_of(t * LANES, LANES), LANES)
            q_ref, z_ref, v_ref = data[z]
            entries.append((q_ref[rows, :], z_ref[rows, :], v_ref[rows, :], lb[z:z + 1], z == 0))
        pre = _hgrn_precompute(entries, r_i, c_i)

        st = [st_ref[0], st_ref[1]]
        inter = [[None] * n_chunks for _ in slots]
        for u in range(unroll):
            for step in range(n_chunks):
                for z in range(2):
                    idx = z * unroll + u
                    q_in, _, upd, dec = pre[idx]
                    c = step if z == 0 else n_chunks - 1 - step
                    lo = c * HGRN_CHUNK
                    inter[idx][c] = _bdot(q_in[lo:lo + HGRN_CHUNK], st[z], NT_DIMS)
                    st[z] = st[z] * dec[lo:lo + 1] + upd[c * LANES:(c + 1) * LANES]
        st_ref[0] = st[0]
        st_ref[1] = st[1]
        outs = [p[1] + jnp.concatenate(o, axis=0) for p, o in zip(pre, inter)]

        if visit == "both":
            for u in range(unroll):
                finish(pl.ds(u * LANES, LANES), outs[u] + outs[unroll + (n_tiles - 1 - u)])
            return
        for (z, t), out in zip(slots, outs):
            rows = pl.ds(pl.multiple_of(t * LANES, LANES), LANES)
            if visit == "first":
                osum_ref[rows, :] = out
            else:
                finish(rows, osum_ref[rows, :] + out)

    if n_tiles == unroll:
        block(0, "both")
    else:
        def body(i, carry, visit):
            block(i, visit)
            return carry

        half = n_tiles // 2 // unroll
        lax.fori_loop(0, half, functools.partial(body, visit="first"), 0)
        lax.fori_loop(half, 2 * half, functools.partial(body, visit="second"), 0)
    if want_state:
        for z in range(2):
            sfin_ref[z] = st_ref[z].T


def _hgrn_rec(proj, lb, norm_g, s0, prev_out, *, n_seq, seq_len, row_block0, n_heads, total_rows):
    d = n_heads * LANES
    zero_init = s0 is None
    want_state = s0 is None
    aliased = prev_out is not None

    def col(block):
        return lambda b, h: (row_block0 + b, block * n_heads + h)

    blk = (seq_len, LANES)
    in_specs = [pl.BlockSpec(blk, col(0)), pl.BlockSpec(blk, col(1)), pl.BlockSpec(blk, col(2)),
                pl.BlockSpec(blk, col(3)), pl.BlockSpec(blk, col(4)), pl.BlockSpec(blk, col(5)),
                pl.BlockSpec(blk, col(6)),
                pl.BlockSpec((2, LANES), lambda b, h: (0, h)),
                pl.BlockSpec((1, LANES), lambda b, h: (0, h))]
    args = [proj] * 7 + [lb, norm_g.reshape(1, d)]
    if not zero_init:
        in_specs.append(pl.BlockSpec((None, 2, None, LANES, LANES), lambda b, h: (b, 0, h, 0, 0)))
        args.append(s0)
    io_alias = {}
    if aliased:
        in_specs.append(pl.BlockSpec(memory_space=pl.ANY))
        io_alias = {len(args): 0}
        args.append(prev_out)
    out_specs = [pl.BlockSpec(blk, lambda b, h: (row_block0 + b, h))]
    out_shape = [jax.ShapeDtypeStruct((total_rows, d), BF16)]
    if want_state:
        out_specs.append(pl.BlockSpec((None, 2, None, LANES, LANES), lambda b, h: (b, 0, h, 0, 0)))
        out_shape.append(jax.ShapeDtypeStruct((n_seq, 2, n_heads, LANES, LANES), F32))
    res = pl.pallas_call(
        functools.partial(_hgrn_rec_kernel, seq_len=seq_len, zero_init=zero_init,
                          want_state=want_state, aliased=aliased),
        grid=(n_seq, n_heads),
        in_specs=in_specs,
        out_specs=out_specs,
        out_shape=out_shape,
        scratch_shapes=[pltpu.VMEM((seq_len, LANES), F32), pltpu.VMEM((2, LANES, LANES), F32)],
        input_output_aliases=io_alias,
        compiler_params=pltpu.CompilerParams(
            dimension_semantics=("parallel", "parallel"), vmem_limit_bytes=VMEM_LIMIT),
        name="hgrn_rec",
    )(*args)
    return res


def _int_mod(x, n):
    return x & (n - 1) if n & (n - 1) == 0 else lax.rem(x, n)


def _token_shift(x_ref, xp_ref, xn_ref, mod_ref, tile_start, n_prompt_rows, prompt_len, sample_len):
    m = mod_ref[...]
    sh, sc = m[0:1], 1.0 + m[1:2]
    h = x_ref[...] * sc + sh
    h_before = xp_ref[7:8, :] * sc + sh
    h_after = xn_ref[0:1, :] * sc + sh
    tm = h.shape[0]
    rr = lax.broadcasted_iota(jnp.int32, (tm, 1), 0)
    grow = tile_start + rr
    in_prompt = grow < n_prompt_rows
    pos = jnp.where(in_prompt, _int_mod(grow, prompt_len), _int_mod(grow - n_prompt_rows, sample_len))
    last = jnp.where(in_prompt, prompt_len - 1, sample_len - 1)
    prev = jnp.where(rr == 0, h_before, pltpu.roll(h, 1, 0))
    prev = jnp.where(pos == 0, 0.0, prev)
    nxt = jnp.where(rr == tm - 1, h_after, pltpu.roll(h, tm - 1, 0))
    nxt = jnp.where(pos == last, 0.0, nxt)
    return h, 0.5 * (prev + nxt) - h


def _rwkv_rkv_kernel(x_ref, xp_ref, xn_ref, mod_ref, mu_ref, w_ref, o_ref, h_ref, xx_ref, *, tm, seq_info):
    @pl.when(pl.program_id(1) == 0)
    def _():
        h, xx = _token_shift(x_ref, xp_ref, xn_ref, mod_ref, pl.program_id(0) * tm, *seq_info)
        h_ref[...] = h
        xx_ref[...] = xx

    xs = h_ref[...] + xx_ref[...] * mu_ref[...]
    o_ref[...] = _dot(xs.astype(BF16), w_ref[pl.program_id(1)])


def _halo_specs(tm, d, t, n_grid_axes):
    nb = t // 8

    def before(i, *_):
        return (jnp.maximum(i * (tm // 8) - 1, 0), 0)

    def after(i, *_):
        return (jnp.minimum((i + 1) * (tm // 8), nb - 1), 0)

    del n_grid_axes
    return pl.BlockSpec((8, d), before), pl.BlockSpec((8, d), after)


def _rwkv_rkv(x, mods, mu3, w_rkv, seg, *, n_prompt_rows, prompt_len, sample_len, tm=512):
    t, d = x.shape
    n = w_rkv.shape[2]
    before, after = _halo_specs(tm, d, t, 2)
    return pl.pallas_call(
        functools.partial(_rwkv_rkv_kernel, tm=tm, seq_info=(n_prompt_rows, prompt_len, sample_len)),
        grid=(t // tm, 3),
        in_specs=[
            pl.BlockSpec((tm, d), lambda i, j: (i, 0)),
            before, after,
            pl.BlockSpec((None, 6, d), lambda i, j: (seg(i, tm), 0, 0)),
            pl.BlockSpec((None, 1, d), lambda i, j: (j, 0, 0)),
            pl.BlockSpec((3, d, n), lambda i, j: (0, 0, 0)),
        ],
        out_specs=pl.BlockSpec((None, tm, n), lambda i, j: (j, i, 0)),
        out_shape=jax.ShapeDtypeStruct((3, t, n), F32),
        scratch_shapes=[pltpu.VMEM((tm, d), F32), pltpu.VMEM((tm, d), F32)],
        compiler_params=pltpu.CompilerParams(
            dimension_semantics=("parallel", "arbitrary"), vmem_limit_bytes=VMEM_LIMIT),
        name="rwkv_rkv",
    )(x, x, x, mods, mu3, w_rkv)


def _rwkv_lora_kernel(x_ref, xp_ref, xn_ref, mod_ref, mu_ref, wla_ref, wlb_ref, w0_ref, ala_ref, alb_ref,
                      a0_ref, gla_ref, glb_ref, lw_ref, a_ref, g_ref, *, tm, seq_info):
    h, xx = _token_shift(x_ref, xp_ref, xn_ref, mod_ref, pl.program_id(0) * tm, *seq_info)
    mu = mu_ref[...]
    xs_w = (h + xx * mu[1:2]).astype(BF16)
    xs_a = (h + xx * mu[4:5]).astype(BF16)
    xs_g = (h + xx * mu[5:6]).astype(BF16)
    zw = w0_ref[...] + _bdot(jnp.tanh(_dot(xs_w, wla_ref[...])), wlb_ref[...])
    lw_ref[...] = -DECAY_SCALE * _sigmoid(zw)
    za = a0_ref[...] + _bdot(_dot(xs_a, ala_ref[...]), alb_ref[...])
    a_ref[...] = _sigmoid(za)
    g_ref[...] = _bdot(_sigmoid(_dot(xs_g, gla_ref[...])), glb_ref[...])


def _rwkv_lora(x, mods, mu, weights, seg, *, n_prompt_rows, prompt_len, sample_len, tm=256):
    t, d = x.shape
    before, after = _halo_specs(tm, d, t, 1)

    def whole(arr):
        return pl.BlockSpec(arr.shape, lambda i: (0,) * arr.ndim)

    return pl.pallas_call(
        functools.partial(_rwkv_lora_kernel, tm=tm, seq_info=(n_prompt_rows, prompt_len, sample_len)),
        grid=(t // tm,),
        in_specs=[pl.BlockSpec((tm, d), lambda i: (i, 0)), before, after,
                  pl.BlockSpec((None, 6, d), lambda i: (seg(i, tm), 0, 0)), whole(mu)]
                 + [whole(w) for w in weights],
        out_specs=[pl.BlockSpec((tm, 2 * d), lambda i: (i, 0)),
                   pl.BlockSpec((tm, 2 * d), lambda i: (i, 0)),
                   pl.BlockSpec((tm, d), lambda i: (i, 0))],
        out_shape=[jax.ShapeDtypeStruct((t, 2 * d), F32), jax.ShapeDtypeStruct((t, 2 * d), F32),
                   jax.ShapeDtypeStruct((t, d), F32)],
        compiler_params=pltpu.CompilerParams(
            dimension_semantics=("parallel",), vmem_limit_bytes=VMEM_LIMIT),
        name="rwkv_lora",
    )(x, x, x, mods, mu, *weights)


def _rwkv_precompute(entries, consts):
    bd, eye, head_masks, r_i, c_i = consts
    n = len(entries)
    pre = []
    for r, k, v, lw, a, (kkp, kap, rkp), fwd in entries:
        tri = ((c_i <= r_i) if fwd else (c_i >= r_i)).astype(F32)
        kk = k * kkp
        kk = kk / jnp.maximum(jnp.sqrt(_dot_x2l(kk * kk, bd)), 1e-12)
        k2 = k * (1.0 + (a - 1.0) * kap)
        bonus = _dot_x2l(r * k2 * rkp, bd) * v
        b = kk * a
        cl = _dot_x2r(tri, lw)
        cle = cl - lw
        cm = cl[LANES // 2:LANES // 2 + 1]
        ct = cl[LANES - 1:LANES] if fwd else cl[0:1]
        e_inv = jnp.exp(cm - cl)
        e_out = jnp.exp(ct - cl)
        pre.append(dict(
            v=v, bonus=bonus, fwd=fwd,
            kkt=kk * jnp.exp(cle - cm), rt=r * jnp.exp(cl - cm), kh=k2 * e_inv, bh=b * e_inv,
            kkd=kk * jnp.exp(cle), rd=r * jnp.exp(cl), kg=k2 * e_out, bg=b * e_out, gc=jnp.exp(ct)))

    for e in pre:
        lhs = jnp.concatenate([e["kkt"] * head_masks[0], e["kkt"] * head_masks[1],
                               e["rt"] * head_masks[0], e["rt"] * head_masks[1]], axis=0)
        e["gk"] = _dot_x3(lhs, e["kh"], NT_DIMS)
        e["gb"] = _dot_x3(lhs, e["bh"], NT_DIMS)

    chains = []
    for e in pre:
        strict = (c_i < r_i) if e["fwd"] else (c_i > r_i)
        incl = (c_i <= r_i) if e["fwd"] else (c_i >= r_i)
        for hd in range(2):
            lo = hd * LANES
            chains.append(dict(
                e=e, hd=hd,
                a_k=jnp.where(strict, e["gk"][lo:lo + LANES], 0.0),
                a_b=jnp.where(strict, e["gb"][lo:lo + LANES], 0.0),
                b_k=jnp.where(incl, e["gk"][2 * LANES + lo:3 * LANES + lo], 0.0),
                b_b=jnp.where(incl, e["gb"][2 * LANES + lo:3 * LANES + lo], 0.0)))

    for c in chains:
        c["m"] = eye - jnp.where((r_i >> 1) == (c_i >> 1), c["a_b"], 0.0)
    for log_k in range(1, 7):
        k = 1 << log_k
        off = ((r_i >> (log_k + 1)) == (c_i >> (log_k + 1))) & ((r_i >> log_k) != (c_i >> log_k))
        if k < 8:
            for c in chains:
                c["t"] = _dot_x3(jnp.where(off, c["a_b"], 0.0), c["m"])
            for c in chains:
                c["m"] = c["m"] - _dot_x3(c["m"], c["t"])
            continue

        def take(x, fwd):
            first = k if fwd else 0
            return jnp.concatenate([x[lo:lo + k] for lo in range(first, LANES, 2 * k)], axis=0)

        def spread(rows, fwd):
            zero = jnp.zeros((k, LANES), F32)
            parts = []
            for j in range(LANES // (2 * k)):
                piece = rows[j * k:(j + 1) * k]
                parts += [zero, piece] if fwd else [piece, zero]
            return jnp.concatenate(parts, axis=0)

        for c in chains:
            fwd = c["e"]["fwd"]
            c["t"] = spread(_dot_x3(take(jnp.where(off, c["a_b"], 0.0), fwd), c["m"]), fwd)
        for c in chains:
            fwd = c["e"]["fwd"]
            c["m"] = c["m"] - spread(_dot_x3(take(c["m"], fwd), c["t"]), fwd)

    for c in chains:
        e = c["e"]
        c["v_h"] = e["v"] * head_masks[c["hd"]]
        c["kkd_h"] = e["kkd"] * head_masks[c["hd"]]
        c["akv"] = _dot_x3(c["a_k"], c["v_h"])
    for c in chains:
        c["wt"] = _dot_x3(c["m"], c["kkd_h"])
        c["u0"] = _dot_x3(c["m"], c["akv"])
    for c in chains:
        c["y0"] = _dot_x3(c["b_k"], c["v_h"]) - _dot_x3(c["b_b"], c["u0"])
        c["rp"] = _dot_x3(c["b_b"], c["wt"])

    out = []
    for i, e in enumerate(pre):
        c0, c1 = chains[2 * i], chains[2 * i + 1]
        wt = c0["wt"] + c1["wt"]
        u0 = c0["u0"] + c1["u0"]
        q0 = bd * (_dot_x3(e["v"], e["kg"], TN_DIMS) - _dot_x3(u0.T, e["bg"]))
        p_mat = _dot_x3(wt, e["bg"], TN_DIMS)
        out.append((e["rd"] - c0["rp"] - c1["rp"], c0["y0"] + c1["y0"], e["gc"], q0, p_mat, e["bonus"]))
    assert len(out) == n
    return out


def _rwkv_rec_kernel(*refs, seq_len, groups, zero_init, want_state, aliased):
    data = (refs[0:5], refs[5:10])
    g_ref, kk_ref, ka_ref, rk_ref, gg_ref, gb_ref = refs[10:16]
    pos = 16
    s0_ref = None
    if not zero_init:
        s0_ref = refs[pos]
        pos += 1
    if aliased:
        pos += 1
    o_ref = refs[pos]
    pos += 1
    sfin_ref = None
    if want_state:
        sfin_ref = refs[pos]
        pos += 1
    osum_ref, st_ref = refs[pos:pos + 2]

    n_tiles = seq_len // LANES
    unroll = min(8 // (2 * groups), n_tiles)
    assert n_tiles % unroll == 0 and (n_tiles == unroll or (n_tiles // 2) % unroll == 0)
    r_i = lax.broadcasted_iota(jnp.int32, (LANES, LANES), 0)
    c_i = lax.broadcasted_iota(jnp.int32, (LANES, LANES), 1)
    bd = ((r_i >> 6) == (c_i >> 6)).astype(F32)
    eye = (r_i == c_i).astype(F32)
    lane = lax.broadcasted_iota(jnp.int32, (1, LANES), 1)
    head_masks = ((lane < RWKV_HEAD).astype(F32), (lane >= RWKV_HEAD).astype(F32))
    consts = (bd, eye, head_masks, r_i, c_i)
    inv_n = 1.0 / RWKV_HEAD

    def lanes_of(grp):
        return slice(grp * LANES, (grp + 1) * LANES)

    for z in range(2):
        for grp in range(groups):
            st_ref[z * groups + grp] = jnp.zeros((LANES, LANES), F32) if zero_init else s0_ref[z, grp]

    def block(i, visit):
        slots = []
        for z in range(2):
            for grp in range(groups):
                for u in range(unroll):
                    t = i * unroll + u
                    slots.append((z, u, grp, t if z == 0 else n_tiles - 1 - t))
        entries = []
        for z, _, grp, t in slots:
            rows = pl.ds(pl.multiple_of(t * LANES, LANES), LANES)
            cols = lanes_of(grp)
            r_ref, k_ref, v_ref, lw_ref, a_ref = data[z]
            params = (kk_ref[z:z + 1, cols], ka_ref[z:z + 1, cols], rk_ref[z:z + 1, cols])
            entries.append((r_ref[rows, cols], k_ref[rows, cols], v_ref[rows, cols], lw_ref[rows, cols],
                            a_ref[rows, cols], params, z == 0))
        pre = _rwkv_precompute(entries, consts)

        ys = []
        st = [st_ref[c] for c in range(2 * groups)]
        for (z, _, grp, _), (rp, y0, gc, q0, p_mat, _) in zip(slots, pre):
            c = z * groups + grp
            ys.append(_dot_x3(rp, st[c], NT_DIMS) + y0)
            st[c] = st[c] * gc + q0 - bd * _dot_x3(st[c], p_mat)
        for c in range(2 * groups):
            st_ref[c] = st[c]

        outs = {}
        for (z, u, grp, _), y, (_, _, _, _, _, bonus) in zip(slots, ys, pre):
            cols = lanes_of(grp)
            mean = _dot_x2l(y, bd) * inv_n
            yc = y - mean
            var = _dot_x2l(yc * yc, bd) * inv_n
            outs[(z, u, grp)] = (yc * lax.rsqrt(var + GN_EPS) * gg_ref[z:z + 1, cols] + gb_ref[z:z + 1, cols]
                                 + bonus)

        if visit == "both":
            for t in range(n_tiles):
                for grp in range(groups):
                    rows, cols = pl.ds(t * LANES, LANES), lanes_of(grp)
                    total = outs[(0, t, grp)] + outs[(1, n_tiles - 1 - t, grp)]
                    o_ref[rows, cols] = (total * g_ref[rows, cols]).astype(o_ref.dtype)
            return
        for z, u, grp, t in slots:
            out = outs[(z, u, grp)]
            rows, cols = pl.ds(pl.multiple_of(t * LANES, LANES), LANES), lanes_of(grp)
            if visit == "first":
                osum_ref[rows, cols] = out
            else:
                o_ref[rows, cols] = ((osum_ref[rows, cols] + out) * g_ref[rows, cols]).astype(o_ref.dtype)

    if n_tiles == unroll:
        block(0, "both")
    else:
        def body(i, carry, visit):
            block(i, visit)
            return carry

        half = n_tiles // 2 // unroll
        lax.fori_loop(0, half, functools.partial(body, visit="first"), 0)
        lax.fori_loop(half, 2 * half, functools.partial(body, visit="second"), 0)
    if want_state:
        for z in range(2):
            for grp in range(groups):
                st = st_ref[z * groups + grp]
                sfin_ref[z, 2 * grp] = st[:RWKV_HEAD, :RWKV_HEAD]
                sfin_ref[z, 2 * grp + 1] = pltpu.roll(st, RWKV_HEAD, 1)[RWKV_HEAD:, :RWKV_HEAD]


def _rwkv_rec(rkv, lw, a, g, params, s0, prev_out, *, n_seq, seq_len, row_block0, total_rows):
    d = g.shape[1]
    groups = 2 if seq_len // LANES <= 2 else 1
    n_cols = d // (groups * LANES)
    zero_init = s0 is None
    want_state = s0 is None
    aliased = prev_out is not None
    blk = (seq_len, groups * LANES)

    def dir_specs(z):
        col = lambda b, p: (row_block0 + b, z * n_cols + p)
        return ([pl.BlockSpec((None,) + blk, lambda b, p, j=j: (j, row_block0 + b, z * n_cols + p))
                 for j in range(3)] + [pl.BlockSpec(blk, col), pl.BlockSpec(blk, col)])

    in_specs = dir_specs(0) + dir_specs(1)
    in_specs.append(pl.BlockSpec(blk, lambda b, p: (row_block0 + b, p)))
    in_specs += [pl.BlockSpec((2, groups * LANES), lambda b, p: (0, p)) for _ in range(5)]
    args = [rkv, rkv, rkv, lw, a] * 2 + [g] + list(params)
    if not zero_init:
        in_specs.append(pl.BlockSpec((None, 2, groups, LANES, LANES), lambda b, p: (b, 0, p, 0, 0)))
        args.append(s0)
    io_alias = {}
    if aliased:
        in_specs.append(pl.BlockSpec(memory_space=pl.ANY))
        io_alias = {len(args): 0}
        args.append(prev_out)
    out_specs = [pl.BlockSpec(blk, lambda b, p: (row_block0 + b, p))]
    out_shape = [jax.ShapeDtypeStruct((total_rows, d), BF16)]
    if want_state:
        out_specs.append(pl.BlockSpec((None, 2, 2 * groups, RWKV_HEAD, RWKV_HEAD), lambda b, p: (b, 0, p, 0, 0)))
        out_shape.append(jax.ShapeDtypeStruct((n_seq, 2, d // RWKV_HEAD, RWKV_HEAD, RWKV_HEAD), F32))
    return pl.pallas_call(
        functools.partial(_rwkv_rec_kernel, seq_len=seq_len, groups=groups, zero_init=zero_init,
                          want_state=want_state, aliased=aliased),
        grid=(n_seq, n_cols),
        in_specs=in_specs,
        out_specs=out_specs,
        out_shape=out_shape,
        scratch_shapes=[pltpu.VMEM((seq_len, groups * LANES), F32),
                        pltpu.VMEM((2 * groups, LANES, LANES), F32)],
        input_output_aliases=io_alias,
        compiler_params=pltpu.CompilerParams(
            dimension_semantics=("parallel", "parallel"), vmem_limit_bytes=VMEM_LIMIT),
        name="rwkv_rec",
    )(*args)


def _rwkv_prepare_weights(mu, w_rkv, w0, w_la, w_lb, a0, a_la, a_lb, g_la, g_lb):
    d = mu.shape[1]
    rank_w = w_la.shape[2]
    rank_a = a_la.shape[2]
    rank_g = g_la.shape[1]
    rank_g_pad = -(-rank_g // LANES) * LANES

    def block_diag(w):
        rank = w.shape[1]
        out = jnp.zeros((2, rank, 2, d), w.dtype)
        out = out.at[0, :, 0, :].set(w[0]).at[1, :, 1, :].set(w[1])
        return out.reshape(2 * rank, 2 * d)

    lora = (
        w_la.reshape(d, 2 * rank_w).astype(BF16), block_diag(w_lb).astype(BF16), w0.reshape(1, 2 * d),
        a_la.reshape(d, 2 * rank_a).astype(BF16), block_diag(a_lb).astype(BF16), a0.reshape(1, 2 * d),
        jnp.pad(g_la, ((0, 0), (0, rank_g_pad - rank_g))).astype(BF16),
        jnp.pad(g_lb, ((0, rank_g_pad - rank_g), (0, 0))).astype(BF16),
    )
    mu3 = jnp.stack([mu[0], mu[2], mu[3]]).reshape(3, 1, d)
    return {"mu3": mu3, "w_rkv": w_rkv.astype(BF16), "lora": lora}


def _pair_states(s):
    n, _, h, hn, _ = s.shape
    s = s.reshape(n, 2, h // 2, 2, hn, hn)
    out = jnp.zeros((n, 2, h // 2, 2, hn, 2, hn), s.dtype)
    out = out.at[:, :, :, 0, :, 0, :].set(s[:, :, :, 0]).at[:, :, :, 1, :, 1, :].set(s[:, :, :, 1])
    return out.reshape(n, 2, h // 2, 2 * hn, 2 * hn)


def _embed_kernel(xp_ref, xs_ref, row_ref, col_ref, o_ref, *, n_prompt_tiles):
    i = pl.program_id(0)

    @pl.when(i < n_prompt_tiles)
    def _():
        o_ref[...] = xp_ref[...]

    @pl.when(i >= n_prompt_tiles)
    def _():
        half = o_ref.shape[1] // 2
        for grp in range(o_ref.shape[0] // GRID_W):
            rows = pl.ds(grp * GRID_W, GRID_W)
            o_ref[rows, :half] = xs_ref[rows, :half] + row_ref[grp:grp + 1, :]
            o_ref[rows, half:] = xs_ref[rows, half:] + col_ref[...]


def _grid_pos_tables(n_tokens, d):
    quarter = d // 4
    omega = 1.0 / (POS_BASE ** (jnp.arange(quarter, dtype=F32) / quarter))
    r = jnp.arange(n_tokens // GRID_W, dtype=F32)[:, None] * omega
    cc = jnp.arange(GRID_W, dtype=F32)[:, None] * omega
    return (jnp.concatenate([jnp.sin(r), jnp.cos(r)], -1), jnp.concatenate([jnp.sin(cc), jnp.cos(cc)], -1))


def _embed(xp, xs, sample_len, tm=512):
    n_p, d = xp.shape
    n_s = xs.shape[0]
    npt = n_p // tm
    pos_tiles = sample_len // tm
    grid_rows = tm // GRID_W
    row_emb, col_emb = _grid_pos_tables(sample_len, d)
    return pl.pallas_call(
        functools.partial(_embed_kernel, n_prompt_tiles=npt),
        grid=((n_p + n_s) // tm,),
        in_specs=[
            pl.BlockSpec((tm, d), lambda i: (jnp.minimum(i, npt - 1), 0)),
            pl.BlockSpec((tm, d), lambda i: (jnp.maximum(i - npt, 0), 0)),
            pl.BlockSpec((grid_rows, d // 2), lambda i: (lax.rem(jnp.maximum(i - npt, 0), pos_tiles), 0)),
            pl.BlockSpec((GRID_W, d // 2), lambda i: (0, 0)),
        ],
        out_specs=pl.BlockSpec((tm, d), lambda i: (i, 0)),
        out_shape=jax.ShapeDtypeStruct((n_p + n_s, d), F32),
        compiler_params=pltpu.CompilerParams(
            dimension_semantics=("parallel",), vmem_limit_bytes=VMEM_LIMIT),
        name="embed",
    )(xp, xs, row_emb, col_emb)


def kernel(x_prompt, x_sample, state_hgrn, state_rwkv, c, c_ctx, ada_w, ada_b, ln_g, ln_b, ffn_w_up, ffn_w_down, hgrn_w_in, hgrn_lb, hgrn_norm_g, hgrn_w_o, rwkv_mu, rwkv_w_rkv, rwkv_w0, rwkv_w_la, rwkv_w_lb, rwkv_a0, rwkv_a_la, rwkv_a_lb, rwkv_g_la, rwkv_g_lb, rwkv_k_k, rwkv_k_a, rwkv_r_k, rwkv_gn_g, rwkv_gn_b, rwkv_w_o):
    n_b, l_p, d = x_prompt.shape
    n_s, l_s, _ = x_sample.shape
    depth = ada_w.shape[0]
    n_p_rows = n_b * l_p
    total = n_p_rows + n_s * l_s
    assert n_p_rows % l_s == 0 and l_p % LANES == 0 and l_s % LANES == 0
    alpha = (2 * depth) ** 0.25
    a_heads = d // LANES

    def seg(i, tm):
        return _seg_index(i, tm, n_p_rows, l_s)

    x = _embed(x_prompt.reshape(n_p_rows, d), x_sample.reshape(n_s * l_s, d), l_s)

    cond8 = jnp.zeros((8, d), F32).at[0].set(c_ctx).at[1:1 + n_s].set(c)
    mods = _adaln(cond8, ada_w, ada_b).reshape(depth, 8, 6, d)

    lb_soft = jax.nn.softmax(hgrn_lb.astype(F32), axis=0)
    lower_bounds = jnp.cumsum(lb_soft, axis=0) - lb_soft[0]

    new_hgrn = []
    new_rwkv = []
    for l in range(depth):
        j = l // 2
        if l % 2 == 0:
            proj = _modmm(x, mods[l], hgrn_w_in[j].astype(BF16), seg)
            o, s_ctx = _hgrn_rec(proj, lower_bounds[j], hgrn_norm_g[j], None, None, n_seq=n_b, seq_len=l_p,
                                 row_block0=0, n_heads=a_heads, total_rows=total)
            (o,) = _hgrn_rec(proj, lower_bounds[j], hgrn_norm_g[j], state_hgrn[:, j], o, n_seq=n_s, seq_len=l_s,
                             row_block0=n_p_rows // l_s, n_heads=a_heads, total_rows=total)
            new_hgrn.append(s_ctx)
            w_o = hgrn_w_o[j]
        else:
            prep = _rwkv_prepare_weights(rwkv_mu[j], rwkv_w_rkv[j], rwkv_w0[j], rwkv_w_la[j], rwkv_w_lb[j],
                                         rwkv_a0[j], rwkv_a_la[j], rwkv_a_lb[j], rwkv_g_la[j], rwkv_g_lb[j])
            seq_kw = dict(n_prompt_rows=n_p_rows, prompt_len=l_p, sample_len=l_s)
            rkv = _rwkv_rkv(x, mods[l], prep["mu3"], prep["w_rkv"], seg, **seq_kw)
            lw, a, g = _rwkv_lora(x, mods[l], rwkv_mu[j], prep["lora"], seg, **seq_kw)
            params = (rwkv_k_k[j], rwkv_k_a[j], rwkv_r_k[j], rwkv_gn_g[j], rwkv_gn_b[j])
            o, s_ctx = _rwkv_rec(rkv, lw, a, g, params, None, None, n_seq=n_b, seq_len=l_p,
                                 row_block0=0, total_rows=total)
            (o,) = _rwkv_rec(rkv, lw, a, g, params, _pair_states(state_rwkv[:, j]), o, n_seq=n_s, seq_len=l_s,
                             row_block0=n_p_rows // l_s, total_rows=total)
            new_rwkv.append(s_ctx)
            w_o = rwkv_w_o[j]
        x = _post_mixer(o, x, mods[l], w_o.astype(BF16), ffn_w_up[l].astype(BF16), ffn_w_down[l].astype(BF16),
                        ln_g[l], ln_b[l], seg, alpha)

    y_prompt = x[:n_p_rows].reshape(n_b, l_p, d)
    y_sample = x[n_p_rows:].reshape(n_s, l_s, d)
    return (y_prompt, y_sample, jnp.stack(new_hgrn, axis=1), jnp.stack(new_rwkv, axis=1))
```

```python
import functools

import jax
import jax.numpy as jnp
from jax import lax
from jax.experimental import pallas as pl
from jax.experimental.pallas import tpu as pltpu

F32 = jnp.float32
BF16 = jnp.bfloat16
HIGHEST = lax.Precision.HIGHEST

LN_EPS = 1e-5
RMS_EPS = 1e-6
GN_EPS = 64e-5
DECAY_SCALE = 0.606531
EXP_CLIP = 80.0
POS_BASE = 10000.0
GRID_W = 64

LANES = 128
HGRN_CHUNK = 32
HGRN_CHUNK_LOG2 = HGRN_CHUNK.bit_length() - 1
HGRN_SAFE_EXPONENT = 75.0
RWKV_HEAD = 64
VMEM_LIMIT = 56 * 1024 * 1024

NT_DIMS = (((1,), (1,)), ((), ()))
TN_DIMS = (((0,), (0,)), ((), ()))


def _dot(a, b, dims=None, precision=None):
    if dims is None:
        return jnp.dot(a, b, preferred_element_type=F32, precision=precision)
    return lax.dot_general(a, b, dims, preferred_element_type=F32, precision=precision)


def _bdot(a, b, dims=None):
    return _dot(a.astype(BF16), b.astype(BF16), dims)


def _hdot(a, b, dims=None):
    return _dot(a, b, dims, precision=HIGHEST)


def _split_bf16(x):
    hi = x.astype(BF16)
    return hi, (x - hi.astype(F32)).astype(BF16)


def _dot_x3(a, b, dims=None):
    ah, al = _split_bf16(a)
    bh, bl = _split_bf16(b)
    if dims == NT_DIMS:
        a_cat = jnp.concatenate([ah, al], axis=1)
        b_half = jnp.concatenate([bh, bl], axis=0)
        b_cat = jnp.concatenate([b_half, b_half], axis=1)
        n = b.shape[0]
    else:
        a_cat = jnp.concatenate([ah, al], axis=0 if dims == TN_DIMS else 1)
        b_half = jnp.concatenate([bh, bl], axis=1)
        b_cat = jnp.concatenate([b_half, b_half], axis=0)
        n = b.shape[1]
    r = _dot(a_cat, b_cat, dims)
    return r[:, :n] + r[:, n:]


def _dot_x2l(a, b):
    ah, al = _split_bf16(a)
    b = b.astype(BF16)
    return _dot(jnp.concatenate([ah, al], axis=1), jnp.concatenate([b, b], axis=0))


def _dot_x2r(a, b):
    bh, bl = _split_bf16(b)
    n = b.shape[1]
    r = _dot(a.astype(BF16), jnp.concatenate([bh, bl], axis=1))
    return r[:, :n] + r[:, n:]


def _sigmoid(x):
    return jax.nn.sigmoid(x)


def _layer_norm(x, g, b):
    mu = jnp.mean(x, -1, keepdims=True)
    xc = x - mu
    var = jnp.mean(xc * xc, -1, keepdims=True)
    return xc * lax.rsqrt(var + LN_EPS) * g + b


def _seg_index(i, tm, n_prompt_rows, sample_len):
    start = i * tm
    return jnp.where(start < n_prompt_rows, 0, 1 + (start - n_prompt_rows) // sample_len)


def _adaln_kernel(c_ref, w_ref, b_ref, o_ref):
    c = c_ref[...]
    s = c * _sigmoid(c)
    o_ref[...] = _bdot(s, w_ref[...]) + b_ref[...]


def _adaln(cond8, ada_w, ada_b, tn=1536):
    depth, d, n = ada_w.shape
    return pl.pallas_call(
        _adaln_kernel,
        grid=(depth, n // tn),
        in_specs=[
            pl.BlockSpec((8, d), lambda l, j: (0, 0)),
            pl.BlockSpec((None, d, tn), lambda l, j: (l, 0, j)),
            pl.BlockSpec((None, 1, tn), lambda l, j: (l, 0, j)),
        ],
        out_specs=pl.BlockSpec((None, 8, tn), lambda l, j: (l, 0, j)),
        out_shape=jax.ShapeDtypeStruct((depth, 8, n), F32),
        compiler_params=pltpu.CompilerParams(
            dimension_semantics=("parallel", "parallel"), vmem_limit_bytes=VMEM_LIMIT),
        name="adaln",
    )(cond8, ada_w, ada_b.reshape(depth, 1, n))


def _modmm_kernel(x_ref, mod_ref, w_ref, o_ref, xb_ref):
    j = pl.program_id(1)

    @pl.when(j == 0)
    def _():
        m = mod_ref[...]
        xb_ref[...] = (x_ref[...] * (1.0 + m[1:2]) + m[0:1]).astype(BF16)

    o_ref[...] = _dot(xb_ref[...], w_ref[j])


def _modmm(x, mods, w, seg, tm=1024, tn=1024):
    t, d = x.shape
    n = w.shape[1]
    panels = w.reshape(d, n // tn, tn).transpose(1, 0, 2)
    return pl.pallas_call(
        _modmm_kernel,
        grid=(t // tm, n // tn),
        in_specs=[
            pl.BlockSpec((tm, d), lambda i, j: (i, 0)),
            pl.BlockSpec((None, 6, d), lambda i, j: (seg(i, tm), 0, 0)),
            pl.BlockSpec((n // tn, d, tn), lambda i, j: (0, 0, 0)),
        ],
        out_specs=pl.BlockSpec((tm, tn), lambda i, j: (i, j)),
        out_shape=jax.ShapeDtypeStruct((t, n), F32),
        scratch_shapes=[pltpu.VMEM((tm, d), BF16)],
        compiler_params=pltpu.CompilerParams(
            dimension_semantics=("parallel", "arbitrary"), vmem_limit_bytes=VMEM_LIMIT),
        name="modmm",
    )(x, mods, panels)


def _post_mixer_kernel(o_ref, x_ref, mod_ref, wo_ref, wu_ref, wd_ref, g_ref, b_ref,
                       y_ref, x1_ref, xb_ref, acc_ref, *, alpha):
    f = pl.program_id(1)

    @pl.when(f == 0)
    def _():
        m = mod_ref[...]
        y = _dot(o_ref[...], wo_ref[...])
        x1 = _layer_norm(alpha * x_ref[...] + m[2:3] * y, g_ref[0:1], b_ref[0:1])
        x1_ref[...] = x1
        xb_ref[...] = (x1 * (1.0 + m[4:5]) + m[3:4]).astype(BF16)
        acc_ref[...] = jnp.zeros_like(acc_ref)

    h = jnp.maximum(_dot(xb_ref[...], wu_ref[...]), 0.0)
    acc_ref[...] += _dot((h * h).astype(BF16), wd_ref[...])

    @pl.when(f == pl.num_programs(1) - 1)
    def _():
        gate = mod_ref[...][5:6]
        y_ref[...] = _layer_norm(alpha * x1_ref[...] + gate * acc_ref[...], g_ref[1:2], b_ref[1:2])


def _post_mixer(o, x, mods, w_o, w_up, w_down, ln_g, ln_b, seg, alpha, tm=1024, tf=1024):
    t, d = x.shape
    dff = w_up.shape[1]
    both = pl.BlockSpec((2, d), lambda i, f: (0, 0))
    return pl.pallas_call(
        functools.partial(_post_mixer_kernel, alpha=alpha),
        grid=(t // tm, dff // tf),
        in_specs=[
            pl.BlockSpec((tm, d), lambda i, f: (i, 0)),
            pl.BlockSpec((tm, d), lambda i, f: (i, 0)),
            pl.BlockSpec((None, 6, d), lambda i, f: (seg(i, tm), 0, 0)),
            pl.BlockSpec((d, d), lambda i, f: (0, 0)),
            pl.BlockSpec((d, tf), lambda i, f: (0, f)),
            pl.BlockSpec((tf, d), lambda i, f: (f, 0)),
            both, both,
        ],
        out_specs=pl.BlockSpec((tm, d), lambda i, f: (i, 0)),
        out_shape=jax.ShapeDtypeStruct((t, d), F32),
        scratch_shapes=[pltpu.VMEM((tm, d), F32), pltpu.VMEM((tm, d), BF16), pltpu.VMEM((tm, d), F32)],
        compiler_params=pltpu.CompilerParams(
            dimension_semantics=("parallel", "arbitrary"), vmem_limit_bytes=VMEM_LIMIT),
        name="post_mixer",
    )(o, x, mods, w_o, w_up, w_down, ln_g, ln_b)


def _hgrn_precompute(entries, r_i, c_i):
    pos = r_i & (HGRN_CHUNK - 1)
    same_chunk = (r_i >> HGRN_CHUNK_LOG2) == (c_i >> HGRN_CHUNK_LOG2)
    blk = same_chunk.astype(F32)
    pre = []
    for qr, zz, v, lbz, fwd in entries:
        causal = same_chunk & ((c_i <= r_i) if fwd else (c_i >= r_i))
        q = qr * _sigmoid(qr)
        e_abs = jnp.exp(-jnp.abs(zz))
        inv_1p = 1.0 / (1.0 + e_abs)
        log_f = jnp.minimum(zz, 0.0) + jnp.log((1.0 + lbz * jnp.exp(jnp.minimum(-zz, EXP_CLIP))) * inv_1p)
        k = (1.0 - lbz) * jnp.where(zz > 0.0, e_abs, 1.0) * inv_1p
        sums = _dot_x2r(jnp.concatenate([causal.astype(F32), blk], axis=0), log_f)
        cum = sums[:LANES]
        ctot = sums[LANES:]
        pre.append(dict(q=q, k=k, v=v, cum=cum, ctot=ctot, fwd=fwd, causal=causal,
                        q_in=q * jnp.exp(cum), k_out=k * jnp.exp(ctot - cum), dec=jnp.exp(ctot)))

    def scores_factored():
        out = []
        for e in pre:
            half = 0.5 * e["ctot"]
            qk = _dot_x3(e["q"] * jnp.exp(e["cum"] - half), e["k"] * jnp.exp(half - e["cum"]), NT_DIMS)
            out.append(jnp.where(e["causal"], qk, 0.0))
        return out

    def scores_pairwise():
        out = []
        for e in pre:
            fwd, q, k, cum = e["fwd"], e["q"], e["k"], e["cum"]
            scores = jnp.zeros((LANES, LANES), F32)
            for d in range(HGRN_CHUNK):
                if d == 0:
                    kr, cr = k, cum
                else:
                    sh = d if fwd else LANES - d
                    kr = pltpu.roll(k, sh, 0)
                    cr = pltpu.roll(cum, sh, 0)
                valid = (pos >= d) if fwd else (pos <= HGRN_CHUNK - 1 - d)
                ex = jnp.exp(jnp.where(valid, cum - cr, 0.0))
                s = jnp.sum(q * kr * ex, axis=-1, keepdims=True)
                tgt = (c_i == r_i - d) if fwd else (c_i == r_i + d)
                scores = scores + jnp.where(tgt & valid, s, 0.0)
            out.append(scores)
        return out

    lowest = pre[0]["ctot"]
    for e in pre[1:]:
        lowest = jnp.minimum(lowest, e["ctot"])
    scores = lax.cond(jnp.min(lowest) >= -2.0 * HGRN_SAFE_EXPONENT, scores_factored, scores_pairwise)

    chunk_of_lane = c_i >> HGRN_CHUNK_LOG2
    out = []
    for e, sc in zip(pre, scores):
        o_intra = _bdot(sc, e["v"])
        v_t = e["v"].T
        lhs = jnp.concatenate(
            [jnp.where(chunk_of_lane == c, v_t, 0.0) for c in range(LANES // HGRN_CHUNK)], axis=0)
        upd = _bdot(lhs, e["k_out"])
        out.append((e["q_in"], o_intra, upd, e["dec"]))
    return out


def _hgrn_rec_kernel(*refs, seq_len, groups, zero_init, want_state, aliased):
    qf_ref, zf_ref, vf_ref, qb_ref, zb_ref, vb_ref, gate_ref, lb_ref, ng_ref = refs[:9]
    pos = 9
    s0_ref = None
    if not zero_init:
        s0_ref = refs[pos]
        pos += 1
    if aliased:
        pos += 1
    o_ref = refs[pos]
    pos += 1
    sfin_ref = None
    if want_state:
        sfin_ref = refs[pos]
        pos += 1
    osum_ref, st_ref = refs[pos:pos + 2]

    n_tiles = seq_len // LANES
    n_chunks = LANES // HGRN_CHUNK
    unroll = min(8 // (2 * groups), n_tiles)
    assert n_tiles % unroll == 0 and (n_tiles == unroll or (n_tiles // 2) % unroll == 0)
    r_i = lax.broadcasted_iota(jnp.int32, (LANES, LANES), 0)
    c_i = lax.broadcasted_iota(jnp.int32, (LANES, LANES), 1)
    lb = lb_ref[...]
    data = ((qf_ref, zf_ref, vf_ref), (qb_ref, zb_ref, vb_ref))

    def lanes_of(grp):
        return slice(grp * LANES, (grp + 1) * LANES)

    for z in range(2):
        for grp in range(groups):
            if zero_init:
                st_ref[z * groups + grp] = jnp.zeros((LANES, LANES), F32)
            else:
                st_ref[z * groups + grp] = s0_ref[z, grp].T

    def finish(rows, cols, o):
        o = o * lax.rsqrt(jnp.mean(o * o, -1, keepdims=True) + RMS_EPS)
        g = gate_ref[rows, cols]
        o_ref[rows, cols] = (o * ng_ref[:, cols] * (g * _sigmoid(g))).astype(o_ref.dtype)

    def block(i, visit):
        slots = []
        for z in range(2):
            for grp in range(groups):
                for u in range(unroll):
                    t = i * unroll + u
                    slots.append((z, u, grp, t if z == 0 else n_tiles - 1 - t))
        entries = []
        for z, _, grp, t in slots:
            rows = pl.ds(pl.multiple_of(t * LANES, LANES), LANES)
            cols = lanes_of(grp)
            q_ref, z_ref, v_ref = data[z]
            entries.append((q_ref[rows, cols], z_ref[rows, cols], v_ref[rows, cols], lb[z:z + 1, cols], z == 0))
        pre = dict(zip([s[:3] for s in slots], _hgrn_precompute(entries, r_i, c_i)))

        st = [st_ref[c] for c in range(2 * groups)]
        inter = {key: [None] * n_chunks for key in pre}
        for u in range(unroll):
            for step in range(n_chunks):
                for z in range(2):
                    for grp in range(groups):
                        q_in, _, upd, dec = pre[(z, u, grp)]
                        ch = z * groups + grp
                        c = step if z == 0 else n_chunks - 1 - step
                        lo = c * HGRN_CHUNK
                        inter[(z, u, grp)][c] = _bdot(q_in[lo:lo + HGRN_CHUNK], st[ch], NT_DIMS)
                        st[ch] = st[ch] * dec[lo:lo + 1] + upd[c * LANES:(c + 1) * LANES]
        for ch in range(2 * groups):
            st_ref[ch] = st[ch]
        outs = {key: pre[key][1] + jnp.concatenate(inter[key], axis=0) for key in pre}

        if visit == "both":
            for t in range(n_tiles):
                for grp in range(groups):
                    finish(pl.ds(t * LANES, LANES), lanes_of(grp),
                           outs[(0, t, grp)] + outs[(1, n_tiles - 1 - t, grp)])
            return
        for z, u, grp, t in slots:
            rows, cols = pl.ds(pl.multiple_of(t * LANES, LANES), LANES), lanes_of(grp)
            if visit == "first":
                osum_ref[rows, cols] = outs[(z, u, grp)]
            else:
                finish(rows, cols, osum_ref[rows, cols] + outs[(z, u, grp)])

    if n_tiles == unroll:
        block(0, "both")
    else:
        def body(i, carry, visit):
            block(i, visit)
            return carry

        half = n_tiles // 2 // unroll
        lax.fori_loop(0, half, functools.partial(body, visit="first"), 0)
        lax.fori_loop(half, 2 * half, functools.partial(body, visit="second"), 0)
    if want_state:
        for z in range(2):
            for grp in range(groups):
                sfin_ref[z, grp] = st_ref[z * groups + grp].T


def _hgrn_rec(proj, lb, norm_g, s0, prev_out, *, n_seq, seq_len, row_block0, n_heads, total_rows):
    d = n_heads * LANES
    groups = 2 if seq_len // LANES <= 2 else 1
    n_cols = n_heads // groups
    zero_init = s0 is None
    want_state = s0 is None
    aliased = prev_out is not None

    def col(block):
        return lambda b, h: (row_block0 + b, block * n_cols + h)

    blk = (seq_len, groups * LANES)
    in_specs = [pl.BlockSpec(blk, col(0)), pl.BlockSpec(blk, col(1)), pl.BlockSpec(blk, col(2)),
                pl.BlockSpec(blk, col(3)), pl.BlockSpec(blk, col(4)), pl.BlockSpec(blk, col(5)),
                pl.BlockSpec(blk, col(6)),
                pl.BlockSpec((2, groups * LANES), lambda b, h: (0, h)),
                pl.BlockSpec((1, groups * LANES), lambda b, h: (0, h))]
    args = [proj] * 7 + [lb, norm_g.reshape(1, d)]
    state_spec = pl.BlockSpec((None, 2, groups, LANES, LANES), lambda b, h: (b, 0, h, 0, 0))
    if not zero_init:
        in_specs.append(state_spec)
        args.append(s0)
    io_alias = {}
    if aliased:
        in_specs.append(pl.BlockSpec(memory_space=pl.ANY))
        io_alias = {len(args): 0}
        args.append(prev_out)
    out_specs = [pl.BlockSpec(blk, lambda b, h: (row_block0 + b, h))]
    out_shape = [jax.ShapeDtypeStruct((total_rows, d), BF16)]
    if want_state:
        out_specs.append(state_spec)
        out_shape.append(jax.ShapeDtypeStruct((n_seq, 2, n_heads, LANES, LANES), F32))
    res = pl.pallas_call(
        functools.partial(_hgrn_rec_kernel, seq_len=seq_len, groups=groups, zero_init=zero_init,
                          want_state=want_state, aliased=aliased),
        grid=(n_seq, n_cols),
        in_specs=in_specs,
        out_specs=out_specs,
        out_shape=out_shape,
        scratch_shapes=[pltpu.VMEM((seq_len, groups * LANES), F32),
                        pltpu.VMEM((2 * groups, LANES, LANES), F32)],
        input_output_aliases=io_alias,
        compiler_params=pltpu.CompilerParams(
            dimension_semantics=("parallel", "parallel"), vmem_limit_bytes=VMEM_LIMIT),
        name="hgrn_rec",
    )(*args)
    return res


def _int_mod(x, n):
    return x & (n - 1) if n & (n - 1) == 0 else lax.rem(x, n)


def _token_shift(x_ref, xp_ref, xn_ref, mod_ref, tile_start, n_prompt_rows, prompt_len, sample_len):
    m = mod_ref[...]
    sh, sc = m[0:1], 1.0 + m[1:2]
    h = x_ref[...] * sc + sh
    h_before = xp_ref[7:8, :] * sc + sh
    h_after = xn_ref[0:1, :] * sc + sh
    tm = h.shape[0]
    rr = lax.broadcasted_iota(jnp.int32, (tm, 1), 0)
    grow = tile_start + rr
    in_prompt = grow < n_prompt_rows
    pos = jnp.where(in_prompt, _int_mod(grow, prompt_len), _int_mod(grow - n_prompt_rows, sample_len))
    last = jnp.where(in_prompt, prompt_len - 1, sample_len - 1)
    prev = jnp.where(rr == 0, h_before, pltpu.roll(h, 1, 0))
    prev = jnp.where(pos == 0, 0.0, prev)
    nxt = jnp.where(rr == tm - 1, h_after, pltpu.roll(h, tm - 1, 0))
    nxt = jnp.where(pos == last, 0.0, nxt)
    return h, 0.5 * (prev + nxt) - h


def _rwkv_rkv_kernel(x_ref, xp_ref, xn_ref, mod_ref, mu_ref, w_ref, o_ref, h_ref, xx_ref, *, tm, seq_info):
    @pl.when(pl.program_id(1) == 0)
    def _():
        h, xx = _token_shift(x_ref, xp_ref, xn_ref, mod_ref, pl.program_id(0) * tm, *seq_info)
        h_ref[...] = h
        xx_ref[...] = xx

    xs = h_ref[...] + xx_ref[...] * mu_ref[...]
    o_ref[...] = _dot(xs.astype(BF16), w_ref[pl.program_id(1)])


def _halo_specs(tm, d, t, n_grid_axes):
    nb = t // 8

    def before(i, *_):
        return (jnp.maximum(i * (tm // 8) - 1, 0), 0)

    def after(i, *_):
        return (jnp.minimum((i + 1) * (tm // 8), nb - 1), 0)

    del n_grid_axes
    return pl.BlockSpec((8, d), before), pl.BlockSpec((8, d), after)


def _rwkv_rkv(x, mods, mu3, w_rkv, seg, *, n_prompt_rows, prompt_len, sample_len, tm=512):
    t, d = x.shape
    n = w_rkv.shape[2]
    before, after = _halo_specs(tm, d, t, 2)
    return pl.pallas_call(
        functools.partial(_rwkv_rkv_kernel, tm=tm, seq_info=(n_prompt_rows, prompt_len, sample_len)),
        grid=(t // tm, 3),
        in_specs=[
            pl.BlockSpec((tm, d), lambda i, j: (i, 0)),
            before, after,
            pl.BlockSpec((None, 6, d), lambda i, j: (seg(i, tm), 0, 0)),
            pl.BlockSpec((None, 1, d), lambda i, j: (j, 0, 0)),
            pl.BlockSpec((3, d, n), lambda i, j: (0, 0, 0)),
        ],
        out_specs=pl.BlockSpec((None, tm, n), lambda i, j: (j, i, 0)),
        out_shape=jax.ShapeDtypeStruct((3, t, n), F32),
        scratch_shapes=[pltpu.VMEM((tm, d), F32), pltpu.VMEM((tm, d), F32)],
        compiler_params=pltpu.CompilerParams(
            dimension_semantics=("parallel", "arbitrary"), vmem_limit_bytes=VMEM_LIMIT),
        name="rwkv_rkv",
    )(x, x, x, mods, mu3, w_rkv)


def _rwkv_lora_kernel(x_ref, xp_ref, xn_ref, mod_ref, mu_ref, wla_ref, wlb_ref, w0_ref, ala_ref, alb_ref,
                      a0_ref, gla_ref, glb_ref, lw_ref, a_ref, g_ref, *, tm, seq_info):
    h, xx = _token_shift(x_ref, xp_ref, xn_ref, mod_ref, pl.program_id(0) * tm, *seq_info)
    mu = mu_ref[...]
    xs_w = (h + xx * mu[1:2]).astype(BF16)
    xs_a = (h + xx * mu[4:5]).astype(BF16)
    xs_g = (h + xx * mu[5:6]).astype(BF16)
    zw = w0_ref[...] + _bdot(jnp.tanh(_dot(xs_w, wla_ref[...])), wlb_ref[...])
    lw_ref[...] = -DECAY_SCALE * _sigmoid(zw)
    za = a0_ref[...] + _bdot(_dot(xs_a, ala_ref[...]), alb_ref[...])
    a_ref[...] = _sigmoid(za)
    g_ref[...] = _bdot(_sigmoid(_dot(xs_g, gla_ref[...])), glb_ref[...])


def _rwkv_lora(x, mods, mu, weights, seg, *, n_prompt_rows, prompt_len, sample_len, tm=256):
    t, d = x.shape
    before, after = _halo_specs(tm, d, t, 1)

    def whole(arr):
        return pl.BlockSpec(arr.shape, lambda i: (0,) * arr.ndim)

    return pl.pallas_call(
        functools.partial(_rwkv_lora_kernel, tm=tm, seq_info=(n_prompt_rows, prompt_len, sample_len)),
        grid=(t // tm,),
        in_specs=[pl.BlockSpec((tm, d), lambda i: (i, 0)), before, after,
                  pl.BlockSpec((None, 6, d), lambda i: (seg(i, tm), 0, 0)), whole(mu)]
                 + [whole(w) for w in weights],
        out_specs=[pl.BlockSpec((tm, 2 * d), lambda i: (i, 0)),
                   pl.BlockSpec((tm, 2 * d), lambda i: (i, 0)),
                   pl.BlockSpec((tm, d), lambda i: (i, 0))],
        out_shape=[jax.ShapeDtypeStruct((t, 2 * d), F32), jax.ShapeDtypeStruct((t, 2 * d), F32),
                   jax.ShapeDtypeStruct((t, d), F32)],
        compiler_params=pltpu.CompilerParams(
            dimension_semantics=("parallel",), vmem_limit_bytes=VMEM_LIMIT),
        name="rwkv_lora",
    )(x, x, x, mods, mu, *weights)


def _rwkv_precompute(entries, consts):
    bd, eye, head_masks, r_i, c_i = consts
    n = len(entries)
    pre = []
    for r, k, v, lw, a, (kkp, kap, rkp), fwd in entries:
        tri = ((c_i <= r_i) if fwd else (c_i >= r_i)).astype(F32)
        kk = k * kkp
        kk = kk / jnp.maximum(jnp.sqrt(_dot_x2l(kk * kk, bd)), 1e-12)
        k2 = k * (1.0 + (a - 1.0) * kap)
        bonus = _dot_x2l(r * k2 * rkp, bd) * v
        b = kk * a
        cl = _dot_x2r(tri, lw)
        cle = cl - lw
        cm = cl[LANES // 2:LANES // 2 + 1]
        ct = cl[LANES - 1:LANES] if fwd else cl[0:1]
        e_inv = jnp.exp(cm - cl)
        e_out = jnp.exp(ct - cl)
        pre.append(dict(
            v=v, bonus=bonus, fwd=fwd,
            kkt=kk * jnp.exp(cle - cm), rt=r * jnp.exp(cl - cm), kh=k2 * e_inv, bh=b * e_inv,
            kkd=kk * jnp.exp(cle), rd=r * jnp.exp(cl), kg=k2 * e_out, bg=b * e_out, gc=jnp.exp(ct)))

    for e in pre:
        lhs = jnp.concatenate([e["kkt"] * head_masks[0], e["kkt"] * head_masks[1],
                               e["rt"] * head_masks[0], e["rt"] * head_masks[1]], axis=0)
        e["gk"] = _dot_x3(lhs, e["kh"], NT_DIMS)
        e["gb"] = _dot_x3(lhs, e["bh"], NT_DIMS)

    chains = []
    for e in pre:
        strict = (c_i < r_i) if e["fwd"] else (c_i > r_i)
        incl = (c_i <= r_i) if e["fwd"] else (c_i >= r_i)
        for hd in range(2):
            lo = hd * LANES
            chains.append(dict(
                e=e, hd=hd,
                a_k=jnp.where(strict, e["gk"][lo:lo + LANES], 0.0),
                a_b=jnp.where(strict, e["gb"][lo:lo + LANES], 0.0),
                b_k=jnp.where(incl, e["gk"][2 * LANES + lo:3 * LANES + lo], 0.0),
                b_b=jnp.where(incl, e["gb"][2 * LANES + lo:3 * LANES + lo], 0.0)))

    for c in chains:
        c["m"] = eye - jnp.where((r_i >> 1) == (c_i >> 1), c["a_b"], 0.0)
    for log_k in range(1, 7):
        k = 1 << log_k
        off = ((r_i >> (log_k + 1)) == (c_i >> (log_k + 1))) & ((r_i >> log_k) != (c_i >> log_k))
        if k < 8:
            for c in chains:
                c["t"] = _dot_x3(jnp.where(off, c["a_b"], 0.0), c["m"])
            for c in chains:
                c["m"] = c["m"] - _dot_x3(c["m"], c["t"])
            continue

        def take(x, fwd):
            first = k if fwd else 0
            return jnp.concatenate([x[lo:lo + k] for lo in range(first, LANES, 2 * k)], axis=0)

        def spread(rows, fwd):
            zero = jnp.zeros((k, LANES), F32)
            parts = []
            for j in range(LANES // (2 * k)):
                piece = rows[j * k:(j + 1) * k]
                parts += [zero, piece] if fwd else [piece, zero]
            return jnp.concatenate(parts, axis=0)

        for c in chains:
            fwd = c["e"]["fwd"]
            c["t"] = spread(_dot_x3(take(jnp.where(off, c["a_b"], 0.0), fwd), c["m"]), fwd)
        for c in chains:
            fwd = c["e"]["fwd"]
            c["m"] = c["m"] - spread(_dot_x3(take(c["m"], fwd), c["t"]), fwd)

    for c in chains:
        e = c["e"]
        c["v_h"] = e["v"] * head_masks[c["hd"]]
        c["kkd_h"] = e["kkd"] * head_masks[c["hd"]]
        c["akv"] = _dot_x3(c["a_k"], c["v_h"])
    for c in chains:
        c["wt"] = _dot_x3(c["m"], c["kkd_h"])
        c["u0"] = _dot_x3(c["m"], c["akv"])
    for c in chains:
        c["y0"] = _dot_x3(c["b_k"], c["v_h"]) - _dot_x3(c["b_b"], c["u0"])
        c["rp"] = _dot_x3(c["b_b"], c["wt"])

    out = []
    for i, e in enumerate(pre):
        c0, c1 = chains[2 * i], chains[2 * i + 1]
        wt = c0["wt"] + c1["wt"]
        u0 = c0["u0"] + c1["u0"]
        q0 = bd * (_dot_x3(e["v"], e["kg"], TN_DIMS) - _dot_x3(u0.T, e["bg"]))
        p_mat = _dot_x3(wt, e["bg"], TN_DIMS)
        out.append((e["rd"] - c0["rp"] - c1["rp"], c0["y0"] + c1["y0"], e["gc"], q0, p_mat, e["bonus"]))
    assert len(out) == n
    return out


def _rwkv_rec_kernel(*refs, seq_len, groups, zero_init, want_state, aliased):
    data = (refs[0:5], refs[5:10])
    g_ref, kk_ref, ka_ref, rk_ref, gg_ref, gb_ref = refs[10:16]
    pos = 16
    s0_ref = None
    if not zero_init:
        s0_ref = refs[pos]
        pos += 1
    if aliased:
        pos += 1
    o_ref = refs[pos]
    pos += 1
    sfin_ref = None
    if want_state:
        sfin_ref = refs[pos]
        pos += 1
    osum_ref, st_ref = refs[pos:pos + 2]

    n_tiles = seq_len // LANES
    unroll = min(8 // (2 * groups), n_tiles)
    assert n_tiles % unroll == 0 and (n_tiles == unroll or (n_tiles // 2) % unroll == 0)
    r_i = lax.broadcasted_iota(jnp.int32, (LANES, LANES), 0)
    c_i = lax.broadcasted_iota(jnp.int32, (LANES, LANES), 1)
    bd = ((r_i >> 6) == (c_i >> 6)).astype(F32)
    eye = (r_i == c_i).astype(F32)
    lane = lax.broadcasted_iota(jnp.int32, (1, LANES), 1)
    head_masks = ((lane < RWKV_HEAD).astype(F32), (lane >= RWKV_HEAD).astype(F32))
    consts = (bd, eye, head_masks, r_i, c_i)
    inv_n = 1.0 / RWKV_HEAD

    def lanes_of(grp):
        return slice(grp * LANES, (grp + 1) * LANES)

    for z in range(2):
        for grp in range(groups):
            if zero_init:
                st_ref[z * groups + grp] = jnp.zeros((LANES, LANES), F32)
            else:
                zero = jnp.zeros((RWKV_HEAD, RWKV_HEAD), F32)
                st_ref[z * groups + grp] = jnp.concatenate(
                    [jnp.concatenate([s0_ref[z, 2 * grp], zero], axis=1),
                     jnp.concatenate([zero, s0_ref[z, 2 * grp + 1]], axis=1)], axis=0)

    def block(i, visit):
        slots = []
        for z in range(2):
            for grp in range(groups):
                for u in range(unroll):
                    t = i * unroll + u
                    slots.append((z, u, grp, t if z == 0 else n_tiles - 1 - t))
        entries = []
        for z, _, grp, t in slots:
            rows = pl.ds(pl.multiple_of(t * LANES, LANES), LANES)
            cols = lanes_of(grp)
            r_ref, k_ref, v_ref, lw_ref, a_ref = data[z]
            params = (kk_ref[z:z + 1, cols], ka_ref[z:z + 1, cols], rk_ref[z:z + 1, cols])
            entries.append((r_ref[rows, cols], k_ref[rows, cols], v_ref[rows, cols], lw_ref[rows, cols],
                            a_ref[rows, cols], params, z == 0))
        pre = _rwkv_precompute(entries, consts)

        ys = []
        st = [st_ref[c] for c in range(2 * groups)]
        for (z, _, grp, _), (rp, y0, gc, q0, p_mat, _) in zip(slots, pre):
            c = z * groups + grp
            ys.append(_dot_x3(rp, st[c], NT_DIMS) + y0)
            st[c] = st[c] * gc + q0 - bd * _dot_x3(st[c], p_mat)
        for c in range(2 * groups):
            st_ref[c] = st[c]

        outs = {}
        for (z, u, grp, _), y, (_, _, _, _, _, bonus) in zip(slots, ys, pre):
            cols = lanes_of(grp)
            mean = _dot_x2l(y, bd) * inv_n
            yc = y - mean
            var = _dot_x2l(yc * yc, bd) * inv_n
            outs[(z, u, grp)] = (yc * lax.rsqrt(var + GN_EPS) * gg_ref[z:z + 1, cols] + gb_ref[z:z + 1, cols]
                                 + bonus)

        if visit == "both":
            for t in range(n_tiles):
                for grp in range(groups):
                    rows, cols = pl.ds(t * LANES, LANES), lanes_of(grp)
                    total = outs[(0, t, grp)] + outs[(1, n_tiles - 1 - t, grp)]
                    o_ref[rows, cols] = (total * g_ref[rows, cols]).astype(o_ref.dtype)
            return
        for z, u, grp, t in slots:
            out = outs[(z, u, grp)]
            rows, cols = pl.ds(pl.multiple_of(t * LANES, LANES), LANES), lanes_of(grp)
            if visit == "first":
                osum_ref[rows, cols] = out
            else:
                o_ref[rows, cols] = ((osum_ref[rows, cols] + out) * g_ref[rows, cols]).astype(o_ref.dtype)

    if n_tiles == unroll:
        block(0, "both")
    else:
        def body(i, carry, visit):
            block(i, visit)
            return carry

        half = n_tiles // 2 // unroll
        lax.fori_loop(0, half, functools.partial(body, visit="first"), 0)
        lax.fori_loop(half, 2 * half, functools.partial(body, visit="second"), 0)
    if want_state:
        for z in range(2):
            for grp in range(groups):
                st = st_ref[z * groups + grp]
                sfin_ref[z, 2 * grp] = st[:RWKV_HEAD, :RWKV_HEAD]
                sfin_ref[z, 2 * grp + 1] = pltpu.roll(st, RWKV_HEAD, 1)[RWKV_HEAD:, :RWKV_HEAD]


def _rwkv_rec(rkv, lw, a, g, params, s0, prev_out, *, n_seq, seq_len, row_block0, total_rows):
    d = g.shape[1]
    groups = 2 if seq_len // LANES <= 2 else 1
    n_cols = d // (groups * LANES)
    zero_init = s0 is None
    want_state = s0 is None
    aliased = prev_out is not None
    blk = (seq_len, groups * LANES)

    def dir_specs(z):
        col = lambda b, p: (row_block0 + b, z * n_cols + p)
        return ([pl.BlockSpec((None,) + blk, lambda b, p, j=j: (j, row_block0 + b, z * n_cols + p))
                 for j in range(3)] + [pl.BlockSpec(blk, col), pl.BlockSpec(blk, col)])

    in_specs = dir_specs(0) + dir_specs(1)
    in_specs.append(pl.BlockSpec(blk, lambda b, p: (row_block0 + b, p)))
    in_specs += [pl.BlockSpec((2, groups * LANES), lambda b, p: (0, p)) for _ in range(5)]
    args = [rkv, rkv, rkv, lw, a] * 2 + [g] + list(params)
    if not zero_init:
        in_specs.append(pl.BlockSpec((None, 2, 2 * groups, RWKV_HEAD, RWKV_HEAD), lambda b, p: (b, 0, p, 0, 0)))
        args.append(s0)
    io_alias = {}
    if aliased:
        in_specs.append(pl.BlockSpec(memory_space=pl.ANY))
        io_alias = {len(args): 0}
        args.append(prev_out)
    out_specs = [pl.BlockSpec(blk, lambda b, p: (row_block0 + b, p))]
    out_shape = [jax.ShapeDtypeStruct((total_rows, d), BF16)]
    if want_state:
        out_specs.append(pl.BlockSpec((None, 2, 2 * groups, RWKV_HEAD, RWKV_HEAD), lambda b, p: (b, 0, p, 0, 0)))
        out_shape.append(jax.ShapeDtypeStruct((n_seq, 2, d // RWKV_HEAD, RWKV_HEAD, RWKV_HEAD), F32))
    return pl.pallas_call(
        functools.partial(_rwkv_rec_kernel, seq_len=seq_len, groups=groups, zero_init=zero_init,
                          want_state=want_state, aliased=aliased),
        grid=(n_seq, n_cols),
        in_specs=in_specs,
        out_specs=out_specs,
        out_shape=out_shape,
        scratch_shapes=[pltpu.VMEM((seq_len, groups * LANES), F32),
                        pltpu.VMEM((2 * groups, LANES, LANES), F32)],
        input_output_aliases=io_alias,
        compiler_params=pltpu.CompilerParams(
            dimension_semantics=("parallel", "parallel"), vmem_limit_bytes=VMEM_LIMIT),
        name="rwkv_rec",
    )(*args)


def _rwkv_prepare_weights(mu, w_rkv, w0, w_la, w_lb, a0, a_la, a_lb, g_la, g_lb):
    d = mu.shape[1]
    rank_w = w_la.shape[2]
    rank_a = a_la.shape[2]
    rank_g = g_la.shape[1]
    rank_g_pad = -(-rank_g // LANES) * LANES

    def block_diag(w):
        rank = w.shape[1]
        out = jnp.zeros((2, rank, 2, d), w.dtype)
        out = out.at[0, :, 0, :].set(w[0]).at[1, :, 1, :].set(w[1])
        return out.reshape(2 * rank, 2 * d)

    lora = (
        w_la.reshape(d, 2 * rank_w).astype(BF16), block_diag(w_lb).astype(BF16), w0.reshape(1, 2 * d),
        a_la.reshape(d, 2 * rank_a).astype(BF16), block_diag(a_lb).astype(BF16), a0.reshape(1, 2 * d),
        jnp.pad(g_la, ((0, 0), (0, rank_g_pad - rank_g))).astype(BF16),
        jnp.pad(g_lb, ((0, rank_g_pad - rank_g), (0, 0))).astype(BF16),
    )
    mu3 = jnp.stack([mu[0], mu[2], mu[3]]).reshape(3, 1, d)
    return {"mu3": mu3, "w_rkv": w_rkv.astype(BF16), "lora": lora}


def _embed_kernel(xp_ref, xs_ref, row_ref, col_ref, o_ref, *, n_prompt_tiles):
    i = pl.program_id(0)

    @pl.when(i < n_prompt_tiles)
    def _():
        o_ref[...] = xp_ref[...]

    @pl.when(i >= n_prompt_tiles)
    def _():
        half = o_ref.shape[1] // 2
        for grp in range(o_ref.shape[0] // GRID_W):
            rows = pl.ds(grp * GRID_W, GRID_W)
            o_ref[rows, :half] = xs_ref[rows, :half] + row_ref[grp:grp + 1, :]
            o_ref[rows, half:] = xs_ref[rows, half:] + col_ref[...]


def _grid_pos_tables(n_tokens, d):
    quarter = d // 4
    omega = 1.0 / (POS_BASE ** (jnp.arange(quarter, dtype=F32) / quarter))
    r = jnp.arange(n_tokens // GRID_W, dtype=F32)[:, None] * omega
    cc = jnp.arange(GRID_W, dtype=F32)[:, None] * omega
    return (jnp.concatenate([jnp.sin(r), jnp.cos(r)], -1), jnp.concatenate([jnp.sin(cc), jnp.cos(cc)], -1))


def _embed(xp, xs, sample_len, tm=512):
    n_p, d = xp.shape
    n_s = xs.shape[0]
    npt = n_p // tm
    pos_tiles = sample_len // tm
    grid_rows = tm // GRID_W
    row_emb, col_emb = _grid_pos_tables(sample_len, d)
    return pl.pallas_call(
        functools.partial(_embed_kernel, n_prompt_tiles=npt),
        grid=((n_p + n_s) // tm,),
        in_specs=[
            pl.BlockSpec((tm, d), lambda i: (jnp.minimum(i, npt - 1), 0)),
            pl.BlockSpec((tm, d), lambda i: (jnp.maximum(i - npt, 0), 0)),
            pl.BlockSpec((grid_rows, d // 2), lambda i: (lax.rem(jnp.maximum(i - npt, 0), pos_tiles), 0)),
            pl.BlockSpec((GRID_W, d // 2), lambda i: (0, 0)),
        ],
        out_specs=pl.BlockSpec((tm, d), lambda i: (i, 0)),
        out_shape=jax.ShapeDtypeStruct((n_p + n_s, d), F32),
        compiler_params=pltpu.CompilerParams(
            dimension_semantics=("parallel",), vmem_limit_bytes=VMEM_LIMIT),
        name="embed",
    )(xp, xs, row_emb, col_emb)


def kernel(x_prompt, x_sample, state_hgrn, state_rwkv, c, c_ctx, ada_w, ada_b, ln_g, ln_b, ffn_w_up, ffn_w_down, hgrn_w_in, hgrn_lb, hgrn_norm_g, hgrn_w_o, rwkv_mu, rwkv_w_rkv, rwkv_w0, rwkv_w_la, rwkv_w_lb, rwkv_a0, rwkv_a_la, rwkv_a_lb, rwkv_g_la, rwkv_g_lb, rwkv_k_k, rwkv_k_a, rwkv_r_k, rwkv_gn_g, rwkv_gn_b, rwkv_w_o):
    n_b, l_p, d = x_prompt.shape
    n_s, l_s, _ = x_sample.shape
    depth = ada_w.shape[0]
    n_p_rows = n_b * l_p
    total = n_p_rows + n_s * l_s
    assert n_p_rows % l_s == 0 and l_p % LANES == 0 and l_s % LANES == 0
    alpha = (2 * depth) ** 0.25
    a_heads = d // LANES

    def seg(i, tm):
        return _seg_index(i, tm, n_p_rows, l_s)

    x = _embed(x_prompt.reshape(n_p_rows, d), x_sample.reshape(n_s * l_s, d), l_s)

    cond8 = jnp.zeros((8, d), F32).at[0].set(c_ctx).at[1:1 + n_s].set(c)
    mods = _adaln(cond8, ada_w, ada_b).reshape(depth, 8, 6, d)

    lb_soft = jax.nn.softmax(hgrn_lb.astype(F32), axis=0)
    lower_bounds = jnp.cumsum(lb_soft, axis=0) - lb_soft[0]

    new_hgrn = []
    new_rwkv = []
    for l in range(depth):
        j = l // 2
        if l % 2 == 0:
            proj = _modmm(x, mods[l], hgrn_w_in[j].astype(BF16), seg)
            o, s_ctx = _hgrn_rec(proj, lower_bounds[j], hgrn_norm_g[j], None, None, n_seq=n_b, seq_len=l_p,
                                 row_block0=0, n_heads=a_heads, total_rows=total)
            (o,) = _hgrn_rec(proj, lower_bounds[j], hgrn_norm_g[j], state_hgrn[:, j], o, n_seq=n_s, seq_len=l_s,
                             row_block0=n_p_rows // l_s, n_heads=a_heads, total_rows=total)
            new_hgrn.append(s_ctx)
            w_o = hgrn_w_o[j]
        else:
            prep = _rwkv_prepare_weights(rwkv_mu[j], rwkv_w_rkv[j], rwkv_w0[j], rwkv_w_la[j], rwkv_w_lb[j],
                                         rwkv_a0[j], rwkv_a_la[j], rwkv_a_lb[j], rwkv_g_la[j], rwkv_g_lb[j])
            seq_kw = dict(n_prompt_rows=n_p_rows, prompt_len=l_p, sample_len=l_s)
            rkv = _rwkv_rkv(x, mods[l], prep["mu3"], prep["w_rkv"], seg, **seq_kw)
            lw, a, g = _rwkv_lora(x, mods[l], rwkv_mu[j], prep["lora"], seg, **seq_kw)
            params = (rwkv_k_k[j], rwkv_k_a[j], rwkv_r_k[j], rwkv_gn_g[j], rwkv_gn_b[j])
            o, s_ctx = _rwkv_rec(rkv, lw, a, g, params, None, None, n_seq=n_b, seq_len=l_p,
                                 row_block0=0, total_rows=total)
            (o,) = _rwkv_rec(rkv, lw, a, g, params, state_rwkv[:, j], o, n_seq=n_s, seq_len=l_s,
                             row_block0=n_p_rows // l_s, total_rows=total)
            new_rwkv.append(s_ctx)
            w_o = rwkv_w_o[j]
        x = _post_mixer(o, x, mods[l], w_o.astype(BF16), ffn_w_up[l].astype(BF16), ffn_w_down[l].astype(BF16),
                        ln_g[l], ln_b[l], seg, alpha)

    y_prompt = x[:n_p_rows].reshape(n_b, l_p, d)
    y_sample = x[n_p_rows:].reshape(n_s, l_s, d)
    return (y_prompt, y_sample, jnp.stack(new_hgrn, axis=1), jnp.stack(new_rwkv, axis=1))
```

```python
import functools

import jax
import jax.numpy as jnp
from jax import lax
from jax.experimental import pallas as pl
from jax.experimental.pallas import tpu as pltpu

F32 = jnp.float32
BF16 = jnp.bfloat16
HIGHEST = lax.Precision.HIGHEST

LN_EPS = 1e-5
RMS_EPS = 1e-6
GN_EPS = 64e-5
DECAY_SCALE = 0.606531
EXP_CLIP = 80.0
POS_BASE = 10000.0
GRID_W = 64

LANES = 128
HGRN_CHUNK = 32
HGRN_CHUNK_LOG2 = HGRN_CHUNK.bit_length() - 1
HGRN_SAFE_EXPONENT = 75.0
RWKV_HEAD = 64
VMEM_LIMIT = 56 * 1024 * 1024

NT_DIMS = (((1,), (1,)), ((), ()))
TN_DIMS = (((0,), (0,)), ((), ()))


def _dot(a, b, dims=None, precision=None):
    if dims is None:
        return jnp.dot(a, b, preferred_element_type=F32, precision=precision)
    return lax.dot_general(a, b, dims, preferred_element_type=F32, precision=precision)


def _bdot(a, b, dims=None):
    return _dot(a.astype(BF16), b.astype(BF16), dims)


def _hdot(a, b, dims=None):
    return _dot(a, b, dims, precision=HIGHEST)


def _split_bf16(x):
    hi = x.astype(BF16)
    return hi, (x - hi.astype(F32)).astype(BF16)


def _dot_x3(a, b, dims=None):
    ah, al = _split_bf16(a)
    bh, bl = _split_bf16(b)
    if dims == NT_DIMS:
        a_cat = jnp.concatenate([ah, al], axis=1)
        b_half = jnp.concatenate([bh, bl], axis=0)
        b_cat = jnp.concatenate([b_half, b_half], axis=1)
        n = b.shape[0]
    else:
        a_cat = jnp.concatenate([ah, al], axis=0 if dims == TN_DIMS else 1)
        b_half = jnp.concatenate([bh, bl], axis=1)
        b_cat = jnp.concatenate([b_half, b_half], axis=0)
        n = b.shape[1]
    r = _dot(a_cat, b_cat, dims)
    return r[:, :n] + r[:, n:]


def _head_sums(x, head_masks):
    m0, m1 = head_masks
    s0 = jnp.sum(x * m0, axis=-1, keepdims=True)
    s1 = jnp.sum(x * m1, axis=-1, keepdims=True)
    return s0 * m0 + s1 * m1


def _dot_x2r(a, b):
    bh, bl = _split_bf16(b)
    n = b.shape[1]
    r = _dot(a.astype(BF16), jnp.concatenate([bh, bl], axis=1))
    return r[:, :n] + r[:, n:]


def _sigmoid(x):
    return jax.nn.sigmoid(x)


def _layer_norm(x, g, b):
    mu = jnp.mean(x, -1, keepdims=True)
    xc = x - mu
    var = jnp.mean(xc * xc, -1, keepdims=True)
    return xc * lax.rsqrt(var + LN_EPS) * g + b


def _seg_index(i, tm, n_prompt_rows, sample_len):
    start = i * tm
    return jnp.where(start < n_prompt_rows, 0, 1 + (start - n_prompt_rows) // sample_len)


def _adaln_kernel(c_ref, w_ref, b_ref, o_ref):
    c = c_ref[...]
    s = c * _sigmoid(c)
    o_ref[...] = _bdot(s, w_ref[...]) + b_ref[...]


def _adaln(cond8, ada_w, ada_b, tn=1536):
    depth, d, n = ada_w.shape
    return pl.pallas_call(
        _adaln_kernel,
        grid=(depth, n // tn),
        in_specs=[
            pl.BlockSpec((8, d), lambda l, j: (0, 0)),
            pl.BlockSpec((None, d, tn), lambda l, j: (l, 0, j)),
            pl.BlockSpec((None, 1, tn), lambda l, j: (l, 0, j)),
        ],
        out_specs=pl.BlockSpec((None, 8, tn), lambda l, j: (l, 0, j)),
        out_shape=jax.ShapeDtypeStruct((depth, 8, n), F32),
        compiler_params=pltpu.CompilerParams(
            dimension_semantics=("parallel", "parallel"), vmem_limit_bytes=VMEM_LIMIT),
        name="adaln",
    )(cond8, ada_w, ada_b.reshape(depth, 1, n))


def _modmm_kernel(x_ref, mod_ref, w_ref, o_ref, xb_ref):
    j = pl.program_id(1)

    @pl.when(j == 0)
    def _():
        m = mod_ref[...]
        xb_ref[...] = (x_ref[...] * (1.0 + m[1:2]) + m[0:1]).astype(BF16)

    o_ref[...] = _dot(xb_ref[...], w_ref[j])


def _modmm(x, mods, w, seg, tm=1024, tn=1024):
    t, d = x.shape
    n = w.shape[1]
    panels = w.reshape(d, n // tn, tn).transpose(1, 0, 2)
    return pl.pallas_call(
        _modmm_kernel,
        grid=(t // tm, n // tn),
        in_specs=[
            pl.BlockSpec((tm, d), lambda i, j: (i, 0)),
            pl.BlockSpec((None, 6, d), lambda i, j: (seg(i, tm), 0, 0)),
            pl.BlockSpec((n // tn, d, tn), lambda i, j: (0, 0, 0)),
        ],
        out_specs=pl.BlockSpec((tm, tn), lambda i, j: (i, j)),
        out_shape=jax.ShapeDtypeStruct((t, n), F32),
        scratch_shapes=[pltpu.VMEM((tm, d), BF16)],
        compiler_params=pltpu.CompilerParams(
            dimension_semantics=("parallel", "arbitrary"), vmem_limit_bytes=VMEM_LIMIT),
        name="modmm",
    )(x, mods, panels)


def _post_mixer_kernel(o_ref, x_ref, mod_ref, wo_ref, wu_ref, wd_ref, g_ref, b_ref,
                       y_ref, x1_ref, xb_ref, acc_ref, *, alpha):
    f = pl.program_id(1)

    @pl.when(f == 0)
    def _():
        m = mod_ref[...]
        y = _dot(o_ref[...], wo_ref[...])
        x1 = _layer_norm(alpha * x_ref[...] + m[2:3] * y, g_ref[0:1], b_ref[0:1])
        x1_ref[...] = x1
        xb_ref[...] = (x1 * (1.0 + m[4:5]) + m[3:4]).astype(BF16)
        acc_ref[...] = jnp.zeros_like(acc_ref)

    h = jnp.maximum(_dot(xb_ref[...], wu_ref[...]), 0.0)
    acc_ref[...] += _dot((h * h).astype(BF16), wd_ref[...])

    @pl.when(f == pl.num_programs(1) - 1)
    def _():
        gate = mod_ref[...][5:6]
        y_ref[...] = _layer_norm(alpha * x1_ref[...] + gate * acc_ref[...], g_ref[1:2], b_ref[1:2])


def _post_mixer(o, x, mods, w_o, w_up, w_down, ln_g, ln_b, seg, alpha, tm=1024, tf=1024):
    t, d = x.shape
    dff = w_up.shape[1]
    both = pl.BlockSpec((2, d), lambda i, f: (0, 0))
    return pl.pallas_call(
        functools.partial(_post_mixer_kernel, alpha=alpha),
        grid=(t // tm, dff // tf),
        in_specs=[
            pl.BlockSpec((tm, d), lambda i, f: (i, 0)),
            pl.BlockSpec((tm, d), lambda i, f: (i, 0)),
            pl.BlockSpec((None, 6, d), lambda i, f: (seg(i, tm), 0, 0)),
            pl.BlockSpec((d, d), lambda i, f: (0, 0)),
            pl.BlockSpec((d, tf), lambda i, f: (0, f)),
            pl.BlockSpec((tf, d), lambda i, f: (f, 0)),
            both, both,
        ],
        out_specs=pl.BlockSpec((tm, d), lambda i, f: (i, 0)),
        out_shape=jax.ShapeDtypeStruct((t, d), F32),
        scratch_shapes=[pltpu.VMEM((tm, d), F32), pltpu.VMEM((tm, d), BF16), pltpu.VMEM((tm, d), F32)],
        compiler_params=pltpu.CompilerParams(
            dimension_semantics=("parallel", "arbitrary"), vmem_limit_bytes=VMEM_LIMIT),
        name="post_mixer",
    )(o, x, mods, w_o, w_up, w_down, ln_g, ln_b)


def _hgrn_precompute(entries, r_i, c_i):
    pos = r_i & (HGRN_CHUNK - 1)
    same_chunk = (r_i >> HGRN_CHUNK_LOG2) == (c_i >> HGRN_CHUNK_LOG2)
    blk = same_chunk.astype(F32)
    pre = []
    for qr, zz, v, lbz, fwd in entries:
        causal = same_chunk & ((c_i <= r_i) if fwd else (c_i >= r_i))
        q = qr * _sigmoid(qr)
        e_abs = jnp.exp(-jnp.abs(zz))
        inv_1p = 1.0 / (1.0 + e_abs)
        log_f = jnp.minimum(zz, 0.0) + jnp.log((1.0 + lbz * jnp.exp(jnp.minimum(-zz, EXP_CLIP))) * inv_1p)
        k = (1.0 - lbz) * jnp.where(zz > 0.0, e_abs, 1.0) * inv_1p
        sums = _dot_x2r(jnp.concatenate([causal.astype(F32), blk], axis=0), log_f)
        cum = sums[:LANES]
        ctot = sums[LANES:]
        pre.append(dict(q=q, k=k, v=v, cum=cum, ctot=ctot, fwd=fwd, causal=causal,
                        q_in=q * jnp.exp(cum), k_out=k * jnp.exp(ctot - cum), dec=jnp.exp(ctot)))

    def scores_factored():
        out = []
        for e in pre:
            half = 0.5 * e["ctot"]
            qk = _dot_x3(e["q"] * jnp.exp(e["cum"] - half), e["k"] * jnp.exp(half - e["cum"]), NT_DIMS)
            out.append(jnp.where(e["causal"], qk, 0.0))
        return out

    def scores_pairwise():
        out = []
        for e in pre:
            fwd, q, k, cum = e["fwd"], e["q"], e["k"], e["cum"]
            scores = jnp.zeros((LANES, LANES), F32)
            for d in range(HGRN_CHUNK):
                if d == 0:
                    kr, cr = k, cum
                else:
                    sh = d if fwd else LANES - d
                    kr = pltpu.roll(k, sh, 0)
                    cr = pltpu.roll(cum, sh, 0)
                valid = (pos >= d) if fwd else (pos <= HGRN_CHUNK - 1 - d)
                ex = jnp.exp(jnp.where(valid, cum - cr, 0.0))
                s = jnp.sum(q * kr * ex, axis=-1, keepdims=True)
                tgt = (c_i == r_i - d) if fwd else (c_i == r_i + d)
                scores = scores + jnp.where(tgt & valid, s, 0.0)
            out.append(scores)
        return out

    lowest = pre[0]["ctot"]
    for e in pre[1:]:
        lowest = jnp.minimum(lowest, e["ctot"])
    scores = lax.cond(jnp.min(lowest) >= -2.0 * HGRN_SAFE_EXPONENT, scores_factored, scores_pairwise)

    chunk_of_lane = c_i >> HGRN_CHUNK_LOG2
    out = []
    for e, sc in zip(pre, scores):
        o_intra = _bdot(sc, e["v"])
        v_t = e["v"].T
        lhs = jnp.concatenate(
            [jnp.where(chunk_of_lane == c, v_t, 0.0) for c in range(LANES // HGRN_CHUNK)], axis=0)
        upd = _bdot(lhs, e["k_out"])
        out.append((e["q_in"], o_intra, upd, e["dec"]))
    return out


def _hgrn_rec_kernel(*refs, seq_len, groups, zero_init, want_state, aliased):
    qf_ref, zf_ref, vf_ref, qb_ref, zb_ref, vb_ref, gate_ref, lb_ref, ng_ref = refs[:9]
    pos = 9
    s0_ref = None
    if not zero_init:
        s0_ref = refs[pos]
        pos += 1
    if aliased:
        pos += 1
    o_ref = refs[pos]
    pos += 1
    sfin_ref = None
    if want_state:
        sfin_ref = refs[pos]
        pos += 1
    osum_ref, st_ref = refs[pos:pos + 2]

    n_tiles = seq_len // LANES
    n_chunks = LANES // HGRN_CHUNK
    unroll = min(8 // (2 * groups), n_tiles)
    assert n_tiles % unroll == 0 and (n_tiles == unroll or (n_tiles // 2) % unroll == 0)
    r_i = lax.broadcasted_iota(jnp.int32, (LANES, LANES), 0)
    c_i = lax.broadcasted_iota(jnp.int32, (LANES, LANES), 1)
    lb = lb_ref[...]
    data = ((qf_ref, zf_ref, vf_ref), (qb_ref, zb_ref, vb_ref))

    def lanes_of(grp):
        return slice(grp * LANES, (grp + 1) * LANES)

    for z in range(2):
        for grp in range(groups):
            if zero_init:
                st_ref[z * groups + grp] = jnp.zeros((LANES, LANES), F32)
            else:
                st_ref[z * groups + grp] = s0_ref[z, grp].T

    def finish(rows, cols, o):
        o = o * lax.rsqrt(jnp.mean(o * o, -1, keepdims=True) + RMS_EPS)
        g = gate_ref[rows, cols]
        o_ref[rows, cols] = (o * ng_ref[:, cols] * (g * _sigmoid(g))).astype(o_ref.dtype)

    def block(i, visit):
        slots = []
        for z in range(2):
            for grp in range(groups):
                for u in range(unroll):
                    t = i * unroll + u
                    slots.append((z, u, grp, t if z == 0 else n_tiles - 1 - t))
        entries = []
        for z, _, grp, t in slots:
            rows = pl.ds(pl.multiple_of(t * LANES, LANES), LANES)
            cols = lanes_of(grp)
            q_ref, z_ref, v_ref = data[z]
            entries.append((q_ref[rows, cols], z_ref[rows, cols], v_ref[rows, cols], lb[z:z + 1, cols], z == 0))
        pre = dict(zip([s[:3] for s in slots], _hgrn_precompute(entries, r_i, c_i)))

        st = [st_ref[c] for c in range(2 * groups)]
        inter = {key: [None] * n_chunks for key in pre}
        for u in range(unroll):
            for step in range(n_chunks):
                for z in range(2):
                    for grp in range(groups):
                        q_in, _, upd, dec = pre[(z, u, grp)]
                        ch = z * groups + grp
                        c = step if z == 0 else n_chunks - 1 - step
                        lo = c * HGRN_CHUNK
                        inter[(z, u, grp)][c] = _bdot(q_in[lo:lo + HGRN_CHUNK], st[ch], NT_DIMS)
                        st[ch] = st[ch] * dec[lo:lo + 1] + upd[c * LANES:(c + 1) * LANES]
        for ch in range(2 * groups):
            st_ref[ch] = st[ch]
        outs = {key: pre[key][1] + jnp.concatenate(inter[key], axis=0) for key in pre}

        if visit == "both":
            for t in range(n_tiles):
                for grp in range(groups):
                    finish(pl.ds(t * LANES, LANES), lanes_of(grp),
                           outs[(0, t, grp)] + outs[(1, n_tiles - 1 - t, grp)])
            return
        for z, u, grp, t in slots:
            rows, cols = pl.ds(pl.multiple_of(t * LANES, LANES), LANES), lanes_of(grp)
            if visit == "first":
                osum_ref[rows, cols] = outs[(z, u, grp)]
            else:
                finish(rows, cols, osum_ref[rows, cols] + outs[(z, u, grp)])

    if n_tiles == unroll:
        block(0, "both")
    else:
        def body(i, carry, visit):
            block(i, visit)
            return carry

        half = n_tiles // 2 // unroll
        lax.fori_loop(0, half, functools.partial(body, visit="first"), 0)
        lax.fori_loop(half, 2 * half, functools.partial(body, visit="second"), 0)
    if want_state:
        for z in range(2):
            for grp in range(groups):
                sfin_ref[z, grp] = st_ref[z * groups + grp].T


def _hgrn_rec(proj, lb, norm_g, s0, prev_out, *, n_seq, seq_len, row_block0, n_heads, total_rows):
    d = n_heads * LANES
    groups = 2 if seq_len // LANES <= 2 else 1
    n_cols = n_heads // groups
    zero_init = s0 is None
    want_state = s0 is None
    aliased = prev_out is not None

    def col(block):
        return lambda b, h: (row_block0 + b, block * n_cols + h)

    blk = (seq_len, groups * LANES)
    in_specs = [pl.BlockSpec(blk, col(0)), pl.BlockSpec(blk, col(1)), pl.BlockSpec(blk, col(2)),
                pl.BlockSpec(blk, col(3)), pl.BlockSpec(blk, col(4)), pl.BlockSpec(blk, col(5)),
                pl.BlockSpec(blk, col(6)),
                pl.BlockSpec((2, groups * LANES), lambda b, h: (0, h)),
                pl.BlockSpec((1, groups * LANES), lambda b, h: (0, h))]
    args = [proj] * 7 + [lb, norm_g.reshape(1, d)]
    state_spec = pl.BlockSpec((None, 2, groups, LANES, LANES), lambda b, h: (b, 0, h, 0, 0))
    if not zero_init:
        in_specs.append(state_spec)
        args.append(s0)
    io_alias = {}
    if aliased:
        in_specs.append(pl.BlockSpec(memory_space=pl.ANY))
        io_alias = {len(args): 0}
        args.append(prev_out)
    out_specs = [pl.BlockSpec(blk, lambda b, h: (row_block0 + b, h))]
    out_shape = [jax.ShapeDtypeStruct((total_rows, d), BF16)]
    if want_state:
        out_specs.append(state_spec)
        out_shape.append(jax.ShapeDtypeStruct((n_seq, 2, n_heads, LANES, LANES), F32))
    res = pl.pallas_call(
        functools.partial(_hgrn_rec_kernel, seq_len=seq_len, groups=groups, zero_init=zero_init,
                          want_state=want_state, aliased=aliased),
        grid=(n_seq, n_cols),
        in_specs=in_specs,
        out_specs=out_specs,
        out_shape=out_shape,
        scratch_shapes=[pltpu.VMEM((seq_len, groups * LANES), F32),
                        pltpu.VMEM((2 * groups, LANES, LANES), F32)],
        input_output_aliases=io_alias,
        compiler_params=pltpu.CompilerParams(
            dimension_semantics=("parallel", "parallel"), vmem_limit_bytes=VMEM_LIMIT),
        name="hgrn_rec",
    )(*args)
    return res


def _int_mod(x, n):
    return x & (n - 1) if n & (n - 1) == 0 else lax.rem(x, n)


def _token_shift(x_ref, xp_ref, xn_ref, mod_ref, tile_start, n_prompt_rows, prompt_len, sample_len):
    m = mod_ref[...]
    sh, sc = m[0:1], 1.0 + m[1:2]
    h = x_ref[...] * sc + sh
    h_before = xp_ref[7:8, :] * sc + sh
    h_after = xn_ref[0:1, :] * sc + sh
    tm = h.shape[0]
    rr = lax.broadcasted_iota(jnp.int32, (tm, 1), 0)
    grow = tile_start + rr
    in_prompt = grow < n_prompt_rows
    pos = jnp.where(in_prompt, _int_mod(grow, prompt_len), _int_mod(grow - n_prompt_rows, sample_len))
    last = jnp.where(in_prompt, prompt_len - 1, sample_len - 1)
    prev = jnp.where(rr == 0, h_before, pltpu.roll(h, 1, 0))
    prev = jnp.where(pos == 0, 0.0, prev)
    nxt = jnp.where(rr == tm - 1, h_after, pltpu.roll(h, tm - 1, 0))
    nxt = jnp.where(pos == last, 0.0, nxt)
    return h, 0.5 * (prev + nxt) - h


def _rwkv_rkv_kernel(x_ref, xp_ref, xn_ref, mod_ref, mu_ref, w_ref, o_ref, h_ref, xx_ref, *, tm, seq_info):
    @pl.when(pl.program_id(1) == 0)
    def _():
        h, xx = _token_shift(x_ref, xp_ref, xn_ref, mod_ref, pl.program_id(0) * tm, *seq_info)
        h_ref[...] = h
        xx_ref[...] = xx

    xs = h_ref[...] + xx_ref[...] * mu_ref[...]
    o_ref[...] = _dot(xs.astype(BF16), w_ref[pl.program_id(1)])


def _halo_specs(tm, d, t, n_grid_axes):
    nb = t // 8

    def before(i, *_):
        return (jnp.maximum(i * (tm // 8) - 1, 0), 0)

    def after(i, *_):
        return (jnp.minimum((i + 1) * (tm // 8), nb - 1), 0)

    del n_grid_axes
    return pl.BlockSpec((8, d), before), pl.BlockSpec((8, d), after)


def _rwkv_rkv(x, mods, mu3, w_rkv, seg, *, n_prompt_rows, prompt_len, sample_len, tm=512):
    t, d = x.shape
    n = w_rkv.shape[2]
    before, after = _halo_specs(tm, d, t, 2)
    return pl.pallas_call(
        functools.partial(_rwkv_rkv_kernel, tm=tm, seq_info=(n_prompt_rows, prompt_len, sample_len)),
        grid=(t // tm, 3),
        in_specs=[
            pl.BlockSpec((tm, d), lambda i, j: (i, 0)),
            before, after,
            pl.BlockSpec((None, 6, d), lambda i, j: (seg(i, tm), 0, 0)),
            pl.BlockSpec((None, 1, d), lambda i, j: (j, 0, 0)),
            pl.BlockSpec((3, d, n), lambda i, j: (0, 0, 0)),
        ],
        out_specs=pl.BlockSpec((None, tm, n), lambda i, j: (j, i, 0)),
        out_shape=jax.ShapeDtypeStruct((3, t, n), F32),
        scratch_shapes=[pltpu.VMEM((tm, d), F32), pltpu.VMEM((tm, d), F32)],
        compiler_params=pltpu.CompilerParams(
            dimension_semantics=("parallel", "arbitrary"), vmem_limit_bytes=VMEM_LIMIT),
        name="rwkv_rkv",
    )(x, x, x, mods, mu3, w_rkv)


def _rwkv_lora_kernel(x_ref, xp_ref, xn_ref, mod_ref, mu_ref, wla_ref, wlb_ref, w0_ref, ala_ref, alb_ref,
                      a0_ref, gla_ref, glb_ref, lw_ref, a_ref, g_ref, *, tm, seq_info):
    h, xx = _token_shift(x_ref, xp_ref, xn_ref, mod_ref, pl.program_id(0) * tm, *seq_info)
    mu = mu_ref[...]
    xs_w = (h + xx * mu[1:2]).astype(BF16)
    xs_a = (h + xx * mu[4:5]).astype(BF16)
    xs_g = (h + xx * mu[5:6]).astype(BF16)
    zw = w0_ref[...] + _bdot(jnp.tanh(_dot(xs_w, wla_ref[...])), wlb_ref[...])
    lw_ref[...] = -DECAY_SCALE * _sigmoid(zw)
    za = a0_ref[...] + _bdot(_dot(xs_a, ala_ref[...]), alb_ref[...])
    a_ref[...] = _sigmoid(za)
    g_ref[...] = _bdot(_sigmoid(_dot(xs_g, gla_ref[...])), glb_ref[...])


def _rwkv_lora(x, mods, mu, weights, seg, *, n_prompt_rows, prompt_len, sample_len, tm=256):
    t, d = x.shape
    before, after = _halo_specs(tm, d, t, 1)

    def whole(arr):
        return pl.BlockSpec(arr.shape, lambda i: (0,) * arr.ndim)

    return pl.pallas_call(
        functools.partial(_rwkv_lora_kernel, tm=tm, seq_info=(n_prompt_rows, prompt_len, sample_len)),
        grid=(t // tm,),
        in_specs=[pl.BlockSpec((tm, d), lambda i: (i, 0)), before, after,
                  pl.BlockSpec((None, 6, d), lambda i: (seg(i, tm), 0, 0)), whole(mu)]
                 + [whole(w) for w in weights],
        out_specs=[pl.BlockSpec((tm, 2 * d), lambda i: (i, 0)),
                   pl.BlockSpec((tm, 2 * d), lambda i: (i, 0)),
                   pl.BlockSpec((tm, d), lambda i: (i, 0))],
        out_shape=[jax.ShapeDtypeStruct((t, 2 * d), F32), jax.ShapeDtypeStruct((t, 2 * d), F32),
                   jax.ShapeDtypeStruct((t, d), F32)],
        compiler_params=pltpu.CompilerParams(
            dimension_semantics=("parallel",), vmem_limit_bytes=VMEM_LIMIT),
        name="rwkv_lora",
    )(x, x, x, mods, mu, *weights)


def _rwkv_precompute(entries, consts):
    bd, eye, head_masks, r_i, c_i = consts
    n = len(entries)
    pre = []
    for r, k, v, lw, a, (kkp, kap, rkp), fwd in entries:
        tri = ((c_i <= r_i) if fwd else (c_i >= r_i)).astype(F32)
        kk = k * kkp
        kk = kk / jnp.maximum(jnp.sqrt(_head_sums(kk * kk, head_masks)), 1e-12)
        k2 = k * (1.0 + (a - 1.0) * kap)
        bonus = _head_sums(r * k2 * rkp, head_masks) * v
        b = kk * a
        cl = _dot_x2r(tri, lw)
        cle = cl - lw
        cm = cl[LANES // 2:LANES // 2 + 1]
        ct = cl[LANES - 1:LANES] if fwd else cl[0:1]
        e_inv = jnp.exp(cm - cl)
        e_out = jnp.exp(ct - cl)
        pre.append(dict(
            v=v, bonus=bonus, fwd=fwd,
            kkt=kk * jnp.exp(cle - cm), rt=r * jnp.exp(cl - cm), kh=k2 * e_inv, bh=b * e_inv,
            kkd=kk * jnp.exp(cle), rd=r * jnp.exp(cl), kg=k2 * e_out, bg=b * e_out, gc=jnp.exp(ct)))

    for e in pre:
        lhs = jnp.concatenate([e["kkt"] * head_masks[0], e["kkt"] * head_masks[1],
                               e["rt"] * head_masks[0], e["rt"] * head_masks[1]], axis=0)
        e["gk"] = _dot_x3(lhs, e["kh"], NT_DIMS)
        e["gb"] = _dot_x3(lhs, e["bh"], NT_DIMS)

    chains = []
    for e in pre:
        strict = (c_i < r_i) if e["fwd"] else (c_i > r_i)
        incl = (c_i <= r_i) if e["fwd"] else (c_i >= r_i)
        for hd in range(2):
            lo = hd * LANES
            chains.append(dict(
                e=e, hd=hd,
                a_k=jnp.where(strict, e["gk"][lo:lo + LANES], 0.0),
                a_b=jnp.where(strict, e["gb"][lo:lo + LANES], 0.0),
                b_k=jnp.where(incl, e["gk"][2 * LANES + lo:3 * LANES + lo], 0.0),
                b_b=jnp.where(incl, e["gb"][2 * LANES + lo:3 * LANES + lo], 0.0)))

    for c in chains:
        c["m"] = eye - jnp.where((r_i >> 1) == (c_i >> 1), c["a_b"], 0.0)
    for log_k in range(1, 7):
        k = 1 << log_k
        off = ((r_i >> (log_k + 1)) == (c_i >> (log_k + 1))) & ((r_i >> log_k) != (c_i >> log_k))
        if k < 8:
            for c in chains:
                c["t"] = _dot_x3(jnp.where(off, c["a_b"], 0.0), c["m"])
            for c in chains:
                c["m"] = c["m"] - _dot_x3(c["m"], c["t"])
            continue

        def take(x, fwd):
            first = k if fwd else 0
            return jnp.concatenate([x[lo:lo + k] for lo in range(first, LANES, 2 * k)], axis=0)

        def spread(rows, fwd):
            zero = jnp.zeros((k, LANES), F32)
            parts = []
            for j in range(LANES // (2 * k)):
                piece = rows[j * k:(j + 1) * k]
                parts += [zero, piece] if fwd else [piece, zero]
            return jnp.concatenate(parts, axis=0)

        for c in chains:
            fwd = c["e"]["fwd"]
            c["t"] = spread(_dot_x3(take(jnp.where(off, c["a_b"], 0.0), fwd), c["m"]), fwd)
        for c in chains:
            fwd = c["e"]["fwd"]
            c["m"] = c["m"] - spread(_dot_x3(take(c["m"], fwd), c["t"]), fwd)

    for c in chains:
        e = c["e"]
        c["v_h"] = e["v"] * head_masks[c["hd"]]
        c["kkd_h"] = e["kkd"] * head_masks[c["hd"]]
        c["akv"] = _dot_x3(c["a_k"], c["v_h"])
    for c in chains:
        c["wt"] = _dot_x3(c["m"], c["kkd_h"])
        c["u0"] = _dot_x3(c["m"], c["akv"])
    for c in chains:
        c["y0"] = _dot_x3(c["b_k"], c["v_h"]) - _dot_x3(c["b_b"], c["u0"])
        c["rp"] = _dot_x3(c["b_b"], c["wt"])

    out = []
    for i, e in enumerate(pre):
        c0, c1 = chains[2 * i], chains[2 * i + 1]
        wt = c0["wt"] + c1["wt"]
        u0 = c0["u0"] + c1["u0"]
        q0 = bd * (_dot_x3(e["v"], e["kg"], TN_DIMS) - _dot_x3(u0.T, e["bg"]))
        p_mat = _dot_x3(wt, e["bg"], TN_DIMS)
        out.append((e["rd"] - c0["rp"] - c1["rp"], c0["y0"] + c1["y0"], e["gc"], q0, p_mat, e["bonus"]))
    assert len(out) == n
    return out


def _rwkv_rec_kernel(*refs, seq_len, groups, zero_init, want_state, aliased):
    data = (refs[0:5], refs[5:10])
    g_ref, kk_ref, ka_ref, rk_ref, gg_ref, gb_ref = refs[10:16]
    pos = 16
    s0_ref = None
    if not zero_init:
        s0_ref = refs[pos]
        pos += 1
    if aliased:
        pos += 1
    o_ref = refs[pos]
    pos += 1
    sfin_ref = None
    if want_state:
        sfin_ref = refs[pos]
        pos += 1
    osum_ref, st_ref = refs[pos:pos + 2]

    n_tiles = seq_len // LANES
    unroll = min(8 // (2 * groups), n_tiles)
    assert n_tiles % unroll == 0 and (n_tiles == unroll or (n_tiles // 2) % unroll == 0)
    r_i = lax.broadcasted_iota(jnp.int32, (LANES, LANES), 0)
    c_i = lax.broadcasted_iota(jnp.int32, (LANES, LANES), 1)
    bd = ((r_i >> 6) == (c_i >> 6)).astype(F32)
    eye = (r_i == c_i).astype(F32)
    lane = lax.broadcasted_iota(jnp.int32, (1, LANES), 1)
    head_masks = ((lane < RWKV_HEAD).astype(F32), (lane >= RWKV_HEAD).astype(F32))
    consts = (bd, eye, head_masks, r_i, c_i)
    inv_n = 1.0 / RWKV_HEAD

    def lanes_of(grp):
        return slice(grp * LANES, (grp + 1) * LANES)

    for z in range(2):
        for grp in range(groups):
            if zero_init:
                st_ref[z * groups + grp] = jnp.zeros((LANES, LANES), F32)
            else:
                zero = jnp.zeros((RWKV_HEAD, RWKV_HEAD), F32)
                st_ref[z * groups + grp] = jnp.concatenate(
                    [jnp.concatenate([s0_ref[z, 2 * grp], zero], axis=1),
                     jnp.concatenate([zero, s0_ref[z, 2 * grp + 1]], axis=1)], axis=0)

    def block(i, visit):
        slots = []
        for z in range(2):
            for grp in range(groups):
                for u in range(unroll):
                    t = i * unroll + u
                    slots.append((z, u, grp, t if z == 0 else n_tiles - 1 - t))
        entries = []
        for z, _, grp, t in slots:
            rows = pl.ds(pl.multiple_of(t * LANES, LANES), LANES)
            cols = lanes_of(grp)
            r_ref, k_ref, v_ref, lw_ref, a_ref = data[z]
            params = (kk_ref[z:z + 1, cols], ka_ref[z:z + 1, cols], rk_ref[z:z + 1, cols])
            entries.append((r_ref[rows, cols], k_ref[rows, cols], v_ref[rows, cols], lw_ref[rows, cols],
                            a_ref[rows, cols], params, z == 0))
        pre = _rwkv_precompute(entries, consts)

        ys = []
        st = [st_ref[c] for c in range(2 * groups)]
        for (z, _, grp, _), (rp, y0, gc, q0, p_mat, _) in zip(slots, pre):
            c = z * groups + grp
            ys.append(_dot_x3(rp, st[c], NT_DIMS) + y0)
            st[c] = st[c] * gc + q0 - bd * _dot_x3(st[c], p_mat)
        for c in range(2 * groups):
            st_ref[c] = st[c]

        outs = {}
        for (z, u, grp, _), y, (_, _, _, _, _, bonus) in zip(slots, ys, pre):
            cols = lanes_of(grp)
            mean = _head_sums(y, head_masks) * inv_n
            yc = y - mean
            var = _head_sums(yc * yc, head_masks) * inv_n
            outs[(z, u, grp)] = (yc * lax.rsqrt(var + GN_EPS) * gg_ref[z:z + 1, cols] + gb_ref[z:z + 1, cols]
                                 + bonus)

        if visit == "both":
            for t in range(n_tiles):
                for grp in range(groups):
                    rows, cols = pl.ds(t * LANES, LANES), lanes_of(grp)
                    total = outs[(0, t, grp)] + outs[(1, n_tiles - 1 - t, grp)]
                    o_ref[rows, cols] = (total * g_ref[rows, cols]).astype(o_ref.dtype)
            return
        for z, u, grp, t in slots:
            out = outs[(z, u, grp)]
            rows, cols = pl.ds(pl.multiple_of(t * LANES, LANES), LANES), lanes_of(grp)
            if visit == "first":
                osum_ref[rows, cols] = out
            else:
                o_ref[rows, cols] = ((osum_ref[rows, cols] + out) * g_ref[rows, cols]).astype(o_ref.dtype)

    if n_tiles == unroll:
        block(0, "both")
    else:
        def body(i, carry, visit):
            block(i, visit)
            return carry

        half = n_tiles // 2 // unroll
        lax.fori_loop(0, half, functools.partial(body, visit="first"), 0)
        lax.fori_loop(half, 2 * half, functools.partial(body, visit="second"), 0)
    if want_state:
        for z in range(2):
            for grp in range(groups):
                st = st_ref[z * groups + grp]
                sfin_ref[z, 2 * grp] = st[:RWKV_HEAD, :RWKV_HEAD]
                sfin_ref[z, 2 * grp + 1] = pltpu.roll(st, RWKV_HEAD, 1)[RWKV_HEAD:, :RWKV_HEAD]


def _rwkv_rec(rkv, lw, a, g, params, s0, prev_out, *, n_seq, seq_len, row_block0, total_rows):
    d = g.shape[1]
    groups = 2 if seq_len // LANES <= 2 else 1
    n_cols = d // (groups * LANES)
    zero_init = s0 is None
    want_state = s0 is None
    aliased = prev_out is not None
    blk = (seq_len, groups * LANES)

    def dir_specs(z):
        col = lambda b, p: (row_block0 + b, z * n_cols + p)
        return ([pl.BlockSpec((None,) + blk, lambda b, p, j=j: (j, row_block0 + b, z * n_cols + p))
                 for j in range(3)] + [pl.BlockSpec(blk, col), pl.BlockSpec(blk, col)])

    in_specs = dir_specs(0) + dir_specs(1)
    in_specs.append(pl.BlockSpec(blk, lambda b, p: (row_block0 + b, p)))
    in_specs += [pl.BlockSpec((2, groups * LANES), lambda b, p: (0, p)) for _ in range(5)]
    args = [rkv, rkv, rkv, lw, a] * 2 + [g] + list(params)
    if not zero_init:
        in_specs.append(pl.BlockSpec((None, 2, 2 * groups, RWKV_HEAD, RWKV_HEAD), lambda b, p: (b, 0, p, 0, 0)))
        args.append(s0)
    io_alias = {}
    if aliased:
        in_specs.append(pl.BlockSpec(memory_space=pl.ANY))
        io_alias = {len(args): 0}
        args.append(prev_out)
    out_specs = [pl.BlockSpec(blk, lambda b, p: (row_block0 + b, p))]
    out_shape = [jax.ShapeDtypeStruct((total_rows, d), BF16)]
    if want_state:
        out_specs.append(pl.BlockSpec((None, 2, 2 * groups, RWKV_HEAD, RWKV_HEAD), lambda b, p: (b, 0, p, 0, 0)))
        out_shape.append(jax.ShapeDtypeStruct((n_seq, 2, d // RWKV_HEAD, RWKV_HEAD, RWKV_HEAD), F32))
    return pl.pallas_call(
        functools.partial(_rwkv_rec_kernel, seq_len=seq_len, groups=groups, zero_init=zero_init,
                          want_state=want_state, aliased=aliased),
        grid=(n_seq, n_cols),
        in_specs=in_specs,
        out_specs=out_specs,
        out_shape=out_shape,
        scratch_shapes=[pltpu.VMEM((seq_len, groups * LANES), F32),
                        pltpu.VMEM((2 * groups, LANES, LANES), F32)],
        input_output_aliases=io_alias,
        compiler_params=pltpu.CompilerParams(
            dimension_semantics=("parallel", "parallel"), vmem_limit_bytes=VMEM_LIMIT),
        name="rwkv_rec",
    )(*args)


def _rwkv_prepare_weights(mu, w_rkv, w0, w_la, w_lb, a0, a_la, a_lb, g_la, g_lb):
    d = mu.shape[1]
    rank_w = w_la.shape[2]
    rank_a = a_la.shape[2]
    rank_g = g_la.shape[1]
    rank_g_pad = -(-rank_g // LANES) * LANES

    def block_diag(w):
        rank = w.shape[1]
        out = jnp.zeros((2, rank, 2, d), w.dtype)
        out = out.at[0, :, 0, :].set(w[0]).at[1, :, 1, :].set(w[1])
        return out.reshape(2 * rank, 2 * d)

    lora = (
        w_la.reshape(d, 2 * rank_w).astype(BF16), block_diag(w_lb).astype(BF16), w0.reshape(1, 2 * d),
        a_la.reshape(d, 2 * rank_a).astype(BF16), block_diag(a_lb).astype(BF16), a0.reshape(1, 2 * d),
        jnp.pad(g_la, ((0, 0), (0, rank_g_pad - rank_g))).astype(BF16),
        jnp.pad(g_lb, ((0, rank_g_pad - rank_g), (0, 0))).astype(BF16),
    )
    mu3 = jnp.stack([mu[0], mu[2], mu[3]]).reshape(3, 1, d)
    return {"mu3": mu3, "w_rkv": w_rkv.astype(BF16), "lora": lora}


def _embed_kernel(xp_ref, xs_ref, row_ref, col_ref, o_ref, *, n_prompt_tiles):
    i = pl.program_id(0)

    @pl.when(i < n_prompt_tiles)
    def _():
        o_ref[...] = xp_ref[...]

    @pl.when(i >= n_prompt_tiles)
    def _():
        half = o_ref.shape[1] // 2
        for grp in range(o_ref.shape[0] // GRID_W):
            rows = pl.ds(grp * GRID_W, GRID_W)
            o_ref[rows, :half] = xs_ref[rows, :half] + row_ref[grp:grp + 1, :]
            o_ref[rows, half:] = xs_ref[rows, half:] + col_ref[...]


def _grid_pos_tables(n_tokens, d):
    quarter = d // 4
    omega = 1.0 / (POS_BASE ** (jnp.arange(quarter, dtype=F32) / quarter))
    r = jnp.arange(n_tokens // GRID_W, dtype=F32)[:, None] * omega
    cc = jnp.arange(GRID_W, dtype=F32)[:, None] * omega
    return (jnp.concatenate([jnp.sin(r), jnp.cos(r)], -1), jnp.concatenate([jnp.sin(cc), jnp.cos(cc)], -1))


def _embed(xp, xs, sample_len, tm=512):
    n_p, d = xp.shape
    n_s = xs.shape[0]
    npt = n_p // tm
    pos_tiles = sample_len // tm
    grid_rows = tm // GRID_W
    row_emb, col_emb = _grid_pos_tables(sample_len, d)
    return pl.pallas_call(
        functools.partial(_embed_kernel, n_prompt_tiles=npt),
        grid=((n_p + n_s) // tm,),
        in_specs=[
            pl.BlockSpec((tm, d), lambda i: (jnp.minimum(i, npt - 1), 0)),
            pl.BlockSpec((tm, d), lambda i: (jnp.maximum(i - npt, 0), 0)),
            pl.BlockSpec((grid_rows, d // 2), lambda i: (lax.rem(jnp.maximum(i - npt, 0), pos_tiles), 0)),
            pl.BlockSpec((GRID_W, d // 2), lambda i: (0, 0)),
        ],
        out_specs=pl.BlockSpec((tm, d), lambda i: (i, 0)),
        out_shape=jax.ShapeDtypeStruct((n_p + n_s, d), F32),
        compiler_params=pltpu.CompilerParams(
            dimension_semantics=("parallel",), vmem_limit_bytes=VMEM_LIMIT),
        name="embed",
    )(xp, xs, row_emb, col_emb)


def kernel(x_prompt, x_sample, state_hgrn, state_rwkv, c, c_ctx, ada_w, ada_b, ln_g, ln_b, ffn_w_up, ffn_w_down, hgrn_w_in, hgrn_lb, hgrn_norm_g, hgrn_w_o, rwkv_mu, rwkv_w_rkv, rwkv_w0, rwkv_w_la, rwkv_w_lb, rwkv_a0, rwkv_a_la, rwkv_a_lb, rwkv_g_la, rwkv_g_lb, rwkv_k_k, rwkv_k_a, rwkv_r_k, rwkv_gn_g, rwkv_gn_b, rwkv_w_o):
    n_b, l_p, d = x_prompt.shape
    n_s, l_s, _ = x_sample.shape
    depth = ada_w.shape[0]
    n_p_rows = n_b * l_p
    total = n_p_rows + n_s * l_s
    assert n_p_rows % l_s == 0 and l_p % LANES == 0 and l_s % LANES == 0
    alpha = (2 * depth) ** 0.25
    a_heads = d // LANES

    def seg(i, tm):
        return _seg_index(i, tm, n_p_rows, l_s)

    x = _embed(x_prompt.reshape(n_p_rows, d), x_sample.reshape(n_s * l_s, d), l_s)

    cond8 = jnp.zeros((8, d), F32).at[0].set(c_ctx).at[1:1 + n_s].set(c)
    mods = _adaln(cond8, ada_w, ada_b).reshape(depth, 8, 6, d)

    lb_soft = jax.nn.softmax(hgrn_lb.astype(F32), axis=0)
    lower_bounds = jnp.cumsum(lb_soft, axis=0) - lb_soft[0]

    new_hgrn = []
    new_rwkv = []
    for l in range(depth):
        j = l // 2
        if l % 2 == 0:
            proj = _modmm(x, mods[l], hgrn_w_in[j].astype(BF16), seg)
            o, s_ctx = _hgrn_rec(proj, lower_bounds[j], hgrn_norm_g[j], None, None, n_seq=n_b, seq_len=l_p,
                                 row_block0=0, n_heads=a_heads, total_rows=total)
            (o,) = _hgrn_rec(proj, lower_bounds[j], hgrn_norm_g[j], state_hgrn[:, j], o, n_seq=n_s, seq_len=l_s,
                             row_block0=n_p_rows // l_s, n_heads=a_heads, total_rows=total)
            new_hgrn.append(s_ctx)
            w_o = hgrn_w_o[j]
        else:
            prep = _rwkv_prepare_weights(rwkv_mu[j], rwkv_w_rkv[j], rwkv_w0[j], rwkv_w_la[j], rwkv_w_lb[j],
                                         rwkv_a0[j], rwkv_a_la[j], rwkv_a_lb[j], rwkv_g_la[j], rwkv_g_lb[j])
            seq_kw = dict(n_prompt_rows=n_p_rows, prompt_len=l_p, sample_len=l_s)
            rkv = _rwkv_rkv(x, mods[l], prep["mu3"], prep["w_rkv"], seg, **seq_kw)
            lw, a, g = _rwkv_lora(x, mods[l], rwkv_mu[j], prep["lora"], seg, **seq_kw)
            params = (rwkv_k_k[j], rwkv_k_a[j], rwkv_r_k[j], rwkv_gn_g[j], rwkv_gn_b[j])
            o, s_ctx = _rwkv_rec(rkv, lw, a, g, params, None, None, n_seq=n_b, seq_len=l_p,
                                 row_block0=0, total_rows=total)
            (o,) = _rwkv_rec(rkv, lw, a, g, params, state_rwkv[:, j], o, n_seq=n_s, seq_len=l_s,
                             row_block0=n_p_rows // l_s, total_rows=total)
            new_rwkv.append(s_ctx)
            w_o = rwkv_w_o[j]
        x = _post_mixer(o, x, mods[l], w_o.astype(BF16), ffn_w_up[l].astype(BF16), ffn_w_down[l].astype(BF16),
                        ln_g[l], ln_b[l], seg, alpha)

    y_prompt = x[:n_p_rows].reshape(n_b, l_p, d)
    y_sample = x[n_p_rows:].reshape(n_s, l_s, d)
    return (y_prompt, y_sample, jnp.stack(new_hgrn, axis=1), jnp.stack(new_rwkv, axis=1))
```

```python
import functools

import jax
import jax.numpy as jnp
from jax import lax
from jax.experimental import pallas as pl
from jax.experimental.pallas import tpu as pltpu

F32 = jnp.float32
BF16 = jnp.bfloat16
HIGHEST = lax.Precision.HIGHEST

LN_EPS = 1e-5
RMS_EPS = 1e-6
GN_EPS = 64e-5
DECAY_SCALE = 0.606531
EXP_CLIP = 80.0
POS_BASE = 10000.0
GRID_W = 64

LANES = 128
HGRN_CHUNK = 32
HGRN_CHUNK_LOG2 = HGRN_CHUNK.bit_length() - 1
HGRN_SAFE_EXPONENT = 75.0
RWKV_HEAD = 64
VMEM_LIMIT = 56 * 1024 * 1024

NT_DIMS = (((1,), (1,)), ((), ()))
TN_DIMS = (((0,), (0,)), ((), ()))


def _dot(a, b, dims=None, precision=None):
    if dims is None:
        return jnp.dot(a, b, preferred_element_type=F32, precision=precision)
    return lax.dot_general(a, b, dims, preferred_element_type=F32, precision=precision)


def _bdot(a, b, dims=None):
    return _dot(a.astype(BF16), b.astype(BF16), dims)


def _hdot(a, b, dims=None):
    return _dot(a, b, dims, precision=HIGHEST)


def _split_bf16(x):
    hi = x.astype(BF16)
    return hi, (x - hi.astype(F32)).astype(BF16)


def _dot_x3(a, b, dims=None):
    ah, al = _split_bf16(a)
    bh, bl = _split_bf16(b)
    if dims == NT_DIMS:
        a_cat = jnp.concatenate([ah, al], axis=1)
        b_half = jnp.concatenate([bh, bl], axis=0)
        b_cat = jnp.concatenate([b_half, b_half], axis=1)
        n = b.shape[0]
    else:
        a_cat = jnp.concatenate([ah, al], axis=0 if dims == TN_DIMS else 1)
        b_half = jnp.concatenate([bh, bl], axis=1)
        b_cat = jnp.concatenate([b_half, b_half], axis=0)
        n = b.shape[1]
    r = _dot(a_cat, b_cat, dims)
    return r[:, :n] + r[:, n:]


def _head_sums(x, head_masks):
    m0, m1 = head_masks
    s0 = jnp.sum(x * m0, axis=-1, keepdims=True)
    s1 = jnp.sum(x * m1, axis=-1, keepdims=True)
    return s0 * m0 + s1 * m1


def _dot_x2r(a, b):
    bh, bl = _split_bf16(b)
    n = b.shape[1]
    r = _dot(a.astype(BF16), jnp.concatenate([bh, bl], axis=1))
    return r[:, :n] + r[:, n:]


def _sigmoid(x):
    return jax.nn.sigmoid(x)


def _layer_norm(x, g, b):
    mu = jnp.mean(x, -1, keepdims=True)
    xc = x - mu
    var = jnp.mean(xc * xc, -1, keepdims=True)
    return xc * lax.rsqrt(var + LN_EPS) * g + b


def _seg_index(i, tm, n_prompt_rows, sample_len):
    start = i * tm
    return jnp.where(start < n_prompt_rows, 0, 1 + (start - n_prompt_rows) // sample_len)


def _adaln_kernel(c_ref, w_ref, b_ref, o_ref):
    c = c_ref[...]
    s = c * _sigmoid(c)
    o_ref[...] = _bdot(s, w_ref[...]) + b_ref[...]


def _adaln(cond8, ada_w, ada_b, tn=1536):
    depth, d, n = ada_w.shape
    return pl.pallas_call(
        _adaln_kernel,
        grid=(depth, n // tn),
        in_specs=[
            pl.BlockSpec((8, d), lambda l, j: (0, 0)),
            pl.BlockSpec((None, d, tn), lambda l, j: (l, 0, j)),
            pl.BlockSpec((None, 1, tn), lambda l, j: (l, 0, j)),
        ],
        out_specs=pl.BlockSpec((None, 8, tn), lambda l, j: (l, 0, j)),
        out_shape=jax.ShapeDtypeStruct((depth, 8, n), F32),
        compiler_params=pltpu.CompilerParams(
            dimension_semantics=("parallel", "parallel"), vmem_limit_bytes=VMEM_LIMIT),
        name="adaln",
    )(cond8, ada_w, ada_b.reshape(depth, 1, n))


def _modmm_kernel(x_ref, mod_ref, w_ref, o_ref, xb_ref):
    j = pl.program_id(1)

    @pl.when(j == 0)
    def _():
        m = mod_ref[...]
        xb_ref[...] = (x_ref[...] * (1.0 + m[1:2]) + m[0:1]).astype(BF16)

    o_ref[...] = _dot(xb_ref[...], w_ref[j])


def _modmm(x, mods, w, seg, tm=1024, tn=1024):
    t, d = x.shape
    n = w.shape[1]
    panels = w.reshape(d, n // tn, tn).transpose(1, 0, 2)
    return pl.pallas_call(
        _modmm_kernel,
        grid=(t // tm, n // tn),
        in_specs=[
            pl.BlockSpec((tm, d), lambda i, j: (i, 0)),
            pl.BlockSpec((None, 6, d), lambda i, j: (seg(i, tm), 0, 0)),
            pl.BlockSpec((n // tn, d, tn), lambda i, j: (0, 0, 0)),
        ],
        out_specs=pl.BlockSpec((tm, tn), lambda i, j: (i, j)),
        out_shape=jax.ShapeDtypeStruct((t, n), F32),
        scratch_shapes=[pltpu.VMEM((tm, d), BF16)],
        compiler_params=pltpu.CompilerParams(
            dimension_semantics=("parallel", "arbitrary"), vmem_limit_bytes=VMEM_LIMIT),
        name="modmm",
    )(x, mods, panels)


def _post_mixer_kernel(o_ref, x_ref, mod_ref, wo_ref, wu_ref, wd_ref, g_ref, b_ref,
                       y_ref, x1_ref, xb_ref, acc_ref, *, alpha):
    f = pl.program_id(1)

    @pl.when(f == 0)
    def _():
        m = mod_ref[...]
        y = _dot(o_ref[...], wo_ref[...])
        x1 = _layer_norm(alpha * x_ref[...] + m[2:3] * y, g_ref[0:1], b_ref[0:1])
        x1_ref[...] = x1
        xb_ref[...] = (x1 * (1.0 + m[4:5]) + m[3:4]).astype(BF16)
        acc_ref[...] = jnp.zeros_like(acc_ref)

    h = jnp.maximum(_dot(xb_ref[...], wu_ref[...]), 0.0)
    acc_ref[...] += _dot((h * h).astype(BF16), wd_ref[...])

    @pl.when(f == pl.num_programs(1) - 1)
    def _():
        gate = mod_ref[...][5:6]
        y_ref[...] = _layer_norm(alpha * x1_ref[...] + gate * acc_ref[...], g_ref[1:2], b_ref[1:2])


def _post_mixer(o, x, mods, w_o, w_up, w_down, ln_g, ln_b, seg, alpha, tm=1024, tf=1024):
    t, d = x.shape
    dff = w_up.shape[1]
    both = pl.BlockSpec((2, d), lambda i, f: (0, 0))
    return pl.pallas_call(
        functools.partial(_post_mixer_kernel, alpha=alpha),
        grid=(t // tm, dff // tf),
        in_specs=[
            pl.BlockSpec((tm, d), lambda i, f: (i, 0)),
            pl.BlockSpec((tm, d), lambda i, f: (i, 0)),
            pl.BlockSpec((None, 6, d), lambda i, f: (seg(i, tm), 0, 0)),
            pl.BlockSpec((d, d), lambda i, f: (0, 0)),
            pl.BlockSpec((d, tf), lambda i, f: (0, f)),
            pl.BlockSpec((tf, d), lambda i, f: (f, 0)),
            both, both,
        ],
        out_specs=pl.BlockSpec((tm, d), lambda i, f: (i, 0)),
        out_shape=jax.ShapeDtypeStruct((t, d), F32),
        scratch_shapes=[pltpu.VMEM((tm, d), F32), pltpu.VMEM((tm, d), BF16), pltpu.VMEM((tm, d), F32)],
        compiler_params=pltpu.CompilerParams(
            dimension_semantics=("parallel", "arbitrary"), vmem_limit_bytes=VMEM_LIMIT),
        name="post_mixer",
    )(o, x, mods, w_o, w_up, w_down, ln_g, ln_b)


def _hgrn_precompute(entries, r_i, c_i):
    pos = r_i & (HGRN_CHUNK - 1)
    same_chunk = (r_i >> HGRN_CHUNK_LOG2) == (c_i >> HGRN_CHUNK_LOG2)
    blk = same_chunk.astype(F32)
    pre = []
    for qr, zz, v, lbz, fwd in entries:
        causal = same_chunk & ((c_i <= r_i) if fwd else (c_i >= r_i))
        q = qr * _sigmoid(qr)
        e_abs = jnp.exp(-jnp.abs(zz))
        inv_1p = 1.0 / (1.0 + e_abs)
        log_f = jnp.minimum(zz, 0.0) + jnp.log((1.0 + lbz * jnp.exp(jnp.minimum(-zz, EXP_CLIP))) * inv_1p)
        k = (1.0 - lbz) * jnp.where(zz > 0.0, e_abs, 1.0) * inv_1p
        sums = _dot_x2r(jnp.concatenate([causal.astype(F32), blk], axis=0), log_f)
        cum = sums[:LANES]
        ctot = sums[LANES:]
        pre.append(dict(q=q, k=k, v=v, cum=cum, ctot=ctot, fwd=fwd, causal=causal,
                        q_in=q * jnp.exp(cum), k_out=k * jnp.exp(ctot - cum), dec=jnp.exp(ctot)))

    def scores_factored():
        out = []
        for e in pre:
            half = 0.5 * e["ctot"]
            qk = _dot_x3(e["q"] * jnp.exp(e["cum"] - half), e["k"] * jnp.exp(half - e["cum"]), NT_DIMS)
            out.append(jnp.where(e["causal"], qk, 0.0))
        return out

    def scores_pairwise():
        out = []
        for e in pre:
            fwd, q, k, cum = e["fwd"], e["q"], e["k"], e["cum"]
            scores = jnp.zeros((LANES, LANES), F32)
            for d in range(HGRN_CHUNK):
                if d == 0:
                    kr, cr = k, cum
                else:
                    sh = d if fwd else LANES - d
                    kr = pltpu.roll(k, sh, 0)
                    cr = pltpu.roll(cum, sh, 0)
                valid = (pos >= d) if fwd else (pos <= HGRN_CHUNK - 1 - d)
                ex = jnp.exp(jnp.where(valid, cum - cr, 0.0))
                s = jnp.sum(q * kr * ex, axis=-1, keepdims=True)
                tgt = (c_i == r_i - d) if fwd else (c_i == r_i + d)
                scores = scores + jnp.where(tgt & valid, s, 0.0)
            out.append(scores)
        return out

    lowest = pre[0]["ctot"]
    for e in pre[1:]:
        lowest = jnp.minimum(lowest, e["ctot"])
    scores = lax.cond(jnp.min(lowest) >= -2.0 * HGRN_SAFE_EXPONENT, scores_factored, scores_pairwise)

    chunk_of_lane = c_i >> HGRN_CHUNK_LOG2
    out = []
    for e, sc in zip(pre, scores):
        o_intra = _bdot(sc, e["v"])
        v_t = e["v"].T
        lhs = jnp.concatenate(
            [jnp.where(chunk_of_lane == c, v_t, 0.0) for c in range(LANES // HGRN_CHUNK)], axis=0)
        upd = _bdot(lhs, e["k_out"])
        out.append((e["q_in"], o_intra, upd, e["dec"]))
    return out


def _hgrn_rec_kernel(*refs, seq_len, groups, zero_init, want_state, aliased):
    qf_ref, zf_ref, vf_ref, qb_ref, zb_ref, vb_ref, gate_ref, lb_ref, ng_ref = refs[:9]
    pos = 9
    s0_ref = None
    if not zero_init:
        s0_ref = refs[pos]
        pos += 1
    pos += aliased
    o_ref = refs[pos]
    pos += 1
    sfin_ref = None
    if want_state:
        sfin_ref = refs[pos]
        pos += 1
    osum_ref, st_ref = refs[pos:pos + 2]

    n_tiles = seq_len // LANES
    n_chunks = LANES // HGRN_CHUNK
    unroll = min(8 // (2 * groups), n_tiles)
    assert n_tiles % unroll == 0 and (n_tiles == unroll or (n_tiles // 2) % unroll == 0)
    r_i = lax.broadcasted_iota(jnp.int32, (LANES, LANES), 0)
    c_i = lax.broadcasted_iota(jnp.int32, (LANES, LANES), 1)
    lb = lb_ref[...]
    data = ((qf_ref, zf_ref, vf_ref), (qb_ref, zb_ref, vb_ref))

    def lanes_of(grp):
        return slice(grp * LANES, (grp + 1) * LANES)

    for z in range(2):
        for grp in range(groups):
            if zero_init:
                st_ref[z * groups + grp] = jnp.zeros((LANES, LANES), F32)
            else:
                st_ref[z * groups + grp] = s0_ref[z, grp].T

    def finish(rows, cols, o):
        o = o * lax.rsqrt(jnp.mean(o * o, -1, keepdims=True) + RMS_EPS)
        g = gate_ref[rows, cols]
        o_ref[rows, cols] = (o * ng_ref[:, cols] * (g * _sigmoid(g))).astype(o_ref.dtype)

    def block(i, visit):
        slots = []
        for z in range(2):
            for grp in range(groups):
                for u in range(unroll):
                    t = i * unroll + u
                    slots.append((z, u, grp, t if z == 0 else n_tiles - 1 - t))
        entries = []
        for z, _, grp, t in slots:
            rows = pl.ds(pl.multiple_of(t * LANES, LANES), LANES)
            cols = lanes_of(grp)
            q_ref, z_ref, v_ref = data[z]
            entries.append((q_ref[rows, cols], z_ref[rows, cols], v_ref[rows, cols], lb[z:z + 1, cols], z == 0))
        pre = dict(zip([s[:3] for s in slots], _hgrn_precompute(entries, r_i, c_i)))

        st = [st_ref[c] for c in range(2 * groups)]
        inter = {key: [None] * n_chunks for key in pre}
        for u in range(unroll):
            for step in range(n_chunks):
                for z in range(2):
                    for grp in range(groups):
                        q_in, _, upd, dec = pre[(z, u, grp)]
                        ch = z * groups + grp
                        c = step if z == 0 else n_chunks - 1 - step
                        lo = c * HGRN_CHUNK
                        inter[(z, u, grp)][c] = _bdot(q_in[lo:lo + HGRN_CHUNK], st[ch], NT_DIMS)
                        st[ch] = st[ch] * dec[lo:lo + 1] + upd[c * LANES:(c + 1) * LANES]
        for ch in range(2 * groups):
            st_ref[ch] = st[ch]
        outs = {key: pre[key][1] + jnp.concatenate(inter[key], axis=0) for key in pre}

        if visit == "both":
            for t in range(n_tiles):
                for grp in range(groups):
                    finish(pl.ds(t * LANES, LANES), lanes_of(grp),
                           outs[(0, t, grp)] + outs[(1, n_tiles - 1 - t, grp)])
            return
        for z, u, grp, t in slots:
            rows, cols = pl.ds(pl.multiple_of(t * LANES, LANES), LANES), lanes_of(grp)
            if visit == "first":
                osum_ref[rows, cols] = outs[(z, u, grp)]
            else:
                finish(rows, cols, osum_ref[rows, cols] + outs[(z, u, grp)])

    if n_tiles == unroll:
        block(0, "both")
    else:
        def body(i, carry, visit):
            block(i, visit)
            return carry

        half = n_tiles // 2 // unroll
        lax.fori_loop(0, half, functools.partial(body, visit="first"), 0)
        lax.fori_loop(half, 2 * half, functools.partial(body, visit="second"), 0)
    if want_state:
        for z in range(2):
            for grp in range(groups):
                sfin_ref[z, grp] = st_ref[z * groups + grp].T


def _hgrn_rec(proj, lb, norm_g, s0, prev_out, *, n_seq, seq_len, row_block0, n_heads, total_rows,
              states_out=None, layer=0, n_layers=1):
    d = n_heads * LANES
    groups = 2 if seq_len // LANES <= 2 else 1
    n_cols = n_heads // groups
    zero_init = s0 is None
    want_state = s0 is None
    aliased = prev_out is not None

    def col(block):
        return lambda b, h: (row_block0 + b, block * n_cols + h)

    blk = (seq_len, groups * LANES)
    in_specs = [pl.BlockSpec(blk, col(0)), pl.BlockSpec(blk, col(1)), pl.BlockSpec(blk, col(2)),
                pl.BlockSpec(blk, col(3)), pl.BlockSpec(blk, col(4)), pl.BlockSpec(blk, col(5)),
                pl.BlockSpec(blk, col(6)),
                pl.BlockSpec((2, groups * LANES), lambda b, h: (0, h)),
                pl.BlockSpec((1, groups * LANES), lambda b, h: (0, h))]
    args = [proj] * 7 + [lb, norm_g.reshape(1, d)]
    state_spec = pl.BlockSpec((None, 2, groups, LANES, LANES), lambda b, h: (b, 0, h, 0, 0))
    if not zero_init:
        in_specs.append(state_spec)
        args.append(s0)
    io_alias = {}
    if aliased:
        in_specs.append(pl.BlockSpec(memory_space=pl.ANY))
        io_alias[len(args)] = 0
        args.append(prev_out)
    if states_out is not None:
        in_specs.append(pl.BlockSpec(memory_space=pl.ANY))
        io_alias[len(args)] = 1
        args.append(states_out)
    out_specs = [pl.BlockSpec(blk, lambda b, h: (row_block0 + b, h))]
    out_shape = [jax.ShapeDtypeStruct((total_rows, d), BF16)]
    if want_state:
        out_specs.append(pl.BlockSpec((None, None, 2, groups, LANES, LANES), lambda b, h: (b, layer, 0, h, 0, 0)))
        out_shape.append(jax.ShapeDtypeStruct((n_seq, n_layers, 2, n_heads, LANES, LANES), F32))
    res = pl.pallas_call(
        functools.partial(_hgrn_rec_kernel, seq_len=seq_len, groups=groups, zero_init=zero_init,
                          want_state=want_state, aliased=len(io_alias)),
        grid=(n_seq, n_cols),
        in_specs=in_specs,
        out_specs=out_specs,
        out_shape=out_shape,
        scratch_shapes=[pltpu.VMEM((seq_len, groups * LANES), F32),
                        pltpu.VMEM((2 * groups, LANES, LANES), F32)],
        input_output_aliases=io_alias,
        compiler_params=pltpu.CompilerParams(
            dimension_semantics=("parallel", "parallel"), vmem_limit_bytes=VMEM_LIMIT),
        name="hgrn_rec",
    )(*args)
    return res


def _int_mod(x, n):
    return x & (n - 1) if n & (n - 1) == 0 else lax.rem(x, n)


def _token_shift(x_ref, xp_ref, xn_ref, mod_ref, tile_start, n_prompt_rows, prompt_len, sample_len):
    m = mod_ref[...]
    sh, sc = m[0:1], 1.0 + m[1:2]
    h = x_ref[...] * sc + sh
    h_before = xp_ref[7:8, :] * sc + sh
    h_after = xn_ref[0:1, :] * sc + sh
    tm = h.shape[0]
    rr = lax.broadcasted_iota(jnp.int32, (tm, 1), 0)
    grow = tile_start + rr
    in_prompt = grow < n_prompt_rows
    pos = jnp.where(in_prompt, _int_mod(grow, prompt_len), _int_mod(grow - n_prompt_rows, sample_len))
    last = jnp.where(in_prompt, prompt_len - 1, sample_len - 1)
    prev = jnp.where(rr == 0, h_before, pltpu.roll(h, 1, 0))
    prev = jnp.where(pos == 0, 0.0, prev)
    nxt = jnp.where(rr == tm - 1, h_after, pltpu.roll(h, tm - 1, 0))
    nxt = jnp.where(pos == last, 0.0, nxt)
    return h, 0.5 * (prev + nxt) - h


def _rwkv_rkv_kernel(x_ref, xp_ref, xn_ref, mod_ref, mu_ref, w_ref, o_ref, h_ref, xx_ref, *, tm, seq_info):
    @pl.when(pl.program_id(1) == 0)
    def _():
        h, xx = _token_shift(x_ref, xp_ref, xn_ref, mod_ref, pl.program_id(0) * tm, *seq_info)
        h_ref[...] = h
        xx_ref[...] = xx

    xs = h_ref[...] + xx_ref[...] * mu_ref[...]
    o_ref[...] = _dot(xs.astype(BF16), w_ref[pl.program_id(1)])


def _halo_specs(tm, d, t, n_grid_axes):
    nb = t // 8

    def before(i, *_):
        return (jnp.maximum(i * (tm // 8) - 1, 0), 0)

    def after(i, *_):
        return (jnp.minimum((i + 1) * (tm // 8), nb - 1), 0)

    del n_grid_axes
    return pl.BlockSpec((8, d), before), pl.BlockSpec((8, d), after)


def _rwkv_rkv(x, mods, mu3, w_rkv, seg, *, n_prompt_rows, prompt_len, sample_len, tm=512):
    t, d = x.shape
    n = w_rkv.shape[2]
    before, after = _halo_specs(tm, d, t, 2)
    return pl.pallas_call(
        functools.partial(_rwkv_rkv_kernel, tm=tm, seq_info=(n_prompt_rows, prompt_len, sample_len)),
        grid=(t // tm, 3),
        in_specs=[
            pl.BlockSpec((tm, d), lambda i, j: (i, 0)),
            before, after,
            pl.BlockSpec((None, 6, d), lambda i, j: (seg(i, tm), 0, 0)),
            pl.BlockSpec((None, 1, d), lambda i, j: (j, 0, 0)),
            pl.BlockSpec((3, d, n), lambda i, j: (0, 0, 0)),
        ],
        out_specs=pl.BlockSpec((None, tm, n), lambda i, j: (j, i, 0)),
        out_shape=jax.ShapeDtypeStruct((3, t, n), F32),
        scratch_shapes=[pltpu.VMEM((tm, d), F32), pltpu.VMEM((tm, d), F32)],
        compiler_params=pltpu.CompilerParams(
            dimension_semantics=("parallel", "arbitrary"), vmem_limit_bytes=VMEM_LIMIT),
        name="rwkv_rkv",
    )(x, x, x, mods, mu3, w_rkv)


def _rwkv_lora_kernel(x_ref, xp_ref, xn_ref, mod_ref, mu_ref, wla_ref, wlb_ref, w0_ref, ala_ref, alb_ref,
                      a0_ref, gla_ref, glb_ref, lw_ref, a_ref, g_ref, *, tm, seq_info):
    h, xx = _token_shift(x_ref, xp_ref, xn_ref, mod_ref, pl.program_id(0) * tm, *seq_info)
    mu = mu_ref[...]
    xs_w = (h + xx * mu[1:2]).astype(BF16)
    xs_a = (h + xx * mu[4:5]).astype(BF16)
    xs_g = (h + xx * mu[5:6]).astype(BF16)
    zw = w0_ref[...] + _bdot(jnp.tanh(_dot(xs_w, wla_ref[...])), wlb_ref[...])
    lw_ref[...] = -DECAY_SCALE * _sigmoid(zw)
    za = a0_ref[...] + _bdot(_dot(xs_a, ala_ref[...]), alb_ref[...])
    a_ref[...] = _sigmoid(za)
    g_ref[...] = _bdot(_sigmoid(_dot(xs_g, gla_ref[...])), glb_ref[...])


def _rwkv_lora(x, mods, mu, weights, seg, *, n_prompt_rows, prompt_len, sample_len, tm=256):
    t, d = x.shape
    before, after = _halo_specs(tm, d, t, 1)

    def whole(arr):
        return pl.BlockSpec(arr.shape, lambda i: (0,) * arr.ndim)

    return pl.pallas_call(
        functools.partial(_rwkv_lora_kernel, tm=tm, seq_info=(n_prompt_rows, prompt_len, sample_len)),
        grid=(t // tm,),
        in_specs=[pl.BlockSpec((tm, d), lambda i: (i, 0)), before, after,
                  pl.BlockSpec((None, 6, d), lambda i: (seg(i, tm), 0, 0)), whole(mu)]
                 + [whole(w) for w in weights],
        out_specs=[pl.BlockSpec((tm, 2 * d), lambda i: (i, 0)),
                   pl.BlockSpec((tm, 2 * d), lambda i: (i, 0)),
                   pl.BlockSpec((tm, d), lambda i: (i, 0))],
        out_shape=[jax.ShapeDtypeStruct((t, 2 * d), F32), jax.ShapeDtypeStruct((t, 2 * d), F32),
                   jax.ShapeDtypeStruct((t, d), F32)],
        compiler_params=pltpu.CompilerParams(
            dimension_semantics=("parallel",), vmem_limit_bytes=VMEM_LIMIT),
        name="rwkv_lora",
    )(x, x, x, mods, mu, *weights)


def _rwkv_precompute(entries, consts):
    bd, eye, head_masks, r_i, c_i = consts
    n = len(entries)
    pre = []
    for r, k, v, lw, a, (kkp, kap, rkp), fwd in entries:
        tri = ((c_i <= r_i) if fwd else (c_i >= r_i)).astype(F32)
        kk = k * kkp
        kk = kk / jnp.maximum(jnp.sqrt(_head_sums(kk * kk, head_masks)), 1e-12)
        k2 = k * (1.0 + (a - 1.0) * kap)
        bonus = _head_sums(r * k2 * rkp, head_masks) * v
        b = kk * a
        cl = _dot_x2r(tri, lw)
        cle = cl - lw
        cm = cl[LANES // 2:LANES // 2 + 1]
        ct = cl[LANES - 1:LANES] if fwd else cl[0:1]
        e_inv = jnp.exp(cm - cl)
        e_out = jnp.exp(ct - cl)
        pre.append(dict(
            v=v, bonus=bonus, fwd=fwd,
            kkt=kk * jnp.exp(cle - cm), rt=r * jnp.exp(cl - cm), kh=k2 * e_inv, bh=b * e_inv,
            kkd=kk * jnp.exp(cle), rd=r * jnp.exp(cl), kg=k2 * e_out, bg=b * e_out, gc=jnp.exp(ct)))

    for e in pre:
        lhs = jnp.concatenate([e["kkt"] * head_masks[0], e["kkt"] * head_masks[1],
                               e["rt"] * head_masks[0], e["rt"] * head_masks[1]], axis=0)
        e["gk"] = _dot_x3(lhs, e["kh"], NT_DIMS)
        e["gb"] = _dot_x3(lhs, e["bh"], NT_DIMS)

    chains = []
    for e in pre:
        strict = (c_i < r_i) if e["fwd"] else (c_i > r_i)
        incl = (c_i <= r_i) if e["fwd"] else (c_i >= r_i)
        for hd in range(2):
            lo = hd * LANES
            chains.append(dict(
                e=e, hd=hd,
                a_k=jnp.where(strict, e["gk"][lo:lo + LANES], 0.0),
                a_b=jnp.where(strict, e["gb"][lo:lo + LANES], 0.0),
                b_k=jnp.where(incl, e["gk"][2 * LANES + lo:3 * LANES + lo], 0.0),
                b_b=jnp.where(incl, e["gb"][2 * LANES + lo:3 * LANES + lo], 0.0)))

    for c in chains:
        c["m"] = eye - jnp.where((r_i >> 1) == (c_i >> 1), c["a_b"], 0.0)
    for log_k in range(1, 7):
        k = 1 << log_k
        off = ((r_i >> (log_k + 1)) == (c_i >> (log_k + 1))) & ((r_i >> log_k) != (c_i >> log_k))
        if k < 8:
            for c in chains:
                c["t"] = _dot_x3(jnp.where(off, c["a_b"], 0.0), c["m"])
            for c in chains:
                c["m"] = c["m"] - _dot_x3(c["m"], c["t"])
            continue

        def take(x, fwd):
            first = k if fwd else 0
            return jnp.concatenate([x[lo:lo + k] for lo in range(first, LANES, 2 * k)], axis=0)

        def spread(rows, fwd):
            zero = jnp.zeros((k, LANES), F32)
            parts = []
            for j in range(LANES // (2 * k)):
                piece = rows[j * k:(j + 1) * k]
                parts += [zero, piece] if fwd else [piece, zero]
            return jnp.concatenate(parts, axis=0)

        for c in chains:
            fwd = c["e"]["fwd"]
            c["t"] = spread(_dot_x3(take(jnp.where(off, c["a_b"], 0.0), fwd), c["m"]), fwd)
        for c in chains:
            fwd = c["e"]["fwd"]
            c["m"] = c["m"] - spread(_dot_x3(take(c["m"], fwd), c["t"]), fwd)

    for c in chains:
        e = c["e"]
        c["v_h"] = e["v"] * head_masks[c["hd"]]
        c["kkd_h"] = e["kkd"] * head_masks[c["hd"]]
        c["akv"] = _dot_x3(c["a_k"], c["v_h"])
    for c in chains:
        c["wt"] = _dot_x3(c["m"], c["kkd_h"])
        c["u0"] = _dot_x3(c["m"], c["akv"])
    for c in chains:
        c["y0"] = _dot_x3(c["b_k"], c["v_h"]) - _dot_x3(c["b_b"], c["u0"])
        c["rp"] = _dot_x3(c["b_b"], c["wt"])

    out = []
    for i, e in enumerate(pre):
        c0, c1 = chains[2 * i], chains[2 * i + 1]
        wt = c0["wt"] + c1["wt"]
        u0 = c0["u0"] + c1["u0"]
        q0 = bd * (_dot_x3(e["v"], e["kg"], TN_DIMS) - _dot_x3(u0.T, e["bg"]))
        p_mat = _dot_x3(wt, e["bg"], TN_DIMS)
        out.append((e["rd"] - c0["rp"] - c1["rp"], c0["y0"] + c1["y0"], e["gc"], q0, p_mat, e["bonus"]))
    assert len(out) == n
    return out


def _rwkv_rec_kernel(*refs, seq_len, groups, zero_init, want_state, aliased):
    data = (refs[0:5], refs[5:10])
    g_ref, kk_ref, ka_ref, rk_ref, gg_ref, gb_ref = refs[10:16]
    pos = 16
    s0_ref = None
    if not zero_init:
        s0_ref = refs[pos]
        pos += 1
    pos += aliased
    o_ref = refs[pos]
    pos += 1
    sfin_ref = None
    if want_state:
        sfin_ref = refs[pos]
        pos += 1
    osum_ref, st_ref = refs[pos:pos + 2]

    n_tiles = seq_len // LANES
    unroll = min(8 // (2 * groups), n_tiles)
    assert n_tiles % unroll == 0 and (n_tiles == unroll or (n_tiles // 2) % unroll == 0)
    r_i = lax.broadcasted_iota(jnp.int32, (LANES, LANES), 0)
    c_i = lax.broadcasted_iota(jnp.int32, (LANES, LANES), 1)
    bd = ((r_i >> 6) == (c_i >> 6)).astype(F32)
    eye = (r_i == c_i).astype(F32)
    lane = lax.broadcasted_iota(jnp.int32, (1, LANES), 1)
    head_masks = ((lane < RWKV_HEAD).astype(F32), (lane >= RWKV_HEAD).astype(F32))
    consts = (bd, eye, head_masks, r_i, c_i)
    inv_n = 1.0 / RWKV_HEAD

    def lanes_of(grp):
        return slice(grp * LANES, (grp + 1) * LANES)

    for z in range(2):
        for grp in range(groups):
            if zero_init:
                st_ref[z * groups + grp] = jnp.zeros((LANES, LANES), F32)
            else:
                zero = jnp.zeros((RWKV_HEAD, RWKV_HEAD), F32)
                st_ref[z * groups + grp] = jnp.concatenate(
                    [jnp.concatenate([s0_ref[z, 2 * grp], zero], axis=1),
                     jnp.concatenate([zero, s0_ref[z, 2 * grp + 1]], axis=1)], axis=0)

    def block(i, visit):
        slots = []
        for z in range(2):
            for grp in range(groups):
                for u in range(unroll):
                    t = i * unroll + u
                    slots.append((z, u, grp, t if z == 0 else n_tiles - 1 - t))
        entries = []
        for z, _, grp, t in slots:
            rows = pl.ds(pl.multiple_of(t * LANES, LANES), LANES)
            cols = lanes_of(grp)
            r_ref, k_ref, v_ref, lw_ref, a_ref = data[z]
            params = (kk_ref[z:z + 1, cols], ka_ref[z:z + 1, cols], rk_ref[z:z + 1, cols])
            entries.append((r_ref[rows, cols], k_ref[rows, cols], v_ref[rows, cols], lw_ref[rows, cols],
                            a_ref[rows, cols], params, z == 0))
        pre = _rwkv_precompute(entries, consts)

        ys = []
        st = [st_ref[c] for c in range(2 * groups)]
        for (z, _, grp, _), (rp, y0, gc, q0, p_mat, _) in zip(slots, pre):
            c = z * groups + grp
            ys.append(_dot_x3(rp, st[c], NT_DIMS) + y0)
            st[c] = st[c] * gc + q0 - bd * _dot_x3(st[c], p_mat)
        for c in range(2 * groups):
            st_ref[c] = st[c]

        outs = {}
        for (z, u, grp, _), y, (_, _, _, _, _, bonus) in zip(slots, ys, pre):
            cols = lanes_of(grp)
            mean = _head_sums(y, head_masks) * inv_n
            yc = y - mean
            var = _head_sums(yc * yc, head_masks) * inv_n
            outs[(z, u, grp)] = (yc * lax.rsqrt(var + GN_EPS) * gg_ref[z:z + 1, cols] + gb_ref[z:z + 1, cols]
                                 + bonus)

        if visit == "both":
            for t in range(n_tiles):
                for grp in range(groups):
                    rows, cols = pl.ds(t * LANES, LANES), lanes_of(grp)
                    total = outs[(0, t, grp)] + outs[(1, n_tiles - 1 - t, grp)]
                    o_ref[rows, cols] = (total * g_ref[rows, cols]).astype(o_ref.dtype)
            return
        for z, u, grp, t in slots:
            out = outs[(z, u, grp)]
            rows, cols = pl.ds(pl.multiple_of(t * LANES, LANES), LANES), lanes_of(grp)
            if visit == "first":
                osum_ref[rows, cols] = out
            else:
                o_ref[rows, cols] = ((osum_ref[rows, cols] + out) * g_ref[rows, cols]).astype(o_ref.dtype)

    if n_tiles == unroll:
        block(0, "both")
    else:
        def body(i, carry, visit):
            block(i, visit)
            return carry

        half = n_tiles // 2 // unroll
        lax.fori_loop(0, half, functools.partial(body, visit="first"), 0)
        lax.fori_loop(half, 2 * half, functools.partial(body, visit="second"), 0)
    if want_state:
        for z in range(2):
            for grp in range(groups):
                st = st_ref[z * groups + grp]
                sfin_ref[z, 2 * grp] = st[:RWKV_HEAD, :RWKV_HEAD]
                sfin_ref[z, 2 * grp + 1] = pltpu.roll(st, RWKV_HEAD, 1)[RWKV_HEAD:, :RWKV_HEAD]


def _rwkv_rec(rkv, lw, a, g, params, s0, prev_out, *, n_seq, seq_len, row_block0, total_rows,
              states_out=None, layer=0, n_layers=1):
    d = g.shape[1]
    groups = 2 if seq_len // LANES <= 2 else 1
    n_cols = d // (groups * LANES)
    zero_init = s0 is None
    want_state = s0 is None
    aliased = prev_out is not None
    blk = (seq_len, groups * LANES)

    def dir_specs(z):
        col = lambda b, p: (row_block0 + b, z * n_cols + p)
        return ([pl.BlockSpec((None,) + blk, lambda b, p, j=j: (j, row_block0 + b, z * n_cols + p))
                 for j in range(3)] + [pl.BlockSpec(blk, col), pl.BlockSpec(blk, col)])

    in_specs = dir_specs(0) + dir_specs(1)
    in_specs.append(pl.BlockSpec(blk, lambda b, p: (row_block0 + b, p)))
    in_specs += [pl.BlockSpec((2, groups * LANES), lambda b, p: (0, p)) for _ in range(5)]
    args = [rkv, rkv, rkv, lw, a] * 2 + [g] + list(params)
    if not zero_init:
        in_specs.append(pl.BlockSpec((None, 2, 2 * groups, RWKV_HEAD, RWKV_HEAD), lambda b, p: (b, 0, p, 0, 0)))
        args.append(s0)
    io_alias = {}
    if aliased:
        in_specs.append(pl.BlockSpec(memory_space=pl.ANY))
        io_alias[len(args)] = 0
        args.append(prev_out)
    if states_out is not None:
        in_specs.append(pl.BlockSpec(memory_space=pl.ANY))
        io_alias[len(args)] = 1
        args.append(states_out)
    out_specs = [pl.BlockSpec(blk, lambda b, p: (row_block0 + b, p))]
    out_shape = [jax.ShapeDtypeStruct((total_rows, d), BF16)]
    if want_state:
        out_specs.append(pl.BlockSpec((None, None, 2, 2 * groups, RWKV_HEAD, RWKV_HEAD),
                                      lambda b, p: (b, layer, 0, p, 0, 0)))
        out_shape.append(jax.ShapeDtypeStruct((n_seq, n_layers, 2, d // RWKV_HEAD, RWKV_HEAD, RWKV_HEAD), F32))
    return pl.pallas_call(
        functools.partial(_rwkv_rec_kernel, seq_len=seq_len, groups=groups, zero_init=zero_init,
                          want_state=want_state, aliased=len(io_alias)),
        grid=(n_seq, n_cols),
        in_specs=in_specs,
        out_specs=out_specs,
        out_shape=out_shape,
        scratch_shapes=[pltpu.VMEM((seq_len, groups * LANES), F32),
                        pltpu.VMEM((2 * groups, LANES, LANES), F32)],
        input_output_aliases=io_alias,
        compiler_params=pltpu.CompilerParams(
            dimension_semantics=("parallel", "parallel"), vmem_limit_bytes=VMEM_LIMIT),
        name="rwkv_rec",
    )(*args)


def _rwkv_prepare_weights(mu, w_rkv, w0, w_la, w_lb, a0, a_la, a_lb, g_la, g_lb):
    d = mu.shape[1]
    rank_w = w_la.shape[2]
    rank_a = a_la.shape[2]
    rank_g = g_la.shape[1]
    rank_g_pad = -(-rank_g // LANES) * LANES

    def block_diag(w):
        rank = w.shape[1]
        out = jnp.zeros((2, rank, 2, d), w.dtype)
        out = out.at[0, :, 0, :].set(w[0]).at[1, :, 1, :].set(w[1])
        return out.reshape(2 * rank, 2 * d)

    lora = (
        w_la.reshape(d, 2 * rank_w).astype(BF16), block_diag(w_lb).astype(BF16), w0.reshape(1, 2 * d),
        a_la.reshape(d, 2 * rank_a).astype(BF16), block_diag(a_lb).astype(BF16), a0.reshape(1, 2 * d),
        jnp.pad(g_la, ((0, 0), (0, rank_g_pad - rank_g))).astype(BF16),
        jnp.pad(g_lb, ((0, rank_g_pad - rank_g), (0, 0))).astype(BF16),
    )
    mu3 = jnp.stack([mu[0], mu[2], mu[3]]).reshape(3, 1, d)
    return {"mu3": mu3, "w_rkv": w_rkv.astype(BF16), "lora": lora}


def _embed_kernel(xp_ref, xs_ref, row_ref, col_ref, o_ref, *, n_prompt_tiles):
    i = pl.program_id(0)

    @pl.when(i < n_prompt_tiles)
    def _():
        o_ref[...] = xp_ref[...]

    @pl.when(i >= n_prompt_tiles)
    def _():
        half = o_ref.shape[1] // 2
        for grp in range(o_ref.shape[0] // GRID_W):
            rows = pl.ds(grp * GRID_W, GRID_W)
            o_ref[rows, :half] = xs_ref[rows, :half] + row_ref[grp:grp + 1, :]
            o_ref[rows, half:] = xs_ref[rows, half:] + col_ref[...]


def _grid_pos_tables(n_tokens, d):
    quarter = d // 4
    omega = 1.0 / (POS_BASE ** (jnp.arange(quarter, dtype=F32) / quarter))
    r = jnp.arange(n_tokens // GRID_W, dtype=F32)[:, None] * omega
    cc = jnp.arange(GRID_W, dtype=F32)[:, None] * omega
    return (jnp.concatenate([jnp.sin(r), jnp.cos(r)], -1), jnp.concatenate([jnp.sin(cc), jnp.cos(cc)], -1))


def _embed(xp, xs, sample_len, tm=512):
    n_p, d = xp.shape
    n_s = xs.shape[0]
    npt = n_p // tm
    pos_tiles = sample_len // tm
    grid_rows = tm // GRID_W
    row_emb, col_emb = _grid_pos_tables(sample_len, d)
    return pl.pallas_call(
        functools.partial(_embed_kernel, n_prompt_tiles=npt),
        grid=((n_p + n_s) // tm,),
        in_specs=[
            pl.BlockSpec((tm, d), lambda i: (jnp.minimum(i, npt - 1), 0)),
            pl.BlockSpec((tm, d), lambda i: (jnp.maximum(i - npt, 0), 0)),
            pl.BlockSpec((grid_rows, d // 2), lambda i: (lax.rem(jnp.maximum(i - npt, 0), pos_tiles), 0)),
            pl.BlockSpec((GRID_W, d // 2), lambda i: (0, 0)),
        ],
        out_specs=pl.BlockSpec((tm, d), lambda i: (i, 0)),
        out_shape=jax.ShapeDtypeStruct((n_p + n_s, d), F32),
        compiler_params=pltpu.CompilerParams(
            dimension_semantics=("parallel",), vmem_limit_bytes=VMEM_LIMIT),
        name="embed",
    )(xp, xs, row_emb, col_emb)


def kernel(x_prompt, x_sample, state_hgrn, state_rwkv, c, c_ctx, ada_w, ada_b, ln_g, ln_b, ffn_w_up, ffn_w_down, hgrn_w_in, hgrn_lb, hgrn_norm_g, hgrn_w_o, rwkv_mu, rwkv_w_rkv, rwkv_w0, rwkv_w_la, rwkv_w_lb, rwkv_a0, rwkv_a_la, rwkv_a_lb, rwkv_g_la, rwkv_g_lb, rwkv_k_k, rwkv_k_a, rwkv_r_k, rwkv_gn_g, rwkv_gn_b, rwkv_w_o):
    n_b, l_p, d = x_prompt.shape
    n_s, l_s, _ = x_sample.shape
    depth = ada_w.shape[0]
    n_p_rows = n_b * l_p
    total = n_p_rows + n_s * l_s
    assert n_p_rows % l_s == 0 and l_p % LANES == 0 and l_s % LANES == 0
    alpha = (2 * depth) ** 0.25
    a_heads = d // LANES

    def seg(i, tm):
        return _seg_index(i, tm, n_p_rows, l_s)

    x = _embed(x_prompt.reshape(n_p_rows, d), x_sample.reshape(n_s * l_s, d), l_s)

    cond8 = jnp.zeros((8, d), F32).at[0].set(c_ctx).at[1:1 + n_s].set(c)
    mods = _adaln(cond8, ada_w, ada_b).reshape(depth, 8, 6, d)

    lb_soft = jax.nn.softmax(hgrn_lb.astype(F32), axis=0)
    lower_bounds = jnp.cumsum(lb_soft, axis=0) - lb_soft[0]

    new_hgrn = None
    new_rwkv = None
    n_mix = (depth + 1) // 2, depth // 2
    for l in range(depth):
        j = l // 2
        if l % 2 == 0:
            proj = _modmm(x, mods[l], hgrn_w_in[j].astype(BF16), seg)
            o, new_hgrn = _hgrn_rec(proj, lower_bounds[j], hgrn_norm_g[j], None, None, n_seq=n_b, seq_len=l_p,
                                    row_block0=0, n_heads=a_heads, total_rows=total,
                                    states_out=new_hgrn, layer=j, n_layers=n_mix[0])
            (o,) = _hgrn_rec(proj, lower_bounds[j], hgrn_norm_g[j], state_hgrn[:, j], o, n_seq=n_s, seq_len=l_s,
                             row_block0=n_p_rows // l_s, n_heads=a_heads, total_rows=total)
            w_o = hgrn_w_o[j]
        else:
            prep = _rwkv_prepare_weights(rwkv_mu[j], rwkv_w_rkv[j], rwkv_w0[j], rwkv_w_la[j], rwkv_w_lb[j],
                                         rwkv_a0[j], rwkv_a_la[j], rwkv_a_lb[j], rwkv_g_la[j], rwkv_g_lb[j])
            seq_kw = dict(n_prompt_rows=n_p_rows, prompt_len=l_p, sample_len=l_s)
            rkv = _rwkv_rkv(x, mods[l], prep["mu3"], prep["w_rkv"], seg, **seq_kw)
            lw, a, g = _rwkv_lora(x, mods[l], rwkv_mu[j], prep["lora"], seg, **seq_kw)
            params = (rwkv_k_k[j], rwkv_k_a[j], rwkv_r_k[j], rwkv_gn_g[j], rwkv_gn_b[j])
            o, new_rwkv = _rwkv_rec(rkv, lw, a, g, params, None, None, n_seq=n_b, seq_len=l_p,
                                    row_block0=0, total_rows=total,
                                    states_out=new_rwkv, layer=j, n_layers=n_mix[1])
            (o,) = _rwkv_rec(rkv, lw, a, g, params, state_rwkv[:, j], o, n_seq=n_s, seq_len=l_s,
                             row_block0=n_p_rows // l_s, total_rows=total)
            w_o = rwkv_w_o[j]
        x = _post_mixer(o, x, mods[l], w_o.astype(BF16), ffn_w_up[l].astype(BF16), ffn_w_down[l].astype(BF16),
                        ln_g[l], ln_b[l], seg, alpha)

    y_prompt = x[:n_p_rows].reshape(n_b, l_p, d)
    y_sample = x[n_p_rows:].reshape(n_s, l_s, d)
    return (y_prompt, y_sample, new_hgrn, new_rwkv)
```

```python
import functools

import jax
import jax.numpy as jnp
from jax import lax
from jax.experimental import pallas as pl
from jax.experimental.pallas import tpu as pltpu

F32 = jnp.float32
BF16 = jnp.bfloat16

LN_EPS = 1e-5
RMS_EPS = 1e-6
GN_EPS = 64e-5
DECAY_SCALE = 0.606531
EXP_CLIP = 80.0
POS_BASE = 10000.0
GRID_W = 64

LANES = 128
HGRN_CHUNK = 32
HGRN_CHUNK_LOG2 = HGRN_CHUNK.bit_length() - 1
HGRN_SAFE_EXPONENT = 75.0
RWKV_HEAD = 64
VMEM_LIMIT = 56 * 1024 * 1024

NT_DIMS = (((1,), (1,)), ((), ()))
TN_DIMS = (((0,), (0,)), ((), ()))


def _dot(a, b, dims=None):
    if dims is None:
        return jnp.dot(a, b, preferred_element_type=F32)
    return lax.dot_general(a, b, dims, preferred_element_type=F32)


def _bdot(a, b, dims=None):
    return _dot(a.astype(BF16), b.astype(BF16), dims)


def _split_bf16(x):
    hi = x.astype(BF16)
    return hi, (x - hi.astype(F32)).astype(BF16)


def _dot_x3(a, b, dims=None):
    ah, al = _split_bf16(a)
    bh, bl = _split_bf16(b)
    if dims == NT_DIMS:
        a_cat = jnp.concatenate([ah, al], axis=1)
        b_half = jnp.concatenate([bh, bl], axis=0)
        b_cat = jnp.concatenate([b_half, b_half], axis=1)
        n = b.shape[0]
    else:
        a_cat = jnp.concatenate([ah, al], axis=0 if dims == TN_DIMS else 1)
        b_half = jnp.concatenate([bh, bl], axis=1)
        b_cat = jnp.concatenate([b_half, b_half], axis=0)
        n = b.shape[1]
    r = _dot(a_cat, b_cat, dims)
    return r[:, :n] + r[:, n:]


def _head_sums(x, head_masks):
    m0, m1 = head_masks
    s0 = jnp.sum(x * m0, axis=-1, keepdims=True)
    s1 = jnp.sum(x * m1, axis=-1, keepdims=True)
    return s0 * m0 + s1 * m1


def _dot_x2r(a, b):
    bh, bl = _split_bf16(b)
    n = b.shape[1]
    r = _dot(a.astype(BF16), jnp.concatenate([bh, bl], axis=1))
    return r[:, :n] + r[:, n:]


def _sigmoid(x):
    return jax.nn.sigmoid(x)


def _layer_norm(x, g, b):
    mu = jnp.mean(x, -1, keepdims=True)
    xc = x - mu
    var = jnp.mean(xc * xc, -1, keepdims=True)
    return xc * lax.rsqrt(var + LN_EPS) * g + b


def _seg_index(i, tm, n_prompt_rows, sample_len):
    start = i * tm
    return jnp.where(start < n_prompt_rows, 0, 1 + (start - n_prompt_rows) // sample_len)


def _adaln_kernel(c_ref, w_ref, b_ref, o_ref):
    c = c_ref[...]
    s = c * _sigmoid(c)
    o_ref[...] = _bdot(s, w_ref[...]) + b_ref[...]


def _adaln(cond8, ada_w, ada_b, tn=1536):
    depth, d, n = ada_w.shape
    return pl.pallas_call(
        _adaln_kernel,
        grid=(depth, n // tn),
        in_specs=[
            pl.BlockSpec((8, d), lambda l, j: (0, 0)),
            pl.BlockSpec((None, d, tn), lambda l, j: (l, 0, j)),
            pl.BlockSpec((None, 1, tn), lambda l, j: (l, 0, j)),
        ],
        out_specs=pl.BlockSpec((None, 8, tn), lambda l, j: (l, 0, j)),
        out_shape=jax.ShapeDtypeStruct((depth, 8, n), F32),
        compiler_params=pltpu.CompilerParams(
            dimension_semantics=("parallel", "parallel"), vmem_limit_bytes=VMEM_LIMIT),
        name="adaln",
    )(cond8, ada_w, ada_b.reshape(depth, 1, n))


def _modmm_kernel(x_ref, mod_ref, w_ref, o_ref, xb_ref):
    j = pl.program_id(1)

    @pl.when(j == 0)
    def _():
        m = mod_ref[...]
        xb_ref[...] = (x_ref[...] * (1.0 + m[1:2]) + m[0:1]).astype(BF16)

    o_ref[...] = _dot(xb_ref[...], w_ref[j])


def _modmm(x, mods, w, seg, tm=1024, tn=1024):
    t, d = x.shape
    n = w.shape[1]
    panels = w.reshape(d, n // tn, tn).transpose(1, 0, 2)
    return pl.pallas_call(
        _modmm_kernel,
        grid=(t // tm, n // tn),
        in_specs=[
            pl.BlockSpec((tm, d), lambda i, j: (i, 0)),
            pl.BlockSpec((None, 6, d), lambda i, j: (seg(i, tm), 0, 0)),
            pl.BlockSpec((n // tn, d, tn), lambda i, j: (0, 0, 0)),
        ],
        out_specs=pl.BlockSpec((tm, tn), lambda i, j: (i, j)),
        out_shape=jax.ShapeDtypeStruct((t, n), F32),
        scratch_shapes=[pltpu.VMEM((tm, d), BF16)],
        compiler_params=pltpu.CompilerParams(
            dimension_semantics=("parallel", "arbitrary"), vmem_limit_bytes=VMEM_LIMIT),
        name="modmm",
    )(x, mods, panels)


def _post_mixer_kernel(o_ref, x_ref, mod_ref, wo_ref, wu_ref, wd_ref, g_ref, b_ref,
                       y_ref, x1_ref, xb_ref, acc_ref, *, alpha):
    f = pl.program_id(1)

    @pl.when(f == 0)
    def _():
        m = mod_ref[...]
        y = _dot(o_ref[...], wo_ref[...])
        x1 = _layer_norm(alpha * x_ref[...] + m[2:3] * y, g_ref[0:1], b_ref[0:1])
        x1_ref[...] = x1
        xb_ref[...] = (x1 * (1.0 + m[4:5]) + m[3:4]).astype(BF16)
        acc_ref[...] = jnp.zeros_like(acc_ref)

    h = jnp.maximum(_dot(xb_ref[...], wu_ref[...]), 0.0)
    acc_ref[...] += _dot((h * h).astype(BF16), wd_ref[...])

    @pl.when(f == pl.num_programs(1) - 1)
    def _():
        gate = mod_ref[...][5:6]
        y_ref[...] = _layer_norm(alpha * x1_ref[...] + gate * acc_ref[...], g_ref[1:2], b_ref[1:2])


def _post_mixer(o, x, mods, w_o, w_up, w_down, ln_g, ln_b, seg, alpha, tm=1024, tf=1024):
    t, d = x.shape
    dff = w_up.shape[1]
    both = pl.BlockSpec((2, d), lambda i, f: (0, 0))
    return pl.pallas_call(
        functools.partial(_post_mixer_kernel, alpha=alpha),
        grid=(t // tm, dff // tf),
        in_specs=[
            pl.BlockSpec((tm, d), lambda i, f: (i, 0)),
            pl.BlockSpec((tm, d), lambda i, f: (i, 0)),
            pl.BlockSpec((None, 6, d), lambda i, f: (seg(i, tm), 0, 0)),
            pl.BlockSpec((d, d), lambda i, f: (0, 0)),
            pl.BlockSpec((d, tf), lambda i, f: (0, f)),
            pl.BlockSpec((tf, d), lambda i, f: (f, 0)),
            both, both,
        ],
        out_specs=pl.BlockSpec((tm, d), lambda i, f: (i, 0)),
        out_shape=jax.ShapeDtypeStruct((t, d), F32),
        scratch_shapes=[pltpu.VMEM((tm, d), F32), pltpu.VMEM((tm, d), BF16), pltpu.VMEM((tm, d), F32)],
        compiler_params=pltpu.CompilerParams(
            dimension_semantics=("parallel", "arbitrary"), vmem_limit_bytes=VMEM_LIMIT),
        name="post_mixer",
    )(o, x, mods, w_o, w_up, w_down, ln_g, ln_b)


def _hgrn_precompute(entries, r_i, c_i):
    pos = r_i & (HGRN_CHUNK - 1)
    same_chunk = (r_i >> HGRN_CHUNK_LOG2) == (c_i >> HGRN_CHUNK_LOG2)
    blk = same_chunk.astype(F32)
    pre = []
    for qr, zz, v, lbz, fwd in entries:
        causal = same_chunk & ((c_i <= r_i) if fwd else (c_i >= r_i))
        q = qr * _sigmoid(qr)
        e_abs = jnp.exp(-jnp.abs(zz))
        inv_1p = 1.0 / (1.0 + e_abs)
        log_f = jnp.minimum(zz, 0.0) + jnp.log((1.0 + lbz * jnp.exp(jnp.minimum(-zz, EXP_CLIP))) * inv_1p)
        k = (1.0 - lbz) * jnp.where(zz > 0.0, e_abs, 1.0) * inv_1p
        sums = _dot_x2r(jnp.concatenate([causal.astype(F32), blk], axis=0), log_f)
        cum = sums[:LANES]
        ctot = sums[LANES:]
        pre.append(dict(q=q, k=k, v=v, cum=cum, ctot=ctot, fwd=fwd, causal=causal,
                        q_in=q * jnp.exp(cum), k_out=k * jnp.exp(ctot - cum), dec=jnp.exp(ctot)))

    def scores_factored():
        out = []
        for e in pre:
            half = 0.5 * e["ctot"]
            qk = _dot_x3(e["q"] * jnp.exp(e["cum"] - half), e["k"] * jnp.exp(half - e["cum"]), NT_DIMS)
            out.append(jnp.where(e["causal"], qk, 0.0))
        return out

    def scores_pairwise():
        out = []
        for e in pre:
            fwd, q, k, cum = e["fwd"], e["q"], e["k"], e["cum"]
            scores = jnp.zeros((LANES, LANES), F32)
            for d in range(HGRN_CHUNK):
                if d == 0:
                    kr, cr = k, cum
                else:
                    sh = d if fwd else LANES - d
                    kr = pltpu.roll(k, sh, 0)
                    cr = pltpu.roll(cum, sh, 0)
                valid = (pos >= d) if fwd else (pos <= HGRN_CHUNK - 1 - d)
                ex = jnp.exp(jnp.where(valid, cum - cr, 0.0))
                s = jnp.sum(q * kr * ex, axis=-1, keepdims=True)
                tgt = (c_i == r_i - d) if fwd else (c_i == r_i + d)
                scores = scores + jnp.where(tgt & valid, s, 0.0)
            out.append(scores)
        return out

    lowest = pre[0]["ctot"]
    for e in pre[1:]:
        lowest = jnp.minimum(lowest, e["ctot"])
    scores = lax.cond(jnp.min(lowest) >= -2.0 * HGRN_SAFE_EXPONENT, scores_factored, scores_pairwise)

    chunk_of_lane = c_i >> HGRN_CHUNK_LOG2
    out = []
    for e, sc in zip(pre, scores):
        o_intra = _bdot(sc, e["v"])
        v_t = e["v"].T
        lhs = jnp.concatenate(
            [jnp.where(chunk_of_lane == c, v_t, 0.0) for c in range(LANES // HGRN_CHUNK)], axis=0)
        upd = _bdot(lhs, e["k_out"])
        out.append((e["q_in"], o_intra, upd, e["dec"]))
    return out


def _hgrn_rec_kernel(*refs, seq_len, groups, zero_init, want_state, aliased):
    qf_ref, zf_ref, vf_ref, qb_ref, zb_ref, vb_ref, gate_ref, lb_ref, ng_ref = refs[:9]
    pos = 9
    s0_ref = None
    if not zero_init:
        s0_ref = refs[pos]
        pos += 1
    pos += aliased
    o_ref = refs[pos]
    pos += 1
    sfin_ref = None
    if want_state:
        sfin_ref = refs[pos]
        pos += 1
    osum_ref, st_ref = refs[pos:pos + 2]

    n_tiles = seq_len // LANES
    n_chunks = LANES // HGRN_CHUNK
    unroll = min(8 // (2 * groups), n_tiles)
    assert n_tiles % unroll == 0 and (n_tiles == unroll or (n_tiles // 2) % unroll == 0)
    r_i = lax.broadcasted_iota(jnp.int32, (LANES, LANES), 0)
    c_i = lax.broadcasted_iota(jnp.int32, (LANES, LANES), 1)
    lb = lb_ref[...]
    data = ((qf_ref, zf_ref, vf_ref), (qb_ref, zb_ref, vb_ref))

    def lanes_of(grp):
        return slice(grp * LANES, (grp + 1) * LANES)

    for z in range(2):
        for grp in range(groups):
            if zero_init:
                st_ref[z * groups + grp] = jnp.zeros((LANES, LANES), F32)
            else:
                st_ref[z * groups + grp] = s0_ref[z, grp].T

    def finish(rows, cols, o):
        o = o * lax.rsqrt(jnp.mean(o * o, -1, keepdims=True) + RMS_EPS)
        g = gate_ref[rows, cols]
        o_ref[rows, cols] = (o * ng_ref[:, cols] * (g * _sigmoid(g))).astype(o_ref.dtype)

    def block(i, visit):
        slots = []
        for z in range(2):
            for grp in range(groups):
                for u in range(unroll):
                    t = i * unroll + u
                    slots.append((z, u, grp, t if z == 0 else n_tiles - 1 - t))
        entries = []
        for z, _, grp, t in slots:
            rows = pl.ds(pl.multiple_of(t * LANES, LANES), LANES)
            cols = lanes_of(grp)
            q_ref, z_ref, v_ref = data[z]
            entries.append((q_ref[rows, cols], z_ref[rows, cols], v_ref[rows, cols], lb[z:z + 1, cols], z == 0))
        pre = dict(zip([s[:3] for s in slots], _hgrn_precompute(entries, r_i, c_i)))

        st = [st_ref[c] for c in range(2 * groups)]
        inter = {key: [None] * n_chunks for key in pre}
        for u in range(unroll):
            for step in range(n_chunks):
                for z in range(2):
                    for grp in range(groups):
                        q_in, _, upd, dec = pre[(z, u, grp)]
                        ch = z * groups + grp
                        c = step if z == 0 else n_chunks - 1 - step
                        lo = c * HGRN_CHUNK
                        inter[(z, u, grp)][c] = _bdot(q_in[lo:lo + HGRN_CHUNK], st[ch], NT_DIMS)
                        st[ch] = st[ch] * dec[lo:lo + 1] + upd[c * LANES:(c + 1) * LANES]
        for ch in range(2 * groups):
            st_ref[ch] = st[ch]
        outs = {key: pre[key][1] + jnp.concatenate(inter[key], axis=0) for key in pre}

        if visit == "both":
            for t in range(n_tiles):
                for grp in range(groups):
                    finish(pl.ds(t * LANES, LANES), lanes_of(grp),
                           outs[(0, t, grp)] + outs[(1, n_tiles - 1 - t, grp)])
            return
        for z, u, grp, t in slots:
            rows, cols = pl.ds(pl.multiple_of(t * LANES, LANES), LANES), lanes_of(grp)
            if visit == "first":
                osum_ref[rows, cols] = outs[(z, u, grp)]
            else:
                finish(rows, cols, osum_ref[rows, cols] + outs[(z, u, grp)])

    if n_tiles == unroll:
        block(0, "both")
    else:
        def body(i, carry, visit):
            block(i, visit)
            return carry

        half = n_tiles // 2 // unroll
        lax.fori_loop(0, half, functools.partial(body, visit="first"), 0)
        lax.fori_loop(half, 2 * half, functools.partial(body, visit="second"), 0)
    if want_state:
        for z in range(2):
            for grp in range(groups):
                sfin_ref[z, grp] = st_ref[z * groups + grp].T


def _hgrn_rec(proj, lb, norm_g, s0, prev_out, *, n_seq, seq_len, row_block0, n_heads, total_rows,
              states_out=None, layer=0, n_layers=1):
    d = n_heads * LANES
    groups = 2 if seq_len // LANES <= 2 else 1
    n_cols = n_heads // groups
    zero_init = s0 is None
    want_state = s0 is None
    aliased = prev_out is not None

    def col(block):
        return lambda b, h: (row_block0 + b, block * n_cols + h)

    blk = (seq_len, groups * LANES)
    in_specs = [pl.BlockSpec(blk, col(0)), pl.BlockSpec(blk, col(1)), pl.BlockSpec(blk, col(2)),
                pl.BlockSpec(blk, col(3)), pl.BlockSpec(blk, col(4)), pl.BlockSpec(blk, col(5)),
                pl.BlockSpec(blk, col(6)),
                pl.BlockSpec((2, groups * LANES), lambda b, h: (0, h)),
                pl.BlockSpec((1, groups * LANES), lambda b, h: (0, h))]
    args = [proj] * 7 + [lb, norm_g.reshape(1, d)]
    state_spec = pl.BlockSpec((None, 2, groups, LANES, LANES), lambda b, h: (b, 0, h, 0, 0))
    if not zero_init:
        in_specs.append(state_spec)
        args.append(s0)
    io_alias = {}
    if aliased:
        in_specs.append(pl.BlockSpec(memory_space=pl.ANY))
        io_alias[len(args)] = 0
        args.append(prev_out)
    if states_out is not None:
        in_specs.append(pl.BlockSpec(memory_space=pl.ANY))
        io_alias[len(args)] = 1
        args.append(states_out)
    out_specs = [pl.BlockSpec(blk, lambda b, h: (row_block0 + b, h))]
    out_shape = [jax.ShapeDtypeStruct((total_rows, d), BF16)]
    if want_state:
        out_specs.append(pl.BlockSpec((None, None, 2, groups, LANES, LANES), lambda b, h: (b, layer, 0, h, 0, 0)))
        out_shape.append(jax.ShapeDtypeStruct((n_seq, n_layers, 2, n_heads, LANES, LANES), F32))
    res = pl.pallas_call(
        functools.partial(_hgrn_rec_kernel, seq_len=seq_len, groups=groups, zero_init=zero_init,
                          want_state=want_state, aliased=len(io_alias)),
        grid=(n_seq, n_cols),
        in_specs=in_specs,
        out_specs=out_specs,
        out_shape=out_shape,
        scratch_shapes=[pltpu.VMEM((seq_len, groups * LANES), F32),
                        pltpu.VMEM((2 * groups, LANES, LANES), F32)],
        input_output_aliases=io_alias,
        compiler_params=pltpu.CompilerParams(
            dimension_semantics=("parallel", "parallel"), vmem_limit_bytes=VMEM_LIMIT),
        name="hgrn_rec",
    )(*args)
    return res


def _int_mod(x, n):
    return x & (n - 1) if n & (n - 1) == 0 else lax.rem(x, n)


def _token_shift(x_ref, xp_ref, xn_ref, mod_ref, tile_start, n_prompt_rows, prompt_len, sample_len):
    m = mod_ref[...]
    sh, sc = m[0:1], 1.0 + m[1:2]
    h = x_ref[...] * sc + sh
    h_before = xp_ref[7:8, :] * sc + sh
    h_after = xn_ref[0:1, :] * sc + sh
    tm = h.shape[0]
    rr = lax.broadcasted_iota(jnp.int32, (tm, 1), 0)
    grow = tile_start + rr
    in_prompt = grow < n_prompt_rows
    pos = jnp.where(in_prompt, _int_mod(grow, prompt_len), _int_mod(grow - n_prompt_rows, sample_len))
    last = jnp.where(in_prompt, prompt_len - 1, sample_len - 1)
    prev = jnp.where(rr == 0, h_before, pltpu.roll(h, 1, 0))
    prev = jnp.where(pos == 0, 0.0, prev)
    nxt = jnp.where(rr == tm - 1, h_after, pltpu.roll(h, tm - 1, 0))
    nxt = jnp.where(pos == last, 0.0, nxt)
    return h, 0.5 * (prev + nxt) - h


def _rwkv_rkv_kernel(x_ref, xp_ref, xn_ref, mod_ref, mu_ref, w_ref, o_ref, h_ref, xx_ref, *, tm, seq_info):
    @pl.when(pl.program_id(1) == 0)
    def _():
        h, xx = _token_shift(x_ref, xp_ref, xn_ref, mod_ref, pl.program_id(0) * tm, *seq_info)
        h_ref[...] = h
        xx_ref[...] = xx

    xs = h_ref[...] + xx_ref[...] * mu_ref[...]
    o_ref[...] = _dot(xs.astype(BF16), w_ref[pl.program_id(1)])


def _halo_specs(tm, d, t):
    nb = t // 8

    def before(i, *_):
        return (jnp.maximum(i * (tm // 8) - 1, 0), 0)

    def after(i, *_):
        return (jnp.minimum((i + 1) * (tm // 8), nb - 1), 0)

    return pl.BlockSpec((8, d), before), pl.BlockSpec((8, d), after)


def _rwkv_rkv(x, mods, mu3, w_rkv, seg, *, n_prompt_rows, prompt_len, sample_len, tm=512):
    t, d = x.shape
    n = w_rkv.shape[2]
    before, after = _halo_specs(tm, d, t)
    return pl.pallas_call(
        functools.partial(_rwkv_rkv_kernel, tm=tm, seq_info=(n_prompt_rows, prompt_len, sample_len)),
        grid=(t // tm, 3),
        in_specs=[
            pl.BlockSpec((tm, d), lambda i, j: (i, 0)),
            before, after,
            pl.BlockSpec((None, 6, d), lambda i, j: (seg(i, tm), 0, 0)),
            pl.BlockSpec((None, 1, d), lambda i, j: (j, 0, 0)),
            pl.BlockSpec((3, d, n), lambda i, j: (0, 0, 0)),
        ],
        out_specs=pl.BlockSpec((None, tm, n), lambda i, j: (j, i, 0)),
        out_shape=jax.ShapeDtypeStruct((3, t, n), F32),
        scratch_shapes=[pltpu.VMEM((tm, d), F32), pltpu.VMEM((tm, d), F32)],
        compiler_params=pltpu.CompilerParams(
            dimension_semantics=("parallel", "arbitrary"), vmem_limit_bytes=VMEM_LIMIT),
        name="rwkv_rkv",
    )(x, x, x, mods, mu3, w_rkv)


def _rwkv_lora_kernel(x_ref, xp_ref, xn_ref, mod_ref, mu_ref, wla_ref, wlb_ref, w0_ref, ala_ref, alb_ref,
                      a0_ref, gla_ref, glb_ref, lw_ref, a_ref, g_ref, *, tm, seq_info):
    h, xx = _token_shift(x_ref, xp_ref, xn_ref, mod_ref, pl.program_id(0) * tm, *seq_info)
    mu = mu_ref[...]
    xs_w = (h + xx * mu[1:2]).astype(BF16)
    xs_a = (h + xx * mu[4:5]).astype(BF16)
    xs_g = (h + xx * mu[5:6]).astype(BF16)
    zw = w0_ref[...] + _bdot(jnp.tanh(_dot(xs_w, wla_ref[...])), wlb_ref[...])
    lw_ref[...] = -DECAY_SCALE * _sigmoid(zw)
    za = a0_ref[...] + _bdot(_dot(xs_a, ala_ref[...]), alb_ref[...])
    a_ref[...] = _sigmoid(za)
    g_ref[...] = _bdot(_sigmoid(_dot(xs_g, gla_ref[...])), glb_ref[...])


def _rwkv_lora(x, mods, mu, weights, seg, *, n_prompt_rows, prompt_len, sample_len, tm=512):
    t, d = x.shape
    before, after = _halo_specs(tm, d, t)

    def whole(arr):
        return pl.BlockSpec(arr.shape, lambda i: (0,) * arr.ndim)

    return pl.pallas_call(
        functools.partial(_rwkv_lora_kernel, tm=tm, seq_info=(n_prompt_rows, prompt_len, sample_len)),
        grid=(t // tm,),
        in_specs=[pl.BlockSpec((tm, d), lambda i: (i, 0)), before, after,
                  pl.BlockSpec((None, 6, d), lambda i: (seg(i, tm), 0, 0)), whole(mu)]
                 + [whole(w) for w in weights],
        out_specs=[pl.BlockSpec((tm, 2 * d), lambda i: (i, 0)),
                   pl.BlockSpec((tm, 2 * d), lambda i: (i, 0)),
                   pl.BlockSpec((tm, d), lambda i: (i, 0))],
        out_shape=[jax.ShapeDtypeStruct((t, 2 * d), F32), jax.ShapeDtypeStruct((t, 2 * d), F32),
                   jax.ShapeDtypeStruct((t, d), F32)],
        compiler_params=pltpu.CompilerParams(
            dimension_semantics=("parallel",), vmem_limit_bytes=VMEM_LIMIT),
        name="rwkv_lora",
    )(x, x, x, mods, mu, *weights)


def _rwkv_precompute(entries, consts):
    bd, eye, head_masks, r_i, c_i = consts
    n = len(entries)
    pre = []
    for r, k, v, lw, a, (kkp, kap, rkp), fwd in entries:
        tri = ((c_i <= r_i) if fwd else (c_i >= r_i)).astype(F32)
        kk = k * kkp
        kk = kk / jnp.maximum(jnp.sqrt(_head_sums(kk * kk, head_masks)), 1e-12)
        k2 = k * (1.0 + (a - 1.0) * kap)
        bonus = _head_sums(r * k2 * rkp, head_masks) * v
        b = kk * a
        cl = _dot_x2r(tri, lw)
        cle = cl - lw
        cm = cl[LANES // 2:LANES // 2 + 1]
        ct = cl[LANES - 1:LANES] if fwd else cl[0:1]
        e_inv = jnp.exp(cm - cl)
        e_out = jnp.exp(ct - cl)
        pre.append(dict(
            v=v, bonus=bonus, fwd=fwd,
            kkt=kk * jnp.exp(cle - cm), rt=r * jnp.exp(cl - cm), kh=k2 * e_inv, bh=b * e_inv,
            kkd=kk * jnp.exp(cle), rd=r * jnp.exp(cl), kg=k2 * e_out, bg=b * e_out, gc=jnp.exp(ct)))

    for e in pre:
        lhs = jnp.concatenate([e["kkt"] * head_masks[0], e["kkt"] * head_masks[1],
                               e["rt"] * head_masks[0], e["rt"] * head_masks[1]], axis=0)
        e["gk"] = _dot_x3(lhs, e["kh"], NT_DIMS)
        e["gb"] = _dot_x3(lhs, e["bh"], NT_DIMS)

    chains = []
    for e in pre:
        strict = (c_i < r_i) if e["fwd"] else (c_i > r_i)
        incl = (c_i <= r_i) if e["fwd"] else (c_i >= r_i)
        for hd in range(2):
            lo = hd * LANES
            chains.append(dict(
                e=e, hd=hd,
                a_k=jnp.where(strict, e["gk"][lo:lo + LANES], 0.0),
                a_b=jnp.where(strict, e["gb"][lo:lo + LANES], 0.0),
                b_k=jnp.where(incl, e["gk"][2 * LANES + lo:3 * LANES + lo], 0.0),
                b_b=jnp.where(incl, e["gb"][2 * LANES + lo:3 * LANES + lo], 0.0)))

    for c in chains:
        c["m"] = eye - jnp.where((r_i >> 1) == (c_i >> 1), c["a_b"], 0.0)
    for log_k in range(1, 7):
        k = 1 << log_k
        off = ((r_i >> (log_k + 1)) == (c_i >> (log_k + 1))) & ((r_i >> log_k) != (c_i >> log_k))
        if k < 8:
            for c in chains:
                c["t"] = _dot_x3(jnp.where(off, c["a_b"], 0.0), c["m"])
            for c in chains:
                c["m"] = c["m"] - _dot_x3(c["m"], c["t"])
            continue

        def take(x, fwd):
            first = k if fwd else 0
            return jnp.concatenate([x[lo:lo + k] for lo in range(first, LANES, 2 * k)], axis=0)

        def spread(rows, fwd):
            zero = jnp.zeros((k, LANES), F32)
            parts = []
            for j in range(LANES // (2 * k)):
                piece = rows[j * k:(j + 1) * k]
                parts += [zero, piece] if fwd else [piece, zero]
            return jnp.concatenate(parts, axis=0)

        for c in chains:
            fwd = c["e"]["fwd"]
            c["t"] = spread(_dot_x3(take(jnp.where(off, c["a_b"], 0.0), fwd), c["m"]), fwd)
        for c in chains:
            fwd = c["e"]["fwd"]
            c["m"] = c["m"] - spread(_dot_x3(take(c["m"], fwd), c["t"]), fwd)

    for c in chains:
        e = c["e"]
        c["v_h"] = e["v"] * head_masks[c["hd"]]
        c["kkd_h"] = e["kkd"] * head_masks[c["hd"]]
        c["akv"] = _dot_x3(c["a_k"], c["v_h"])
    for c in chains:
        c["wt"] = _dot_x3(c["m"], c["kkd_h"])
        c["u0"] = _dot_x3(c["m"], c["akv"])
    for c in chains:
        c["y0"] = _dot_x3(c["b_k"], c["v_h"]) - _dot_x3(c["b_b"], c["u0"])
        c["rp"] = _dot_x3(c["b_b"], c["wt"])

    out = []
    for i, e in enumerate(pre):
        c0, c1 = chains[2 * i], chains[2 * i + 1]
        wt = c0["wt"] + c1["wt"]
        u0 = c0["u0"] + c1["u0"]
        q0 = bd * (_dot_x3(e["v"], e["kg"], TN_DIMS) - _dot_x3(u0.T, e["bg"]))
        p_mat = _dot_x3(wt, e["bg"], TN_DIMS)
        out.append((e["rd"] - c0["rp"] - c1["rp"], c0["y0"] + c1["y0"], e["gc"], q0, p_mat, e["bonus"]))
    assert len(out) == n
    return out


def _rwkv_rec_kernel(*refs, seq_len, groups, zero_init, want_state, aliased):
    data = (refs[0:5], refs[5:10])
    g_ref, kk_ref, ka_ref, rk_ref, gg_ref, gb_ref = refs[10:16]
    pos = 16
    s0_ref = None
    if not zero_init:
        s0_ref = refs[pos]
        pos += 1
    pos += aliased
    o_ref = refs[pos]
    pos += 1
    sfin_ref = None
    if want_state:
        sfin_ref = refs[pos]
        pos += 1
    osum_ref, st_ref = refs[pos:pos + 2]

    n_tiles = seq_len // LANES
    unroll = min(8 // (2 * groups), n_tiles)
    assert n_tiles % unroll == 0 and (n_tiles == unroll or (n_tiles // 2) % unroll == 0)
    r_i = lax.broadcasted_iota(jnp.int32, (LANES, LANES), 0)
    c_i = lax.broadcasted_iota(jnp.int32, (LANES, LANES), 1)
    bd = ((r_i >> 6) == (c_i >> 6)).astype(F32)
    eye = (r_i == c_i).astype(F32)
    lane = lax.broadcasted_iota(jnp.int32, (1, LANES), 1)
    head_masks = ((lane < RWKV_HEAD).astype(F32), (lane >= RWKV_HEAD).astype(F32))
    consts = (bd, eye, head_masks, r_i, c_i)
    inv_n = 1.0 / RWKV_HEAD

    def lanes_of(grp):
        return slice(grp * LANES, (grp + 1) * LANES)

    for z in range(2):
        for grp in range(groups):
            if zero_init:
                st_ref[z * groups + grp] = jnp.zeros((LANES, LANES), F32)
            else:
                zero = jnp.zeros((RWKV_HEAD, RWKV_HEAD), F32)
                st_ref[z * groups + grp] = jnp.concatenate(
                    [jnp.concatenate([s0_ref[z, 2 * grp], zero], axis=1),
                     jnp.concatenate([zero, s0_ref[z, 2 * grp + 1]], axis=1)], axis=0)

    def block(i, visit):
        slots = []
        for z in range(2):
            for grp in range(groups):
                for u in range(unroll):
                    t = i * unroll + u
                    slots.append((z, u, grp, t if z == 0 else n_tiles - 1 - t))
        entries = []
        for z, _, grp, t in slots:
            rows = pl.ds(pl.multiple_of(t * LANES, LANES), LANES)
            cols = lanes_of(grp)
            r_ref, k_ref, v_ref, lw_ref, a_ref = data[z]
            params = (kk_ref[z:z + 1, cols], ka_ref[z:z + 1, cols], rk_ref[z:z + 1, cols])
            entries.append((r_ref[rows, cols], k_ref[rows, cols], v_ref[rows, cols], lw_ref[rows, cols],
                            a_ref[rows, cols], params, z == 0))
        pre = _rwkv_precompute(entries, consts)

        ys = []
        st = [st_ref[c] for c in range(2 * groups)]
        for (z, _, grp, _), (rp, y0, gc, q0, p_mat, _) in zip(slots, pre):
            c = z * groups + grp
            ys.append(_dot_x3(rp, st[c], NT_DIMS) + y0)
            st[c] = st[c] * gc + q0 - bd * _dot_x3(st[c], p_mat)
        for c in range(2 * groups):
            st_ref[c] = st[c]

        outs = {}
        for (z, u, grp, _), y, (_, _, _, _, _, bonus) in zip(slots, ys, pre):
            cols = lanes_of(grp)
            mean = _head_sums(y, head_masks) * inv_n
            yc = y - mean
            var = _head_sums(yc * yc, head_masks) * inv_n
            outs[(z, u, grp)] = (yc * lax.rsqrt(var + GN_EPS) * gg_ref[z:z + 1, cols] + gb_ref[z:z + 1, cols]
                                 + bonus)

        if visit == "both":
            for t in range(n_tiles):
                for grp in range(groups):
                    rows, cols = pl.ds(t * LANES, LANES), lanes_of(grp)
                    total = outs[(0, t, grp)] + outs[(1, n_tiles - 1 - t, grp)]
                    o_ref[rows, cols] = (total * g_ref[rows, cols]).astype(o_ref.dtype)
            return
        for z, u, grp, t in slots:
            out = outs[(z, u, grp)]
            rows, cols = pl.ds(pl.multiple_of(t * LANES, LANES), LANES), lanes_of(grp)
            if visit == "first":
                osum_ref[rows, cols] = out
            else:
                o_ref[rows, cols] = ((osum_ref[rows, cols] + out) * g_ref[rows, cols]).astype(o_ref.dtype)

    if n_tiles == unroll:
        block(0, "both")
    else:
        def body(i, carry, visit):
            block(i, visit)
            return carry

        half = n_tiles // 2 // unroll
        lax.fori_loop(0, half, functools.partial(body, visit="first"), 0)
        lax.fori_loop(half, 2 * half, functools.partial(body, visit="second"), 0)
    if want_state:
        for z in range(2):
            for grp in range(groups):
                st = st_ref[z * groups + grp]
                sfin_ref[z, 2 * grp] = st[:RWKV_HEAD, :RWKV_HEAD]
                sfin_ref[z, 2 * grp + 1] = pltpu.roll(st, RWKV_HEAD, 1)[RWKV_HEAD:, :RWKV_HEAD]


def _rwkv_rec(rkv, lw, a, g, params, s0, prev_out, *, n_seq, seq_len, row_block0, total_rows,
              states_out=None, layer=0, n_layers=1):
    d = g.shape[1]
    groups = 2 if seq_len // LANES <= 2 else 1
    n_cols = d // (groups * LANES)
    zero_init = s0 is None
    want_state = s0 is None
    aliased = prev_out is not None
    blk = (seq_len, groups * LANES)

    def dir_specs(z):
        col = lambda b, p: (row_block0 + b, z * n_cols + p)
        return ([pl.BlockSpec((None,) + blk, lambda b, p, j=j: (j, row_block0 + b, z * n_cols + p))
                 for j in range(3)] + [pl.BlockSpec(blk, col), pl.BlockSpec(blk, col)])

    in_specs = dir_specs(0) + dir_specs(1)
    in_specs.append(pl.BlockSpec(blk, lambda b, p: (row_block0 + b, p)))
    in_specs += [pl.BlockSpec((2, groups * LANES), lambda b, p: (0, p)) for _ in range(5)]
    args = [rkv, rkv, rkv, lw, a] * 2 + [g] + list(params)
    if not zero_init:
        in_specs.append(pl.BlockSpec((None, 2, 2 * groups, RWKV_HEAD, RWKV_HEAD), lambda b, p: (b, 0, p, 0, 0)))
        args.append(s0)
    io_alias = {}
    if aliased:
        in_specs.append(pl.BlockSpec(memory_space=pl.ANY))
        io_alias[len(args)] = 0
        args.append(prev_out)
    if states_out is not None:
        in_specs.append(pl.BlockSpec(memory_space=pl.ANY))
        io_alias[len(args)] = 1
        args.append(states_out)
    out_specs = [pl.BlockSpec(blk, lambda b, p: (row_block0 + b, p))]
    out_shape = [jax.ShapeDtypeStruct((total_rows, d), BF16)]
    if want_state:
        out_specs.append(pl.BlockSpec((None, None, 2, 2 * groups, RWKV_HEAD, RWKV_HEAD),
                                      lambda b, p: (b, layer, 0, p, 0, 0)))
        out_shape.append(jax.ShapeDtypeStruct((n_seq, n_layers, 2, d // RWKV_HEAD, RWKV_HEAD, RWKV_HEAD), F32))
    return pl.pallas_call(
        functools.partial(_rwkv_rec_kernel, seq_len=seq_len, groups=groups, zero_init=zero_init,
                          want_state=want_state, aliased=len(io_alias)),
        grid=(n_seq, n_cols),
        in_specs=in_specs,
        out_specs=out_specs,
        out_shape=out_shape,
        scratch_shapes=[pltpu.VMEM((seq_len, groups * LANES), F32),
                        pltpu.VMEM((2 * groups, LANES, LANES), F32)],
        input_output_aliases=io_alias,
        compiler_params=pltpu.CompilerParams(
            dimension_semantics=("parallel", "parallel"), vmem_limit_bytes=VMEM_LIMIT),
        name="rwkv_rec",
    )(*args)


def _rwkv_prepare_weights(mu, w_rkv, w0, w_la, w_lb, a0, a_la, a_lb, g_la, g_lb):
    d = mu.shape[1]
    rank_w = w_la.shape[2]
    rank_a = a_la.shape[2]
    rank_g = g_la.shape[1]
    rank_g_pad = -(-rank_g // LANES) * LANES

    def block_diag(w):
        rank = w.shape[1]
        out = jnp.zeros((2, rank, 2, d), w.dtype)
        out = out.at[0, :, 0, :].set(w[0]).at[1, :, 1, :].set(w[1])
        return out.reshape(2 * rank, 2 * d)

    lora = (
        w_la.reshape(d, 2 * rank_w).astype(BF16), block_diag(w_lb).astype(BF16), w0.reshape(1, 2 * d),
        a_la.reshape(d, 2 * rank_a).astype(BF16), block_diag(a_lb).astype(BF16), a0.reshape(1, 2 * d),
        jnp.pad(g_la, ((0, 0), (0, rank_g_pad - rank_g))).astype(BF16),
        jnp.pad(g_lb, ((0, rank_g_pad - rank_g), (0, 0))).astype(BF16),
    )
    mu3 = jnp.stack([mu[0], mu[2], mu[3]]).reshape(3, 1, d)
    return {"mu3": mu3, "w_rkv": w_rkv.astype(BF16), "lora": lora}


def _embed_kernel(xp_ref, xs_ref, row_ref, col_ref, o_ref, *, n_prompt_tiles):
    i = pl.program_id(0)

    @pl.when(i < n_prompt_tiles)
    def _():
        o_ref[...] = xp_ref[...]

    @pl.when(i >= n_prompt_tiles)
    def _():
        half = o_ref.shape[1] // 2
        for grp in range(o_ref.shape[0] // GRID_W):
            rows = pl.ds(grp * GRID_W, GRID_W)
            o_ref[rows, :half] = xs_ref[rows, :half] + row_ref[grp:grp + 1, :]
            o_ref[rows, half:] = xs_ref[rows, half:] + col_ref[...]


def _grid_pos_tables(n_tokens, d):
    quarter = d // 4
    omega = 1.0 / (POS_BASE ** (jnp.arange(quarter, dtype=F32) / quarter))
    r = jnp.arange(n_tokens // GRID_W, dtype=F32)[:, None] * omega
    cc = jnp.arange(GRID_W, dtype=F32)[:, None] * omega
    return (jnp.concatenate([jnp.sin(r), jnp.cos(r)], -1), jnp.concatenate([jnp.sin(cc), jnp.cos(cc)], -1))


def _embed(xp, xs, sample_len, tm=512):
    n_p, d = xp.shape
    n_s = xs.shape[0]
    npt = n_p // tm
    pos_tiles = sample_len // tm
    grid_rows = tm // GRID_W
    row_emb, col_emb = _grid_pos_tables(sample_len, d)
    return pl.pallas_call(
        functools.partial(_embed_kernel, n_prompt_tiles=npt),
        grid=((n_p + n_s) // tm,),
        in_specs=[
            pl.BlockSpec((tm, d), lambda i: (jnp.minimum(i, npt - 1), 0)),
            pl.BlockSpec((tm, d), lambda i: (jnp.maximum(i - npt, 0), 0)),
            pl.BlockSpec((grid_rows, d // 2), lambda i: (lax.rem(jnp.maximum(i - npt, 0), pos_tiles), 0)),
            pl.BlockSpec((GRID_W, d // 2), lambda i: (0, 0)),
        ],
        out_specs=pl.BlockSpec((tm, d), lambda i: (i, 0)),
        out_shape=jax.ShapeDtypeStruct((n_p + n_s, d), F32),
        compiler_params=pltpu.CompilerParams(
            dimension_semantics=("parallel",), vmem_limit_bytes=VMEM_LIMIT),
        name="embed",
    )(xp, xs, row_emb, col_emb)


def kernel(x_prompt, x_sample, state_hgrn, state_rwkv, c, c_ctx, ada_w, ada_b, ln_g, ln_b, ffn_w_up, ffn_w_down, hgrn_w_in, hgrn_lb, hgrn_norm_g, hgrn_w_o, rwkv_mu, rwkv_w_rkv, rwkv_w0, rwkv_w_la, rwkv_w_lb, rwkv_a0, rwkv_a_la, rwkv_a_lb, rwkv_g_la, rwkv_g_lb, rwkv_k_k, rwkv_k_a, rwkv_r_k, rwkv_gn_g, rwkv_gn_b, rwkv_w_o):
    n_b, l_p, d = x_prompt.shape
    n_s, l_s, _ = x_sample.shape
    depth = ada_w.shape[0]
    n_p_rows = n_b * l_p
    total = n_p_rows + n_s * l_s
    assert n_p_rows % l_s == 0 and l_p % LANES == 0 and l_s % LANES == 0
    alpha = (2 * depth) ** 0.25
    a_heads = d // LANES

    def seg(i, tm):
        return _seg_index(i, tm, n_p_rows, l_s)

    x = _embed(x_prompt.reshape(n_p_rows, d), x_sample.reshape(n_s * l_s, d), l_s)

    cond8 = jnp.zeros((8, d), F32).at[0].set(c_ctx).at[1:1 + n_s].set(c)
    mods = _adaln(cond8, ada_w, ada_b).reshape(depth, 8, 6, d)

    lb_soft = jax.nn.softmax(hgrn_lb.astype(F32), axis=0)
    lower_bounds = jnp.cumsum(lb_soft, axis=0) - lb_soft[0]

    new_hgrn = None
    new_rwkv = None
    n_mix = (depth + 1) // 2, depth // 2
    for l in range(depth):
        j = l // 2
        if l % 2 == 0:
            proj = _modmm(x, mods[l], hgrn_w_in[j].astype(BF16), seg)
            o, new_hgrn = _hgrn_rec(proj, lower_bounds[j], hgrn_norm_g[j], None, None, n_seq=n_b, seq_len=l_p,
                                    row_block0=0, n_heads=a_heads, total_rows=total,
                                    states_out=new_hgrn, layer=j, n_layers=n_mix[0])
            (o,) = _hgrn_rec(proj, lower_bounds[j], hgrn_norm_g[j], state_hgrn[:, j], o, n_seq=n_s, seq_len=l_s,
                             row_block0=n_p_rows // l_s, n_heads=a_heads, total_rows=total)
            w_o = hgrn_w_o[j]
        else:
            prep = _rwkv_prepare_weights(rwkv_mu[j], rwkv_w_rkv[j], rwkv_w0[j], rwkv_w_la[j], rwkv_w_lb[j],
                                         rwkv_a0[j], rwkv_a_la[j], rwkv_a_lb[j], rwkv_g_la[j], rwkv_g_lb[j])
            seq_kw = dict(n_prompt_rows=n_p_rows, prompt_len=l_p, sample_len=l_s)
            rkv = _rwkv_rkv(x, mods[l], prep["mu3"], prep["w_rkv"], seg, **seq_kw)
            lw, a, g = _rwkv_lora(x, mods[l], rwkv_mu[j], prep["lora"], seg, **seq_kw)
            params = (rwkv_k_k[j], rwkv_k_a[j], rwkv_r_k[j], rwkv_gn_g[j], rwkv_gn_b[j])
            o, new_rwkv = _rwkv_rec(rkv, lw, a, g, params, None, None, n_seq=n_b, seq_len=l_p,
                                    row_block0=0, total_rows=total,
                                    states_out=new_rwkv, layer=j, n_layers=n_mix[1])
            (o,) = _rwkv_rec(rkv, lw, a, g, params, state_rwkv[:, j], o, n_seq=n_s, seq_len=l_s,
                             row_block0=n_p_rows // l_s, total_rows=total)
            w_o = rwkv_w_o[j]
        x = _post_mixer(o, x, mods[l], w_o.astype(BF16), ffn_w_up[l].astype(BF16), ffn_w_down[l].astype(BF16),
                        ln_g[l], ln_b[l], seg, alpha)

    y_prompt = x[:n_p_rows].reshape(n_b, l_p, d)
    y_sample = x[n_p_rows:].reshape(n_s, l_s, d)
    return (y_prompt, y_sample, new_hgrn, new_rwkv)
```
